```python
import numpy as np
import jax
import jax.numpy as jnp
from jax import lax

D_MODEL = 2048
BATCH = 4
SEQ = 8192
DEPTH = 4

CHUNK = 64
RET_WIDTH = D_MODEL // 2
RET_HEADS = 4
RET_HEAD_DIM = RET_WIDTH // RET_HEADS
ROPE_BASE = 10000.0
RET_GN_EPS = 1e-5
RWKV_WIDTH = D_MODEL - RET_WIDTH
RWKV_HEAD_DIM = 64
RWKV_HEADS = RWKV_WIDTH // RWKV_HEAD_DIM
RWKV_LN_EPS = 64e-5


def _rwkv_lora_rank(mult, power):
    return max(32, int(round(mult * RWKV_WIDTH ** power / 32)) * 32)


DECAY_RANK = _rwkv_lora_rank(1.8, 0.5)
ICLR_RANK = _rwkv_lora_rank(1.8, 0.5)
VRES_RANK = _rwkv_lora_rank(1.3, 0.5)
GATE_RANK = _rwkv_lora_rank(0.6, 0.8)
RET_IN = 4 * RET_WIDTH
RWKV_IN = 3 * RWKV_WIDTH + DECAY_RANK + ICLR_RANK + GATE_RANK
EVEN_IN = RET_IN + RWKV_IN
LRU_WIDTH = D_MODEL
LRU_BLOCK = 256
LRU_BLOCKS = LRU_WIDTH // LRU_BLOCK
CONV_WIDTH = 4
LRU_C = 8.0
D_FF = 4 * D_MODEL
NORM_EPS = 1e-6
N_EVEN = (DEPTH + 1) // 2
N_ODD = DEPTH // 2

kernel_name = 'hybrid_retention_rwkv7_rglru_trunk'


def _rmsnorm(x, g):
    x32 = x.astype(jnp.float32)
    y = x32 * lax.rsqrt(jnp.mean(x32 * x32, axis=-1, keepdims=True) + NORM_EPS)
    return (y * g.astype(jnp.float32)).astype(x.dtype)


def _head_norm(y, eps):
    mu = jnp.mean(y, axis=-1, keepdims=True)
    var = jnp.mean(jnp.square(y - mu), axis=-1, keepdims=True)
    return (y - mu) * lax.rsqrt(var + eps)


def _split(z, sizes):
    return jnp.split(z, np.cumsum(sizes)[:-1].tolist(), axis=-1)


def _time_shift(z):
    return jnp.pad(z, ((0, 0), (1, 0), (0, 0)))[:, :-1]


def _rotary(t):
    T, d = t.shape[1], t.shape[-1]
    inv = 1.0 / (ROPE_BASE ** (jnp.arange(0, d, 2, dtype=jnp.float32) / d))
    ang = jnp.arange(T, dtype=jnp.float32)[:, None] * inv[None, :]
    cos = jnp.cos(ang)[None, :, None, :]
    sin = jnp.sin(ang)[None, :, None, :]
    t1, t2 = jnp.split(t, 2, axis=-1)
    return jnp.concatenate([t1 * cos - t2 * sin, t2 * cos + t1 * sin], axis=-1)


def _to_chunks(t):
    B, T, H, d = t.shape
    return t.reshape(B, T // CHUNK, CHUNK, H, d).transpose(1, 0, 3, 2, 4)


def _from_chunks(t):
    Nc, B, H, C, d = t.shape
    return t.transpose(1, 0, 3, 2, 4).reshape(B, Nc * C, H, d)


def _retention(q, k, v):
    B, _, H, dk = q.shape
    dv = v.shape[-1]
    log_g = jnp.log1p(-jnp.exp2(-5.0 - jnp.arange(H, dtype=jnp.float32)))
    pos = jnp.arange(CHUNK, dtype=jnp.float32)
    intra_decay = jnp.exp(log_g[:, None, None] * jnp.abs(pos[:, None] - pos[None, :]))
    q_decay = jnp.exp(log_g[:, None] * (pos + 1.0))[:, :, None]
    k_decay = jnp.exp(log_g[:, None] * (CHUNK - 1.0 - pos))[:, :, None]
    chunk_decay = jnp.exp(log_g * CHUNK)[:, None, None]

    def step(state, qkv):
        qc, kc, vc = qkv
        scores = jnp.einsum('bhnd,bhmd->bhnm', qc, kc) * intra_decay
        out = (jnp.einsum('bhnm,bhme->bhne', scores, vc)
               + jnp.einsum('bhnd,bhde->bhne', qc * q_decay, state))
        state = state * chunk_decay + jnp.einsum('bhmd,bhme->bhde', kc * k_decay, vc)
        return state, out

    s0 = jnp.zeros((B, H, dk, dv), jnp.float32)
    _, out = lax.scan(step, s0, (_to_chunks(q), _to_chunks(k), _to_chunks(v)))
    return _from_chunks(out)


def _rwkv7_scan(r, w, k, v, kk, a):
    B, T, H, N = r.shape

    def step(S, inp):
        r_t, w_t, k_t, v_t, kk_t, a_t = inp
        sa = jnp.einsum('bhvk,bhk->bhv', S, kk_t)
        S = (S * w_t[:, :, None, :]
             - jnp.einsum('bhv,bhk->bhvk', sa, kk_t * a_t)
             + jnp.einsum('bhv,bhk->bhvk', v_t, k_t))
        return S, jnp.einsum('bhvk,bhk->bhv', S, r_t)

    tm = lambda t: jnp.moveaxis(t, 1, 0)
    s0 = jnp.zeros((B, H, N, N), jnp.float32)
    _, y = lax.scan(step, s0, (tm(r), tm(w), tm(k), tm(v), tm(kk), tm(a)))
    return jnp.moveaxis(y, 0, 1)


def _even_mixer(h, w_in, w_out, mu, w0, w2, a0, a2, g2, k_k, k_a, r_k, lnx_w, lnx_b,
                v_first, v_res):
    B, T, _ = h.shape
    heads = lambda t, n: t.reshape(B, T, n, -1)
    z = (h @ w_in).astype(jnp.float32)
    z_ret, z_rw = z[..., :RET_IN], z[..., RET_IN:]

    q, k, v, g = _split(z_ret, [RET_WIDTH] * 4)
    q = _rotary(heads(q, RET_HEADS))
    k = _rotary(heads(k, RET_HEADS)) * (RET_HEAD_DIM ** -0.5)
    y_ret = _head_norm(_retention(q, k, heads(v, RET_HEADS)), RET_GN_EPS)
    out_ret = jax.nn.silu(g) * y_ret.reshape(B, T, RET_WIDTH)

    zs = z_rw + mu * (_time_shift(z_rw) - z_rw)
    r, kr, vr, wl, al, gl = _split(zs, [RWKV_WIDTH] * 3 + [DECAY_RANK, ICLR_RANK, GATE_RANK])
    decay = jnp.exp(-jnp.exp(-jax.nn.softplus(-(w0 + jnp.tanh(wl) @ w2)) - 0.5))
    iclr = jax.nn.sigmoid(a0 + al @ a2)
    gate = jax.nn.sigmoid(gl) @ g2
    if v_res is None:
        v_first = vr
    else:
        v0, v1, v2 = v_res
        vr = vr + (v_first - vr) * jax.nn.sigmoid(v0 + (vr @ v1) @ v2)
    kk = heads(kr * k_k, RWKV_HEADS)
    kk = kk / jnp.maximum(jnp.linalg.norm(kk, axis=-1, keepdims=True), 1e-12)
    kr = kr * (1.0 + (iclr - 1.0) * k_a)
    rh, kh, vh, wh, ah = [heads(t, RWKV_HEADS) for t in (r, kr, vr, decay, iclr)]
    y_rw = _rwkv7_scan(rh, wh, kh, vh, kk, ah)
    y_rw = _head_norm(y_rw, RWKV_LN_EPS).reshape(B, T, RWKV_WIDTH) * lnx_w + lnx_b
    bonus = (jnp.sum(rh * kh * r_k, axis=-1, keepdims=True) * vh).reshape(B, T, RWKV_WIDTH)
    out_rw = (y_rw + bonus) * gate

    mixed = jnp.concatenate([out_ret, out_rw], axis=-1).astype(h.dtype) @ w_out
    return mixed, v_first


def _linear_combine(left, right):
    a_l, b_l = left
    a_r, b_r = right
    return a_l * a_r, a_r * b_l + b_r


def _odd_mixer(h, w_in, conv_w, conv_b, gx_w, gx_b, ga_w, ga_b, lam, w_out):
    B, T, _ = h.shape
    z = (h @ w_in).astype(jnp.float32)
    y_br, x_br = jnp.split(z, 2, axis=-1)
    y_br = jax.nn.gelu(y_br, approximate=True)
    x_br = lax.conv_general_dilated(
        x_br, conv_w.astype(jnp.float32)[:, None, :], window_strides=(1,),
        padding=[(CONV_WIDTH - 1, 0)], dimension_numbers=('NWC', 'WIO', 'NWC'),
        feature_group_count=LRU_WIDTH) + conv_b
    xb = x_br.reshape(B, T, LRU_BLOCKS, LRU_BLOCK)
    gate_x = jax.nn.sigmoid(jnp.einsum('btni,nij->btnj', xb, gx_w).reshape(B, T, LRU_WIDTH) + gx_b)
    gate_a = jax.nn.sigmoid(jnp.einsum('btni,nij->btnj', xb, ga_w).reshape(B, T, LRU_WIDTH) + ga_b)
    log_a = -LRU_C * gate_a * jax.nn.softplus(-lam)
    a = jnp.exp(log_a)
    b = x_br * gate_x * jnp.sqrt(-jnp.expm1(2.0 * log_a))
    _, hs = lax.associative_scan(_linear_combine, (a, b), axis=1)
    return (y_br * hs).astype(h.dtype) @ w_out


def _sq_relu_mlp(h, w1, w2):
    return jnp.square(jax.nn.relu(h @ w1)) @ w2


def setup_inputs(seed: int = 0) -> dict:
    key = jax.random.key(seed)
    ks = iter(jax.random.split(key, 48))
    f32 = jnp.float32

    def normal(shape, scale):
        return jax.random.normal(next(ks), shape, f32) * scale

    def uniform(shape, lo, hi):
        return jax.random.uniform(next(ks), shape, f32, lo, hi)

    def gain(shape):
        return 1.0 + normal(shape, 0.02)

    NE1 = N_EVEN - 1
    ramp = jnp.linspace(0.0, 1.0, RWKV_WIDTH, dtype=f32)
    s = uniform((N_ODD, LRU_WIDTH), 0.9, 0.999) ** (1.0 / LRU_C)
    return {
        'x': normal((BATCH, SEQ, D_MODEL), 1.0),
        'ev_norm': gain((N_EVEN, D_MODEL)),
        'ev_w_in': normal((N_EVEN, D_MODEL, EVEN_IN), D_MODEL ** -0.5),
        'ev_w_out': normal((N_EVEN, RET_WIDTH + RWKV_WIDTH, D_MODEL), (RET_WIDTH + RWKV_WIDTH) ** -0.5),
        'rw_mu': uniform((N_EVEN, RWKV_IN), 0.2, 0.8),
        'rw_w0': -6.5 + 5.0 * ramp ** 0.85 + normal((N_EVEN, RWKV_WIDTH), 0.1),
        'rw_w2': normal((N_EVEN, DECAY_RANK, RWKV_WIDTH), 0.5 * DECAY_RANK ** -0.5),
        'rw_a0': normal((N_EVEN, RWKV_WIDTH), 0.1),
        'rw_a2': normal((N_EVEN, ICLR_RANK, RWKV_WIDTH), 0.5 * ICLR_RANK ** -0.5),
        'rw_g2': normal((N_EVEN, GATE_RANK, RWKV_WIDTH), GATE_RANK ** -0.5),
        'rw_k_k': 0.85 + normal((N_EVEN, RWKV_WIDTH), 0.02),
        'rw_k_a': 1.0 + normal((N_EVEN, RWKV_WIDTH), 0.02),
        'rw_r_k': -0.04 + normal((N_EVEN, RWKV_HEADS, RWKV_HEAD_DIM), 0.1),
        'rw_lnx_w': gain((N_EVEN, RWKV_WIDTH)),
        'rw_lnx_b': normal((N_EVEN, RWKV_WIDTH), 0.02),
        'rw_v0': 1.0 + normal((NE1, RWKV_WIDTH), 0.1),
        'rw_v1': normal((NE1, RWKV_WIDTH, VRES_RANK), RWKV_WIDTH ** -0.5),
        'rw_v2': normal((NE1, VRES_RANK, RWKV_WIDTH), 0.5 * VRES_RANK ** -0.5),
        'od_norm': gain((N_ODD, D_MODEL)),
        'od_w_in': normal((N_ODD, D_MODEL, 2 * LRU_WIDTH), D_MODEL ** -0.5),
        'od_conv_w': normal((N_ODD, CONV_WIDTH, LRU_WIDTH), CONV_WIDTH ** -0.5),
        'od_conv_b': normal((N_ODD, LRU_WIDTH), 0.02),
        'od_gx_w': normal((N_ODD, LRU_BLOCKS, LRU_BLOCK, LRU_BLOCK), LRU_BLOCK ** -0.5),
        'od_gx_b': normal((N_ODD, LRU_WIDTH), 0.02),
        'od_ga_w': normal((N_ODD, LRU_BLOCKS, LRU_BLOCK, LRU_BLOCK), LRU_BLOCK ** -0.5),
        'od_ga_b': normal((N_ODD, LRU_WIDTH), 0.02),
        'od_lam': jnp.log(s) - jnp.log1p(-s),
        'od_w_out': normal((N_ODD, LRU_WIDTH, D_MODEL), LRU_WIDTH ** -0.5),
        'ff_norm': gain((DEPTH, D_MODEL)),
        'ff_w1': normal((DEPTH, D_MODEL, D_FF), D_MODEL ** -0.5),
        'ff_w2': normal((DEPTH, D_FF, D_MODEL), D_FF ** -0.5),
        'final_norm': gain((D_MODEL,)),
    }


def reference(x, ev_norm, ev_w_in, ev_w_out, rw_mu, rw_w0, rw_w2, rw_a0, rw_a2, rw_g2,
              rw_k_k, rw_k_a, rw_r_k, rw_lnx_w, rw_lnx_b, rw_v0, rw_v1, rw_v2,
              od_norm, od_w_in, od_conv_w, od_conv_b, od_gx_w, od_gx_b, od_ga_w, od_ga_b,
              od_lam, od_w_out, ff_norm, ff_w1, ff_w2, final_norm):
    h = x
    v_first = None
    for layer in range(DEPTH):
        if layer % 2 == 0:
            e = layer // 2
            v_res = None if e == 0 else (rw_v0[e - 1], rw_v1[e - 1], rw_v2[e - 1])
            mix, v_first = _even_mixer(
                _rmsnorm(h, ev_norm[e]), ev_w_in[e], ev_w_out[e], rw_mu[e], rw_w0[e], rw_w2[e],
                rw_a0[e], rw_a2[e], rw_g2[e], rw_k_k[e], rw_k_a[e], rw_r_k[e],
                rw_lnx_w[e], rw_lnx_b[e], v_first, v_res)
        else:
            o = layer // 2
            mix = _odd_mixer(
                _rmsnorm(h, od_norm[o]), od_w_in[o], od_conv_w[o], od_conv_b[o],
                od_gx_w[o], od_gx_b[o], od_ga_w[o], od_ga_b[o], od_lam[o], od_w_out[o])
        h = h + mix
        h = h + _sq_relu_mlp(_rmsnorm(h, ff_norm[layer]), ff_w1[layer], ff_w2[layer])
    return _rmsnorm(h, final_norm)
```

```python
import functools
import math

import jax
import jax.numpy as jnp
from jax import lax
from jax.experimental import pallas as pl
from jax.experimental.pallas import tpu as pltpu

F32 = jnp.float32
BF16 = jnp.bfloat16

NORM_EPS = 1e-6
RET_HEADS = 4
RET_GN_EPS = 1e-5
ROPE_BASE = 10000.0
STREAM_CHUNK = 64
RWKV_HEAD_DIM = 64
RWKV_LN_EPS = 64e-5
LRU_BLOCK = 256
CONV_WIDTH = 4
LRU_C = 8.0

V7X_LANES = 128
V7X_SUBLANES = 8
V7X_VMEM_BYTES = 64 * 1024 * 1024
VMEM_CAP_BYTES = V7X_VMEM_BYTES - 8 * 1024 * 1024
VMEM_FLOOR_BYTES = 16 * 1024 * 1024

RWKV_CHUNK = 64
RWKV_PAIR = 2 * RWKV_HEAD_DIM
LORA_PAD = 512


def _tile(n, pref, mult=V7X_SUBLANES):
    if n <= pref:
        return n
    t = (pref // mult) * mult
    while t >= mult:
        if n % t == 0:
            return t
        t -= mult
    raise ValueError(f"no tile for {n} <= {pref}")


def _params(semantics, vmem_bytes):
    limit = int(min(max(vmem_bytes, VMEM_FLOOR_BYTES), VMEM_CAP_BYTES))
    return pltpu.CompilerParams(dimension_semantics=semantics, vmem_limit_bytes=limit)


def _nbytes(shape, dtype):
    return math.prod(shape) * jnp.dtype(dtype).itemsize


def _dot(a, b):
    return jnp.dot(a, b, preferred_element_type=F32)


def _dot_nt(a, b):
    return lax.dot_general(a, b, (((1,), (1,)), ((), ())), preferred_element_type=F32)


def _dot_tn(a, b):
    return lax.dot_general(a, b, (((0,), (0,)), ((), ())), preferred_element_type=F32)


def _split2(x):
    hi = x.astype(BF16)
    lo = (x - hi.astype(F32)).astype(BF16)
    return hi, lo


def _split3(x):
    hi = x.astype(BF16)
    r1 = x - hi.astype(F32)
    mid = r1.astype(BF16)
    lo = (r1 - mid.astype(F32)).astype(BF16)
    return hi, mid, lo


def _seg_sum(x, ones_bd):
    hi, lo = _split2(x)
    return _dot(hi, ones_bd) + _dot(lo, ones_bd)


def _rms(x, g):
    ms = jnp.mean(x * x, axis=-1, keepdims=True)
    return x * lax.rsqrt(ms + NORM_EPS) * g


def _sigmoid(x):
    return jax.nn.sigmoid(x)


def _norm_matmul_kernel(x_ref, g_ref, w_ref, o_ref, xn_ref):
    @pl.when(pl.program_id(1) == 0)
    def _():
        xn_ref[...] = _rms(x_ref[...], g_ref[...]).astype(BF16)

    o_ref[...] = _dot(xn_ref[...], w_ref[...])


def _norm_matmul(h, g, w_bf16, name):
    n, d = h.shape
    m = w_bf16.shape[1]
    tm = _tile(n, 1024)
    tn = _tile(m, 512, V7X_LANES)
    vmem = (2 * _nbytes((tm, d), F32) + _nbytes((tm, d), BF16) + 2 * _nbytes((d, tn), BF16)
            + 2 * _nbytes((tm, tn), F32) + 2 * _nbytes((tm, d), F32))
    return pl.pallas_call(
        _norm_matmul_kernel,
        grid=(n // tm, m // tn),
        in_specs=[pl.BlockSpec((tm, d), lambda i, j: (i, 0)),
                  pl.BlockSpec((1, d), lambda i, j: (0, 0)),
                  pl.BlockSpec((d, tn), lambda i, j: (0, j))],
        out_specs=pl.BlockSpec((tm, tn), lambda i, j: (i, j)),
        out_shape=jax.ShapeDtypeStruct((n, m), F32),
        scratch_shapes=[pltpu.VMEM((tm, d), BF16)],
        compiler_params=_params(("parallel", "arbitrary"), vmem),
        name=name,
    )(h, g.reshape(1, d), w_bf16)


def _matmul_residual_kernel(x_ref, w_ref, r_ref, o_ref):
    o_ref[...] = r_ref[...] + _dot(x_ref[...], w_ref[...])


def _matmul_residual(x_bf16, w_bf16, res, name):
    n, k = x_bf16.shape
    m = w_bf16.shape[1]
    tm = _tile(n, 1024)
    tn = _tile(m, 1024, V7X_LANES)
    vmem = (2 * _nbytes((tm, k), BF16) + 2 * _nbytes((k, tn), BF16) + 5 * _nbytes((tm, tn), F32))
    return pl.pallas_call(
        _matmul_residual_kernel,
        grid=(n // tm, m // tn),
        in_specs=[pl.BlockSpec((tm, k), lambda i, j: (i, 0)),
                  pl.BlockSpec((k, tn), lambda i, j: (0, j)),
                  pl.BlockSpec((tm, tn), lambda i, j: (i, j))],
        out_specs=pl.BlockSpec((tm, tn), lambda i, j: (i, j)),
        out_shape=jax.ShapeDtypeStruct((n, m), F32),
        compiler_params=_params(("parallel", "arbitrary"), vmem),
        name=name,
    )(x_bf16, w_bf16, res)


def _mlp_kernel(x_ref, g_ref, w1_ref, w2_ref, o_ref, xn_ref, acc_ref):
    j = pl.program_id(1)

    @pl.when(j == 0)
    def _():
        xn_ref[...] = _rms(x_ref[...], g_ref[...]).astype(BF16)
        acc_ref[...] = jnp.zeros_like(acc_ref)

    a = jnp.maximum(_dot(xn_ref[...], w1_ref[...]), 0.0)
    acc_ref[...] += _dot((a * a).astype(BF16), w2_ref[...])

    @pl.when(j == pl.num_programs(1) - 1)
    def _():
        o_ref[...] = x_ref[...] + acc_ref[...]


def _mlp(h, g, w1_bf16, w2_bf16, name):
    n, d = h.shape
    f = w1_bf16.shape[1]
    tm = _tile(n, 512)
    tf = _tile(f, 512, V7X_LANES)
    vmem = (4 * _nbytes((tm, d), F32) + _nbytes((tm, d), BF16) + _nbytes((tm, d), F32)
            + 4 * _nbytes((d, tf), BF16) + 3 * _nbytes((tm, tf), F32) + _nbytes((tm, d), F32))
    return pl.pallas_call(
        _mlp_kernel,
        grid=(n // tm, f // tf),
        in_specs=[pl.BlockSpec((tm, d), lambda i, j: (i, 0)),
                  pl.BlockSpec((1, d), lambda i, j: (0, 0)),
                  pl.BlockSpec((d, tf), lambda i, j: (0, j)),
                  pl.BlockSpec((tf, d), lambda i, j: (j, 0))],
        out_specs=pl.BlockSpec((tm, d), lambda i, j: (i, 0)),
        out_shape=jax.ShapeDtypeStruct((n, d), F32),
        scratch_shapes=[pltpu.VMEM((tm, d), BF16), pltpu.VMEM((tm, d), F32)],
        compiler_params=_params(("parallel", "arbitrary"), vmem),
        name=name,
    )(h, g.reshape(1, d), w1_bf16, w2_bf16)


def _final_norm_kernel(x_ref, g_ref, o_ref):
    o_ref[...] = _rms(x_ref[...], g_ref[...])


def _final_norm(h, g):
    n, d = h.shape
    tm = _tile(n, 512)
    vmem = 6 * _nbytes((tm, d), F32)
    return pl.pallas_call(
        _final_norm_kernel,
        grid=(n // tm,),
        in_specs=[pl.BlockSpec((tm, d), lambda i: (i, 0)),
                  pl.BlockSpec((1, d), lambda i: (0, 0))],
        out_specs=pl.BlockSpec((tm, d), lambda i: (i, 0)),
        out_shape=jax.ShapeDtypeStruct((n, d), F32),
        compiler_params=_params(("parallel",), vmem),
        name="final_norm",
    )(h, g.reshape(1, d))


def _retention_tables(blk, dh):
    log_g = jnp.log1p(-jnp.exp2(-5.0 - jnp.arange(RET_HEADS, dtype=F32)))
    pos = jnp.arange(blk, dtype=F32)
    n, m = pos[:, None], pos[None, :]
    cn, cm = jnp.floor(n / STREAM_CHUNK), jnp.floor(m / STREAM_CHUNK)
    dist = jnp.where(cn == cm, jnp.abs(n - m), n - m)
    lg = log_g[:, None, None]
    dmask = jnp.where((cm <= cn)[None], jnp.exp(lg * dist[None]), 0.0)
    qdec = jnp.broadcast_to(jnp.exp(lg * (pos + 1.0)[None, :, None]), (RET_HEADS, blk, dh))
    kdec = jnp.broadcast_to(jnp.exp(lg * (blk - 1.0 - pos)[None, :, None]), (RET_HEADS, blk, dh))
    cdec = jnp.broadcast_to(jnp.exp(lg * blk), (RET_HEADS, 1, dh))
    return dmask, qdec, kdec, cdec


def _rope_tables(t, dh):
    inv = 1.0 / (ROPE_BASE ** (jnp.arange(0, dh, 2, dtype=F32) / dh))
    ang = jnp.arange(t, dtype=F32)[:, None] * inv[None, :]
    return jnp.cos(ang), jnp.sin(ang)


def _retention_kernel(q_ref, k_ref, v_ref, g_ref, cos_ref, sin_ref, dm_ref, qd_ref, kd_ref,
                      cd_ref, o_ref, s_ref):
    @pl.when(pl.program_id(2) == 0)
    def _():
        s_ref[...] = jnp.zeros_like(s_ref)

    dh = q_ref.shape[1]
    half = dh // 2
    cos, sin = cos_ref[...], sin_ref[...]

    def rot(t):
        t1, t2 = t[:, :half], t[:, half:]
        return jnp.concatenate([t1 * cos - t2 * sin, t2 * cos + t1 * sin], axis=-1)

    q = rot(q_ref[...])
    k = rot(k_ref[...]) * (dh ** -0.5)
    vb = v_ref[...].astype(BF16)
    scores = _dot_nt(q.astype(BF16), k.astype(BF16)) * dm_ref[0]
    state = s_ref[...]
    out = _dot(scores.astype(BF16), vb) + _dot((q * qd_ref[0]).astype(BF16), state.astype(BF16))
    s_ref[...] = state * cd_ref[0] + _dot_tn((k * kd_ref[0]).astype(BF16), vb)

    mu = jnp.mean(out, axis=-1, keepdims=True)
    cen = out - mu
    var = jnp.mean(cen * cen, axis=-1, keepdims=True)
    y = cen * lax.rsqrt(var + RET_GN_EPS)
    g = g_ref[...]
    o_ref[...] = (g * _sigmoid(g) * y).astype(BF16)


def _retention(z, b, t, ret_w, cos, sin):
    n = z.shape[0]
    dh = ret_w // RET_HEADS
    blk = _tile(t, 256, STREAM_CHUNK)
    nblk = t // blk
    dmask, qdec, kdec, cdec = _retention_tables(blk, dh)
    row = lambda bi, hi, ci: bi * nblk + ci
    col_spec = lambda off: pl.BlockSpec((blk, dh), lambda bi, hi, ci: (row(bi, hi, ci), off + hi))
    head_spec = lambda shape: pl.BlockSpec((1,) + shape, lambda bi, hi, ci: (hi, 0, 0))
    vmem = (2 * 4 * _nbytes((blk, dh), F32) + 4 * _nbytes((blk, dh // 2), F32)
            + 2 * _nbytes((blk, blk), F32) + 4 * _nbytes((blk, dh), F32) + _nbytes((dh, dh), F32)
            + 12 * _nbytes((blk, max(blk, dh)), F32))
    return pl.pallas_call(
        _retention_kernel,
        grid=(b, RET_HEADS, nblk),
        in_specs=[col_spec(0), col_spec(RET_HEADS), col_spec(2 * RET_HEADS), col_spec(3 * RET_HEADS),
                  pl.BlockSpec((blk, dh // 2), lambda bi, hi, ci: (ci, 0)),
                  pl.BlockSpec((blk, dh // 2), lambda bi, hi, ci: (ci, 0)),
                  head_spec((blk, blk)), head_spec((blk, dh)), head_spec((blk, dh)),
                  head_spec((1, dh))],
        out_specs=pl.BlockSpec((blk, dh), lambda bi, hi, ci: (row(bi, hi, ci), hi)),
        out_shape=jax.ShapeDtypeStruct((n, ret_w), BF16),
        scratch_shapes=[pltpu.VMEM((dh, dh), F32)],
        compiler_params=_params(("parallel", "parallel", "arbitrary"), vmem),
        name="retention",
    )(z, z, z, z, cos, sin, dmask, qdec, kdec, cdec)


def _rwkv_prep_kernel(has_vres, t_len, lora_ranks, *refs):
    if has_vres:
        (zr_ref, zk_ref, zv_ref, zl_ref, mur_ref, muk_ref, muv_ref, mul_ref, wl_ref, w0_ref, a0_ref,
         kk_ref, ka_ref, ones_ref, vf_ref, v0_ref, v1_ref, v2_ref,
         r_out, lw_out, k_out, v_out, p_out, a_out, g_out, cr, ck, cv, cl) = refs
    else:
        (zr_ref, zk_ref, zv_ref, zl_ref, mur_ref, muk_ref, muv_ref, mul_ref, wl_ref, w0_ref, a0_ref,
         kk_ref, ka_ref, ones_ref,
         r_out, lw_out, k_out, v_out, p_out, a_out, g_out, cr, ck, cv, cl) = refs
    tm = zr_ref.shape[0]
    rw = zr_ref.shape[1]
    at_start = (pl.program_id(0) * tm) % t_len == 0
    row0 = lax.broadcasted_iota(jnp.int32, (tm, 1), 0) == 0

    def shift_mix(x_ref, carry_ref, mu_ref):
        x = x_ref[...]
        last = jnp.where(at_start, 0.0, carry_ref[...])
        prev = jnp.where(row0, last, pltpu.roll(x, 1, 0))
        carry_ref[...] = x[tm - 1:tm, :]
        return x + mu_ref[...] * (prev - x)

    r = shift_mix(zr_ref, cr, mur_ref)
    kr = shift_mix(zk_ref, ck, muk_ref)
    vr = shift_mix(zv_ref, cv, muv_ref)
    lo = shift_mix(zl_ref, cl, mul_ref)

    rank_w, rank_a = lora_ranks
    lane = lax.broadcasted_iota(jnp.int32, (1, lo.shape[1]), 1)
    feat = jnp.where(lane < rank_w, jnp.tanh(lo), jnp.where(lane < rank_w + rank_a, lo, _sigmoid(lo)))
    proj = _dot(feat.astype(BF16), wl_ref[...])
    log_w = -math.exp(-0.5) * _sigmoid(w0_ref[...] + proj[:, :rw])
    iclr = _sigmoid(a0_ref[...] + proj[:, rw:2 * rw])
    gate = proj[:, 2 * rw:]

    if has_vres:
        low = _dot(vr.astype(BF16), v1_ref[...])
        mix = _sigmoid(v0_ref[...] + _dot(low.astype(BF16), v2_ref[...]))
        vr = vr + (vf_ref[...] - vr) * mix

    kk = kr * kk_ref[...]
    norm = jnp.sqrt(_seg_sum(kk * kk, ones_ref[...]))
    kk = kk / jnp.maximum(norm, 1e-12)
    k2 = kr * (1.0 + (iclr - 1.0) * ka_ref[...])

    r_out[...] = r
    lw_out[...] = log_w
    k_out[...] = k2
    v_out[...] = vr
    p_out[...] = kk
    a_out[...] = iclr
    g_out[...] = gate


def _head_ones(width, head):
    idx = jnp.arange(width) // head
    return (idx[:, None] == idx[None, :]).astype(BF16)


def _rwkv_prep(z, t, ret_in, rw, mu, w2, a2, g2, w0, a0, k_k, k_a, v_first, v_res):
    n = z.shape[0]
    has_vres = v_res is not None
    rank_w, rank_a, rank_g = w2.shape[0], a2.shape[0], g2.shape[0]
    lora = rank_w + rank_a + rank_g
    assert lora <= LORA_PAD and ret_in % rw == 0 and (ret_in + 3 * rw) % LORA_PAD == 0
    tm = _tile(t, 256)
    row2 = lambda v: v.reshape(1, -1)
    wl = jnp.zeros((LORA_PAD, 3 * rw), F32)
    wl = wl.at[:rank_w, :rw].set(w2).at[rank_w:rank_w + rank_a, rw:2 * rw].set(a2)
    wl = wl.at[rank_w + rank_a:lora, 2 * rw:].set(g2).astype(BF16)
    mu_l = jnp.zeros((1, LORA_PAD), F32).at[0, :lora].set(mu[3 * rw:])
    c0 = ret_in // rw
    zspec = lambda ci: pl.BlockSpec((tm, rw), lambda i: (i, ci))
    vec = pl.BlockSpec((1, rw), lambda i: (0, 0))
    full = lambda a: pl.BlockSpec(a.shape, lambda i: (0,) * a.ndim)
    ones_bd = _head_ones(rw, RWKV_HEAD_DIM)
    args = [z, z, z, z, row2(mu[:rw]), row2(mu[rw:2 * rw]), row2(mu[2 * rw:3 * rw]), mu_l, wl,
            row2(w0), row2(a0), row2(k_k), row2(k_a), ones_bd]
    specs = [zspec(c0), zspec(c0 + 1), zspec(c0 + 2),
             pl.BlockSpec((tm, LORA_PAD), lambda i: (i, (ret_in + 3 * rw) // LORA_PAD)),
             vec, vec, vec, full(mu_l), full(wl), vec, vec, vec, vec, full(ones_bd)]
    if has_vres:
        v0, v1, v2 = v_res
        rank_v = v1.shape[1]
        rank_pad = -(-rank_v // V7X_LANES) * V7X_LANES
        v1p = jnp.zeros((rw, rank_pad), F32).at[:, :rank_v].set(v1).astype(BF16)
        v2p = jnp.zeros((rank_pad, rw), F32).at[:rank_v, :].set(v2).astype(BF16)
        args += [v_first, row2(v0), v1p, v2p]
        specs += [pl.BlockSpec((tm, rw), lambda i: (i, 0)), vec, full(v1p), full(v2p)]
    out_spec = pl.BlockSpec((tm, rw), lambda i: (i, 0))
    vmem = (2 * (3 + has_vres) * _nbytes((tm, rw), F32) + 2 * _nbytes((tm, LORA_PAD), F32)
            + 2 * _nbytes(wl.shape, BF16) + 2 * _nbytes((rw, rw), BF16)
            + 14 * _nbytes((tm, rw), F32) + 12 * _nbytes((tm, rw), F32))
    return pl.pallas_call(
        functools.partial(_rwkv_prep_kernel, has_vres, t, (rank_w, rank_a)),
        grid=(n // tm,),
        in_specs=specs,
        out_specs=[out_spec] * 7,
        out_shape=[jax.ShapeDtypeStruct((n, rw), F32)] * 7,
        scratch_shapes=[pltpu.VMEM((1, rw), F32)] * 3 + [pltpu.VMEM((1, LORA_PAD), F32)],
        compiler_params=_params(("arbitrary",), vmem),
        name="rwkv_prep",
    )(*args)


def _rwkv_core_kernel(r_ref, lw_ref, k_ref, v_ref, p_ref, a_ref, g_ref, rk_ref, lnw_ref, lnb_ref,
                      o_ref, h_ref, y_ref):
    @pl.when(pl.program_id(2) == 0)
    def _():
        h_ref[...] = jnp.zeros_like(h_ref)

    L = RWKV_CHUNK
    W = RWKV_PAIR
    hd = RWKV_HEAD_DIM
    n_chunks = r_ref.shape[0] // L

    ri = lax.broadcasted_iota(jnp.int32, (W, W), 0)
    ci = lax.broadcasted_iota(jnp.int32, (W, W), 1)
    same_head = (ri // hd) == (ci // hd)
    strict = same_head & ((ri % hd) > (ci % hd))
    incl = same_head & ((ri % hd) >= (ci % hd))
    eye = (ri == ci).astype(F32)
    merge_masks = []
    s = 1
    while s < hd:
        merge_masks.append(((ri // (2 * s)) == (ci // (2 * s))) & ((ri // s) != (ci // s)))
        s *= 2
    ones_bd = same_head.astype(BF16)
    ti = lax.broadcasted_iota(jnp.int32, (L, L), 0)
    si = lax.broadcasted_iota(jnp.int32, (L, L), 1)
    tri = (ti >= si).astype(BF16)
    bd_mask = (lax.broadcasted_iota(jnp.int32, (2 * L, W), 0) // L) == (
        lax.broadcasted_iota(jnp.int32, (2 * L, W), 1) // hd)

    def bd(x):
        return jnp.where(bd_mask, jnp.concatenate([x, x], axis=0), 0.0)

    def fold(x):
        return x[:L, :] + x[L:, :]

    def chunk(i, carry):
        sl = pl.ds(pl.multiple_of(i * L, L), L)
        r, lw, k, v, p = r_ref[sl, :], lw_ref[sl, :], k_ref[sl, :], v_ref[sl, :], p_ref[sl, :]
        q = p * a_ref[sl, :]
        l_hi, l_mid, l_lo = _split3(lw)
        c = _dot(tri, l_hi) + _dot(tri, l_mid) + _dot(tri, l_lo)
        c_last = c[L - 1:L, :]
        e_in = jnp.exp(c)
        e_out = jnp.exp(-c)
        e_end = jnp.exp(c_last - c)
        rt, kt, qt = r * e_in, k * e_out, q * e_out
        pt = p * jnp.exp(c - lw)
        kh, qh = k * e_end, q * e_end

        pt_bd, rt_bd = bd(pt), bd(rt)
        lhs = jnp.concatenate([pt_bd, rt_bd], axis=0).astype(BF16)
        rhs = jnp.concatenate([bd(qt), bd(kt)], axis=0).astype(BF16)
        m = _dot_nt(lhs, rhs)
        a_pq = jnp.where(strict, m[:W, :W], 0.0)
        a_pk = jnp.where(strict, m[:W, W:], 0.0).astype(BF16)
        a_rq = jnp.where(incl, m[W:, :W], 0.0).astype(BF16)
        a_rk = jnp.where(incl, m[W:, W:], 0.0).astype(BF16)

        t_inv = eye - jnp.where(merge_masks[0], a_pq, 0.0)
        for mask in merge_masks[1:]:
            tb = t_inv.astype(BF16)
            t_inv = t_inv - _dot(tb, _dot(jnp.where(mask, a_pq, 0.0).astype(BF16), tb).astype(BF16))
        t_inv = t_inv.astype(BF16)

        v_bd = bd(v).astype(BF16)
        pk_v = _dot(a_pk, v_bd)
        sol = _dot(t_inv, jnp.concatenate([pt_bd, pk_v], axis=1).astype(BF16))
        p_hat = sol[:, :W].astype(BF16)
        u0 = sol[:, W:].astype(BF16)
        r_hat = fold(rt_bd - _dot(a_rq, p_hat))
        y0 = fold(_dot(a_rk, v_bd) - _dot(a_rq, u0))
        qh_bd = bd(qh).astype(BF16)
        kh_bd = bd(kh).astype(BF16)
        g_mat = eye * jnp.exp(c_last) - _dot_tn(qh_bd, p_hat)
        h_add = _dot_tn(kh_bd, v_bd) - _dot_tn(qh_bd, u0)

        h = h_ref[...]
        h_hi, h_lo = _split2(h)
        y_ref[sl, :] = _dot(r_hat.astype(BF16), h_hi) + y0
        g_hi, g_lo = _split2(g_mat)
        h_ref[...] = _dot(g_hi, h_hi) + _dot(g_hi, h_lo) + _dot(g_lo, h_hi) + h_add
        return carry

    lax.fori_loop(0, n_chunks, chunk, 0)

    y = y_ref[...]
    inv_hd = 1.0 / hd
    mu = _seg_sum(y, ones_bd) * inv_hd
    cen = y - mu
    var = _seg_sum(cen * cen, ones_bd) * inv_hd
    yn = cen * lax.rsqrt(var + RWKV_LN_EPS) * lnw_ref[...] + lnb_ref[...]
    bonus = _seg_sum(r_ref[...] * k_ref[...] * rk_ref[...], ones_bd) * v_ref[...]
    o_ref[...] = ((yn + bonus) * g_ref[...]).astype(BF16)


def _rwkv_core(r, lw, k, v, p, a, gate, r_k, lnx_w, lnx_b, b, t):
    n, rw = r.shape
    assert rw % RWKV_PAIR == 0 and t % RWKV_CHUNK == 0
    blk = _tile(t, 1024, RWKV_CHUNK)
    nblk = t // blk
    tok = pl.BlockSpec((blk, RWKV_PAIR), lambda bi, pi, ci: (bi * nblk + ci, pi))
    vec = pl.BlockSpec((1, RWKV_PAIR), lambda bi, pi, ci: (0, pi))
    row2 = lambda x: x.reshape(1, rw)
    vmem = (2 * 7 * _nbytes((blk, RWKV_PAIR), F32) + 2 * _nbytes((blk, RWKV_PAIR), BF16)
            + _nbytes((blk, RWKV_PAIR), F32) + 64 * _nbytes((2 * RWKV_PAIR, 2 * RWKV_PAIR), F32)
            + 8 * _nbytes((blk, RWKV_PAIR), F32))
    return pl.pallas_call(
        _rwkv_core_kernel,
        grid=(b, rw // RWKV_PAIR, nblk),
        in_specs=[tok] * 7 + [vec] * 3,
        out_specs=tok,
        out_shape=jax.ShapeDtypeStruct((n, rw), BF16),
        scratch_shapes=[pltpu.VMEM((RWKV_PAIR, RWKV_PAIR), F32), pltpu.VMEM((blk, RWKV_PAIR), F32)],
        compiler_params=_params(("parallel", "parallel", "arbitrary"), vmem),
        name="rwkv_core",
    )(r, lw, k, v, p, a, gate, row2(r_k), row2(lnx_w), row2(lnx_b))


def _lru_kernel(y_ref, x_ref, cw_ref, cb_ref, gxw_ref, gxb_ref, gaw_ref, gab_ref, lam_ref,
                o_ref, xtail_ref, h_ref):
    @pl.when(pl.program_id(2) == 0)
    def _():
        xtail_ref[...] = jnp.zeros_like(xtail_ref)
        h_ref[...] = jnp.zeros_like(h_ref)

    lt = x_ref.shape[0]
    tail = xtail_ref.shape[0]
    x = x_ref[...]
    ext = jnp.concatenate([xtail_ref[...], x], axis=0)
    xtail_ref[...] = x[lt - tail:, :]
    cw = cw_ref[...]
    xc = cb_ref[...] + cw[CONV_WIDTH - 1:CONV_WIDTH, :] * x
    for j in range(1, CONV_WIDTH):
        xc = xc + cw[CONV_WIDTH - 1 - j:CONV_WIDTH - j, :] * pltpu.roll(ext, j, 0)[tail:, :]

    xcb = xc.astype(BF16)
    gate_x = _sigmoid(_dot(xcb, gxw_ref[0]) + gxb_ref[...])
    gate_a = _sigmoid(_dot(xcb, gaw_ref[0]) + gab_ref[...])
    neg_lam = -lam_ref[...]
    softplus = jnp.maximum(neg_lam, 0.0) + jnp.log(1.0 + jnp.exp(-jnp.abs(neg_lam)))
    log_a = -LRU_C * gate_a * softplus
    a = jnp.exp(log_a)
    bb = xc * gate_x * jnp.sqrt(-jnp.tanh(log_a) * (a * a + 1.0))

    row = lax.broadcasted_iota(jnp.int32, (lt, 1), 0)
    s = 1
    while s < lt:
        valid = row >= s
        b_prev = jnp.where(valid, pltpu.roll(bb, s, 0), 0.0)
        a_prev = jnp.where(valid, pltpu.roll(a, s, 0), 1.0)
        bb = bb + a * b_prev
        a = a * a_prev
        s *= 2
    hs = a * h_ref[...] + bb
    h_ref[...] = hs[lt - 1:lt, :]

    y = y_ref[...]
    gelu = 0.5 * y * (1.0 + jnp.tanh(math.sqrt(2.0 / math.pi) * (y + 0.044715 * (y * y * y))))
    o_ref[...] = (gelu * hs).astype(BF16)


def _lru(z, b, t, d, conv_w, conv_b, gx_w, gx_b, ga_w, ga_b, lam):
    n = z.shape[0]
    assert d % LRU_BLOCK == 0
    nb = d // LRU_BLOCK
    lt = _tile(t, 256)
    nblk = t // lt
    tail = V7X_SUBLANES
    assert CONV_WIDTH - 1 <= tail <= lt
    tok = lambda off: pl.BlockSpec((lt, LRU_BLOCK), lambda bi, ni, ci: (bi * nblk + ci, off + ni))
    vec = lambda rows: pl.BlockSpec((rows, LRU_BLOCK), lambda bi, ni, ci: (0, ni))
    wsp = pl.BlockSpec((1, LRU_BLOCK, LRU_BLOCK), lambda bi, ni, ci: (ni, 0, 0))
    row2 = lambda x: x.reshape(1, d)
    vmem = (6 * _nbytes((lt, LRU_BLOCK), F32) + 8 * _nbytes((LRU_BLOCK, LRU_BLOCK), BF16)
            + 24 * _nbytes((lt, LRU_BLOCK), F32))
    return pl.pallas_call(
        _lru_kernel,
        grid=(b, nb, nblk),
        in_specs=[tok(0), tok(nb), vec(CONV_WIDTH), vec(1), wsp, vec(1), wsp, vec(1), vec(1)],
        out_specs=tok(0),
        out_shape=jax.ShapeDtypeStruct((n, d), BF16),
        scratch_shapes=[pltpu.VMEM((tail, LRU_BLOCK), F32), pltpu.VMEM((1, LRU_BLOCK), F32)],
        compiler_params=_params(("parallel", "parallel", "arbitrary"), vmem),
        name="rg_lru",
    )(z, z, conv_w, row2(conv_b), gx_w.astype(BF16), row2(gx_b), ga_w.astype(BF16), row2(ga_b),
      row2(lam))


def _even_layer(h, b, t, norm_g, w_in, w_out, mu, w0, w2, a0, a2, g2, k_k, k_a, r_k, lnx_w, lnx_b,
                v_first, v_res, rope):
    d = h.shape[1]
    ret_w = d // 2
    rw = d - ret_w
    ret_in = 4 * ret_w
    n_in = w_in.shape[1]
    n_pad = ret_in + 3 * rw + LORA_PAD
    w_in_p = jnp.pad(w_in, ((0, 0), (0, n_pad - n_in))).astype(BF16)
    z = _norm_matmul(h, norm_g, w_in_p, "even_in_proj")
    out_ret = _retention(z, b, t, ret_w, *rope)
    r, lw, k, v, p, a, gate = _rwkv_prep(z, t, ret_in, rw, mu, w2, a2, g2, w0, a0, k_k, k_a,
                                         v_first, v_res)
    out_rw = _rwkv_core(r, lw, k, v, p, a, gate, r_k.reshape(-1), lnx_w, lnx_b, b, t)
    mixed_in = jnp.concatenate([out_ret, out_rw], axis=-1)
    h = _matmul_residual(mixed_in, w_out.astype(BF16), h, "even_out_proj")
    return h, (v if v_res is None else v_first)


def _odd_layer(h, b, t, norm_g, w_in, conv_w, conv_b, gx_w, gx_b, ga_w, ga_b, lam, w_out):
    d = h.shape[1]
    z = _norm_matmul(h, norm_g, w_in.astype(BF16), "odd_in_proj")
    gated = _lru(z, b, t, d, conv_w, conv_b, gx_w, gx_b, ga_w, ga_b, lam)
    return _matmul_residual(gated, w_out.astype(BF16), h, "odd_out_proj")


def kernel(x, ev_norm, ev_w_in, ev_w_out, rw_mu, rw_w0, rw_w2, rw_a0, rw_a2, rw_g2, rw_k_k, rw_k_a, rw_r_k, rw_lnx_w, rw_lnx_b, rw_v0, rw_v1, rw_v2, od_norm, od_w_in, od_conv_w, od_conv_b, od_gx_w, od_gx_b, od_ga_w, od_ga_b, od_lam, od_w_out, ff_norm, ff_w1, ff_w2, final_norm):
    b, t, d = x.shape
    depth = ff_norm.shape[0]
    h = x.reshape(b * t, d)
    rope = _rope_tables(t, (d // 2) // RET_HEADS)
    v_first = None
    for layer in range(depth):
        if layer % 2 == 0:
            e = layer // 2
            v_res = None if e == 0 else (rw_v0[e - 1], rw_v1[e - 1], rw_v2[e - 1])
            h, v_first = _even_layer(
                h, b, t, ev_norm[e], ev_w_in[e], ev_w_out[e], rw_mu[e], rw_w0[e], rw_w2[e],
                rw_a0[e], rw_a2[e], rw_g2[e], rw_k_k[e], rw_k_a[e], rw_r_k[e], rw_lnx_w[e],
                rw_lnx_b[e], v_first, v_res, rope)
        else:
            o = layer // 2
            h = _odd_layer(h, b, t, od_norm[o], od_w_in[o], od_conv_w[o], od_conv_b[o], od_gx_w[o],
                           od_gx_b[o], od_ga_w[o], od_ga_b[o], od_lam[o], od_w_out[o])
        h = _mlp(h, ff_norm[layer], ff_w1[layer].astype(BF16), ff_w2[layer].astype(BF16),
                 f"mlp_{layer}")
    return _final_norm(h, final_norm).reshape(b, t, d)
```

```python
import functools
import math

import jax
import jax.numpy as jnp
from jax import lax
from jax.experimental import pallas as pl
from jax.experimental.pallas import tpu as pltpu

F32 = jnp.float32
BF16 = jnp.bfloat16

NORM_EPS = 1e-6
RET_HEADS = 4
RET_GN_EPS = 1e-5
ROPE_BASE = 10000.0
STREAM_CHUNK = 64
RWKV_HEAD_DIM = 64
RWKV_LN_EPS = 64e-5
LRU_BLOCK = 256
CONV_WIDTH = 4
LRU_C = 8.0

V7X_LANES = 128
V7X_SUBLANES = 8
V7X_VMEM_BYTES = 64 * 1024 * 1024
VMEM_CAP_BYTES = V7X_VMEM_BYTES - 8 * 1024 * 1024
VMEM_FLOOR_BYTES = 16 * 1024 * 1024

RWKV_CHUNK = 64
RWKV_PAIR = 2 * RWKV_HEAD_DIM
RWKV_GROUP_LANES = 8 * RWKV_PAIR
LORA_PAD = 512


def _tile(n, pref, mult=V7X_SUBLANES):
    if n <= pref:
        return n
    t = (pref // mult) * mult
    while t >= mult:
        if n % t == 0:
            return t
        t -= mult
    raise ValueError(f"no tile for {n} <= {pref}")


def _params(semantics, vmem_bytes):
    limit = int(min(max(vmem_bytes, VMEM_FLOOR_BYTES), VMEM_CAP_BYTES))
    return pltpu.CompilerParams(dimension_semantics=semantics, vmem_limit_bytes=limit)


def _nbytes(shape, dtype):
    return math.prod(shape) * jnp.dtype(dtype).itemsize


def _dot(a, b):
    return jnp.dot(a, b, preferred_element_type=F32)


def _dot_nt(a, b):
    return lax.dot_general(a, b, (((1,), (1,)), ((), ())), preferred_element_type=F32)


def _dot_tn(a, b):
    return lax.dot_general(a, b, (((0,), (0,)), ((), ())), preferred_element_type=F32)


def _split2(x):
    hi = x.astype(BF16)
    lo = (x - hi.astype(F32)).astype(BF16)
    return hi, lo


def _split3(x):
    hi = x.astype(BF16)
    r1 = x - hi.astype(F32)
    mid = r1.astype(BF16)
    lo = (r1 - mid.astype(F32)).astype(BF16)
    return hi, mid, lo


def _seg_sum(x, ones_bd):
    hi, lo = _split2(x)
    return _dot(hi, ones_bd) + _dot(lo, ones_bd)


def _rms(x, g):
    ms = jnp.mean(x * x, axis=-1, keepdims=True)
    return x * lax.rsqrt(ms + NORM_EPS) * g


def _sigmoid(x):
    return jax.nn.sigmoid(x)


def _norm_matmul_kernel(x_ref, g_ref, w_ref, o_ref, xn_ref):
    @pl.when(pl.program_id(1) == 0)
    def _():
        xn_ref[...] = _rms(x_ref[...], g_ref[...]).astype(BF16)

    o_ref[...] = _dot(xn_ref[...], w_ref[...])


def _norm_matmul(h, g, w_bf16, name):
    n, d = h.shape
    m = w_bf16.shape[1]
    tm = _tile(n, 1024)
    tn = _tile(m, 512, V7X_LANES)
    vmem = (2 * _nbytes((tm, d), F32) + _nbytes((tm, d), BF16) + 2 * _nbytes((d, tn), BF16)
            + 2 * _nbytes((tm, tn), F32) + 2 * _nbytes((tm, d), F32))
    return pl.pallas_call(
        _norm_matmul_kernel,
        grid=(n // tm, m // tn),
        in_specs=[pl.BlockSpec((tm, d), lambda i, j: (i, 0)),
                  pl.BlockSpec((1, d), lambda i, j: (0, 0)),
                  pl.BlockSpec((d, tn), lambda i, j: (0, j))],
        out_specs=pl.BlockSpec((tm, tn), lambda i, j: (i, j)),
        out_shape=jax.ShapeDtypeStruct((n, m), F32),
        scratch_shapes=[pltpu.VMEM((tm, d), BF16)],
        compiler_params=_params(("parallel", "arbitrary"), vmem),
        name=name,
    )(h, g.reshape(1, d), w_bf16)


def _matmul_residual_kernel(x_ref, w_ref, r_ref, o_ref):
    o_ref[...] = r_ref[...] + _dot(x_ref[...], w_ref[...])


def _matmul_residual(x_bf16, w_bf16, res, name):
    n, k = x_bf16.shape
    m = w_bf16.shape[1]
    tm = _tile(n, 1024)
    tn = _tile(m, 1024, V7X_LANES)
    vmem = (2 * _nbytes((tm, k), BF16) + 2 * _nbytes((k, tn), BF16) + 5 * _nbytes((tm, tn), F32))
    return pl.pallas_call(
        _matmul_residual_kernel,
        grid=(n // tm, m // tn),
        in_specs=[pl.BlockSpec((tm, k), lambda i, j: (i, 0)),
                  pl.BlockSpec((k, tn), lambda i, j: (0, j)),
                  pl.BlockSpec((tm, tn), lambda i, j: (i, j))],
        out_specs=pl.BlockSpec((tm, tn), lambda i, j: (i, j)),
        out_shape=jax.ShapeDtypeStruct((n, m), F32),
        compiler_params=_params(("parallel", "arbitrary"), vmem),
        name=name,
    )(x_bf16, w_bf16, res)


def _mlp_kernel(x_ref, g_ref, w1_ref, w2_ref, o_ref, xn_ref, acc_ref):
    j = pl.program_id(1)

    @pl.when(j == 0)
    def _():
        xn_ref[...] = _rms(x_ref[...], g_ref[...]).astype(BF16)
        acc_ref[...] = jnp.zeros_like(acc_ref)

    a = jnp.maximum(_dot(xn_ref[...], w1_ref[...]), 0.0)
    acc_ref[...] += _dot((a * a).astype(BF16), w2_ref[...])

    @pl.when(j == pl.num_programs(1) - 1)
    def _():
        o_ref[...] = x_ref[...] + acc_ref[...]


def _mlp(h, g, w1_bf16, w2_bf16, name):
    n, d = h.shape
    f = w1_bf16.shape[1]
    tm = _tile(n, 512)
    tf = _tile(f, 512, V7X_LANES)
    vmem = (4 * _nbytes((tm, d), F32) + _nbytes((tm, d), BF16) + _nbytes((tm, d), F32)
            + 4 * _nbytes((d, tf), BF16) + 3 * _nbytes((tm, tf), F32) + _nbytes((tm, d), F32))
    return pl.pallas_call(
        _mlp_kernel,
        grid=(n // tm, f // tf),
        in_specs=[pl.BlockSpec((tm, d), lambda i, j: (i, 0)),
                  pl.BlockSpec((1, d), lambda i, j: (0, 0)),
                  pl.BlockSpec((d, tf), lambda i, j: (0, j)),
                  pl.BlockSpec((tf, d), lambda i, j: (j, 0))],
        out_specs=pl.BlockSpec((tm, d), lambda i, j: (i, 0)),
        out_shape=jax.ShapeDtypeStruct((n, d), F32),
        scratch_shapes=[pltpu.VMEM((tm, d), BF16), pltpu.VMEM((tm, d), F32)],
        compiler_params=_params(("parallel", "arbitrary"), vmem),
        name=name,
    )(h, g.reshape(1, d), w1_bf16, w2_bf16)


def _final_norm_kernel(x_ref, g_ref, o_ref):
    o_ref[...] = _rms(x_ref[...], g_ref[...])


def _final_norm(h, g):
    n, d = h.shape
    tm = _tile(n, 512)
    vmem = 6 * _nbytes((tm, d), F32)
    return pl.pallas_call(
        _final_norm_kernel,
        grid=(n // tm,),
        in_specs=[pl.BlockSpec((tm, d), lambda i: (i, 0)),
                  pl.BlockSpec((1, d), lambda i: (0, 0))],
        out_specs=pl.BlockSpec((tm, d), lambda i: (i, 0)),
        out_shape=jax.ShapeDtypeStruct((n, d), F32),
        compiler_params=_params(("parallel",), vmem),
        name="final_norm",
    )(h, g.reshape(1, d))


def _retention_tables(blk, dh):
    log_g = jnp.log1p(-jnp.exp2(-5.0 - jnp.arange(RET_HEADS, dtype=F32)))
    pos = jnp.arange(blk, dtype=F32)
    n, m = pos[:, None], pos[None, :]
    cn, cm = jnp.floor(n / STREAM_CHUNK), jnp.floor(m / STREAM_CHUNK)
    dist = jnp.where(cn == cm, jnp.abs(n - m), n - m)
    lg = log_g[:, None, None]
    dmask = jnp.where((cm <= cn)[None], jnp.exp(lg * dist[None]), 0.0)
    qdec = jnp.broadcast_to(jnp.exp(lg * (pos + 1.0)[None, :, None]), (RET_HEADS, blk, dh))
    kdec = jnp.broadcast_to(jnp.exp(lg * (blk - 1.0 - pos)[None, :, None]), (RET_HEADS, blk, dh))
    cdec = jnp.broadcast_to(jnp.exp(lg * blk), (RET_HEADS, 1, dh))
    return dmask, qdec, kdec, cdec


def _rope_tables(t, dh):
    inv = 1.0 / (ROPE_BASE ** (jnp.arange(0, dh, 2, dtype=F32) / dh))
    ang = jnp.arange(t, dtype=F32)[:, None] * inv[None, :]
    return jnp.cos(ang), jnp.sin(ang)


def _retention_kernel(q_ref, k_ref, v_ref, g_ref, cos_ref, sin_ref, dm_ref, qd_ref, kd_ref,
                      cd_ref, o_ref, s_ref):
    @pl.when(pl.program_id(2) == 0)
    def _():
        s_ref[...] = jnp.zeros_like(s_ref)

    dh = q_ref.shape[1]
    half = dh // 2
    cos, sin = cos_ref[...], sin_ref[...]

    def rot(t):
        t1, t2 = t[:, :half], t[:, half:]
        return jnp.concatenate([t1 * cos - t2 * sin, t2 * cos + t1 * sin], axis=-1)

    q = rot(q_ref[...])
    k = rot(k_ref[...]) * (dh ** -0.5)
    vb = v_ref[...].astype(BF16)
    scores = _dot_nt(q.astype(BF16), k.astype(BF16)) * dm_ref[0]
    state = s_ref[...]
    out = _dot(scores.astype(BF16), vb) + _dot((q * qd_ref[0]).astype(BF16), state.astype(BF16))
    s_ref[...] = state * cd_ref[0] + _dot_tn((k * kd_ref[0]).astype(BF16), vb)

    mu = jnp.mean(out, axis=-1, keepdims=True)
    cen = out - mu
    var = jnp.mean(cen * cen, axis=-1, keepdims=True)
    y = cen * lax.rsqrt(var + RET_GN_EPS)
    g = g_ref[...]
    o_ref[...] = (g * _sigmoid(g) * y).astype(BF16)


def _retention(z, b, t, ret_w, cos, sin):
    n = z.shape[0]
    dh = ret_w // RET_HEADS
    blk = _tile(t, 256, STREAM_CHUNK)
    nblk = t // blk
    dmask, qdec, kdec, cdec = _retention_tables(blk, dh)
    row = lambda bi, hi, ci: bi * nblk + ci
    col_spec = lambda off: pl.BlockSpec((blk, dh), lambda bi, hi, ci: (row(bi, hi, ci), off + hi))
    head_spec = lambda shape: pl.BlockSpec((1,) + shape, lambda bi, hi, ci: (hi, 0, 0))
    vmem = (2 * 4 * _nbytes((blk, dh), F32) + 4 * _nbytes((blk, dh // 2), F32)
            + 2 * _nbytes((blk, blk), F32) + 4 * _nbytes((blk, dh), F32) + _nbytes((dh, dh), F32)
            + 12 * _nbytes((blk, max(blk, dh)), F32))
    return pl.pallas_call(
        _retention_kernel,
        grid=(b, RET_HEADS, nblk),
        in_specs=[col_spec(0), col_spec(RET_HEADS), col_spec(2 * RET_HEADS), col_spec(3 * RET_HEADS),
                  pl.BlockSpec((blk, dh // 2), lambda bi, hi, ci: (ci, 0)),
                  pl.BlockSpec((blk, dh // 2), lambda bi, hi, ci: (ci, 0)),
                  head_spec((blk, blk)), head_spec((blk, dh)), head_spec((blk, dh)),
                  head_spec((1, dh))],
        out_specs=pl.BlockSpec((blk, dh), lambda bi, hi, ci: (row(bi, hi, ci), hi)),
        out_shape=jax.ShapeDtypeStruct((n, ret_w), BF16),
        scratch_shapes=[pltpu.VMEM((dh, dh), F32)],
        compiler_params=_params(("parallel", "parallel", "arbitrary"), vmem),
        name="retention",
    )(z, z, z, z, cos, sin, dmask, qdec, kdec, cdec)


def _rwkv_prep_kernel(has_vres, t_len, lora_ranks, *refs):
    if has_vres:
        (zr_ref, zk_ref, zv_ref, zl_ref, mur_ref, muk_ref, muv_ref, mul_ref, wl_ref, w0_ref, a0_ref,
         kk_ref, ka_ref, ones_ref, vf_ref, v0_ref, v1_ref, v2_ref,
         r_out, lw_out, k_out, v_out, p_out, a_out, g_out, cr, ck, cv, cl) = refs
    else:
        (zr_ref, zk_ref, zv_ref, zl_ref, mur_ref, muk_ref, muv_ref, mul_ref, wl_ref, w0_ref, a0_ref,
         kk_ref, ka_ref, ones_ref,
         r_out, lw_out, k_out, v_out, p_out, a_out, g_out, cr, ck, cv, cl) = refs
    tm = zr_ref.shape[0]
    rw = zr_ref.shape[1]
    at_start = (pl.program_id(0) * tm) % t_len == 0
    row0 = lax.broadcasted_iota(jnp.int32, (tm, 1), 0) == 0

    def shift_mix(x_ref, carry_ref, mu_ref):
        x = x_ref[...]
        last = jnp.where(at_start, 0.0, carry_ref[...])
        prev = jnp.where(row0, last, pltpu.roll(x, 1, 0))
        carry_ref[...] = x[tm - 1:tm, :]
        return x + mu_ref[...] * (prev - x)

    r = shift_mix(zr_ref, cr, mur_ref)
    kr = shift_mix(zk_ref, ck, muk_ref)
    vr = shift_mix(zv_ref, cv, muv_ref)
    lo = shift_mix(zl_ref, cl, mul_ref)

    rank_w, rank_a = lora_ranks
    lane = lax.broadcasted_iota(jnp.int32, (1, lo.shape[1]), 1)
    feat = jnp.where(lane < rank_w, jnp.tanh(lo), jnp.where(lane < rank_w + rank_a, lo, _sigmoid(lo)))
    proj = _dot(feat.astype(BF16), wl_ref[...])
    log_w = -math.exp(-0.5) * _sigmoid(w0_ref[...] + proj[:, :rw])
    iclr = _sigmoid(a0_ref[...] + proj[:, rw:2 * rw])
    gate = proj[:, 2 * rw:]

    if has_vres:
        low = _dot(vr.astype(BF16), v1_ref[...])
        mix = _sigmoid(v0_ref[...] + _dot(low.astype(BF16), v2_ref[...]))
        vr = vr + (vf_ref[...] - vr) * mix

    kk = kr * kk_ref[...]
    norm = jnp.sqrt(_seg_sum(kk * kk, ones_ref[...]))
    kk = kk / jnp.maximum(norm, 1e-12)
    k2 = kr * (1.0 + (iclr - 1.0) * ka_ref[...])

    r_out[...] = r
    lw_out[...] = log_w
    k_out[...] = k2
    v_out[...] = vr
    p_out[...] = kk
    a_out[...] = iclr
    g_out[...] = gate


def _head_ones(width, head):
    idx = jnp.arange(width) // head
    return (idx[:, None] == idx[None, :]).astype(BF16)


def _rwkv_prep(z, t, ret_in, rw, mu, w2, a2, g2, w0, a0, k_k, k_a, v_first, v_res):
    n = z.shape[0]
    has_vres = v_res is not None
    rank_w, rank_a, rank_g = w2.shape[0], a2.shape[0], g2.shape[0]
    lora = rank_w + rank_a + rank_g
    assert lora <= LORA_PAD and ret_in % rw == 0 and (ret_in + 3 * rw) % LORA_PAD == 0
    tm = _tile(t, 256)
    row2 = lambda v: v.reshape(1, -1)
    wl = jnp.zeros((LORA_PAD, 3 * rw), F32)
    wl = wl.at[:rank_w, :rw].set(w2).at[rank_w:rank_w + rank_a, rw:2 * rw].set(a2)
    wl = wl.at[rank_w + rank_a:lora, 2 * rw:].set(g2).astype(BF16)
    mu_l = jnp.zeros((1, LORA_PAD), F32).at[0, :lora].set(mu[3 * rw:])
    c0 = ret_in // rw
    zspec = lambda ci: pl.BlockSpec((tm, rw), lambda i: (i, ci))
    vec = pl.BlockSpec((1, rw), lambda i: (0, 0))
    full = lambda a: pl.BlockSpec(a.shape, lambda i: (0,) * a.ndim)
    ones_bd = _head_ones(rw, RWKV_HEAD_DIM)
    args = [z, z, z, z, row2(mu[:rw]), row2(mu[rw:2 * rw]), row2(mu[2 * rw:3 * rw]), mu_l, wl,
            row2(w0), row2(a0), row2(k_k), row2(k_a), ones_bd]
    specs = [zspec(c0), zspec(c0 + 1), zspec(c0 + 2),
             pl.BlockSpec((tm, LORA_PAD), lambda i: (i, (ret_in + 3 * rw) // LORA_PAD)),
             vec, vec, vec, full(mu_l), full(wl), vec, vec, vec, vec, full(ones_bd)]
    if has_vres:
        v0, v1, v2 = v_res
        rank_v = v1.shape[1]
        rank_pad = -(-rank_v // V7X_LANES) * V7X_LANES
        v1p = jnp.zeros((rw, rank_pad), F32).at[:, :rank_v].set(v1).astype(BF16)
        v2p = jnp.zeros((rank_pad, rw), F32).at[:rank_v, :].set(v2).astype(BF16)
        args += [v_first, row2(v0), v1p, v2p]
        specs += [pl.BlockSpec((tm, rw), lambda i: (i, 0)), vec, full(v1p), full(v2p)]
    out_spec = pl.BlockSpec((tm, rw), lambda i: (i, 0))
    vmem = (2 * (3 + has_vres) * _nbytes((tm, rw), F32) + 2 * _nbytes((tm, LORA_PAD), F32)
            + 2 * _nbytes(wl.shape, BF16) + 2 * _nbytes((rw, rw), BF16)
            + 14 * _nbytes((tm, rw), F32) + 12 * _nbytes((tm, rw), F32))
    return pl.pallas_call(
        functools.partial(_rwkv_prep_kernel, has_vres, t, (rank_w, rank_a)),
        grid=(n // tm,),
        in_specs=specs,
        out_specs=[out_spec] * 7,
        out_shape=[jax.ShapeDtypeStruct((n, rw), F32)] * 7,
        scratch_shapes=[pltpu.VMEM((1, rw), F32)] * 3 + [pltpu.VMEM((1, LORA_PAD), F32)],
        compiler_params=_params(("arbitrary",), vmem),
        name="rwkv_prep",
    )(*args)


def _rwkv_core_kernel(r_ref, lw_ref, k_ref, v_ref, p_ref, a_ref, g_ref, rk_ref, lnw_ref, lnb_ref,
                      o_ref, h_ref, y_ref):
    @pl.when(pl.program_id(2) == 0)
    def _():
        h_ref[...] = jnp.zeros_like(h_ref)

    L = RWKV_CHUNK
    W = RWKV_PAIR
    hd = RWKV_HEAD_DIM
    n_chunks = r_ref.shape[0] // L
    n_pairs = r_ref.shape[1] // W

    ri = lax.broadcasted_iota(jnp.int32, (W, W), 0)
    ci = lax.broadcasted_iota(jnp.int32, (W, W), 1)
    same_head = (ri // hd) == (ci // hd)
    strict = same_head & ((ri % hd) > (ci % hd))
    incl = same_head & ((ri % hd) >= (ci % hd))
    eye = (ri == ci).astype(F32)
    merge_masks = []
    s = 1
    while s < hd:
        merge_masks.append(((ri // (2 * s)) == (ci // (2 * s))) & ((ri // s) != (ci // s)))
        s *= 2
    ones_bd = same_head.astype(BF16)
    ti = lax.broadcasted_iota(jnp.int32, (L, L), 0)
    si = lax.broadcasted_iota(jnp.int32, (L, L), 1)
    tri = (ti >= si).astype(BF16)
    bd_mask = (lax.broadcasted_iota(jnp.int32, (2 * L, W), 0) // L) == (
        lax.broadcasted_iota(jnp.int32, (2 * L, W), 1) // hd)

    def bd(x):
        return jnp.where(bd_mask, jnp.concatenate([x, x], axis=0), 0.0)

    def fold(x):
        return x[:L, :] + x[L:, :]

    def pair_chunk(sl, g):
        ln = slice(g * W, (g + 1) * W)
        r, lw, k, v, p = r_ref[sl, ln], lw_ref[sl, ln], k_ref[sl, ln], v_ref[sl, ln], p_ref[sl, ln]
        q = p * a_ref[sl, ln]
        l_hi, l_mid, l_lo = _split3(lw)
        c = _dot(tri, l_hi) + _dot(tri, l_mid) + _dot(tri, l_lo)
        yield
        c_last = c[L - 1:L, :]
        e_in = jnp.exp(c)
        e_out = jnp.exp(-c)
        e_end = jnp.exp(c_last - c)
        rt, kt, qt = r * e_in, k * e_out, q * e_out
        pt = p * jnp.exp(c - lw)
        kh, qh = k * e_end, q * e_end

        pt_bd, rt_bd = bd(pt), bd(rt)
        lhs = jnp.concatenate([pt_bd, rt_bd], axis=0).astype(BF16)
        rhs = jnp.concatenate([bd(qt), bd(kt)], axis=0).astype(BF16)
        m = _dot_nt(lhs, rhs)
        yield
        a_pq = jnp.where(strict, m[:W, :W], 0.0)
        a_pk = jnp.where(strict, m[:W, W:], 0.0).astype(BF16)
        a_rq = jnp.where(incl, m[W:, :W], 0.0).astype(BF16)
        a_rk = jnp.where(incl, m[W:, W:], 0.0).astype(BF16)
        v_bd = bd(v).astype(BF16)
        pk_v = _dot(a_pk, v_bd)
        rk_v = _dot(a_rk, v_bd)
        yield

        t_inv = eye - jnp.where(merge_masks[0], a_pq, 0.0)
        for mask in merge_masks[1:]:
            tb = t_inv.astype(BF16)
            inner = _dot(jnp.where(mask, a_pq, 0.0).astype(BF16), tb)
            yield
            t_inv = t_inv - _dot(tb, inner.astype(BF16))
            yield
        t_inv = t_inv.astype(BF16)

        sol = _dot(t_inv, jnp.concatenate([pt_bd, pk_v], axis=1).astype(BF16))
        yield
        p_hat = sol[:, :W].astype(BF16)
        u0 = sol[:, W:].astype(BF16)
        r_hat = fold(rt_bd - _dot(a_rq, p_hat))
        y0 = fold(rk_v - _dot(a_rq, u0))
        qh_bd = bd(qh).astype(BF16)
        kh_bd = bd(kh).astype(BF16)
        g_mat = eye * jnp.exp(c_last) - _dot_tn(qh_bd, p_hat)
        h_add = _dot_tn(kh_bd, v_bd) - _dot_tn(qh_bd, u0)
        yield

        h = h_ref[g]
        h_hi, h_lo = _split2(h)
        y_ref[sl, ln] = _dot(r_hat.astype(BF16), h_hi) + y0
        g_hi, g_lo = _split2(g_mat)
        h_ref[g] = _dot(g_hi, h_hi) + _dot(g_hi, h_lo) + _dot(g_lo, h_hi) + h_add

    def chunk(i, carry):
        sl = pl.ds(pl.multiple_of(i * L, L), L)
        live = [pair_chunk(sl, g) for g in range(n_pairs)]
        while live:
            live = [gen for gen in live if next(gen, True) is None]
        return carry

    lax.fori_loop(0, n_chunks, chunk, 0)

    inv_hd = 1.0 / hd
    for g in range(n_pairs):
        ln = slice(g * W, (g + 1) * W)
        y = y_ref[:, ln]
        mu = _seg_sum(y, ones_bd) * inv_hd
        cen = y - mu
        var = _seg_sum(cen * cen, ones_bd) * inv_hd
        yn = cen * lax.rsqrt(var + RWKV_LN_EPS) * lnw_ref[:, ln] + lnb_ref[:, ln]
        bonus = _seg_sum(r_ref[:, ln] * k_ref[:, ln] * rk_ref[:, ln], ones_bd) * v_ref[:, ln]
        o_ref[:, ln] = ((yn + bonus) * g_ref[:, ln]).astype(BF16)


def _rwkv_core(r, lw, k, v, p, a, gate, r_k, lnx_w, lnx_b, b, t):
    n, rw = r.shape
    assert rw % RWKV_PAIR == 0 and t % RWKV_CHUNK == 0
    group = _tile(rw, RWKV_GROUP_LANES, RWKV_PAIR)
    blk = _tile(t, 256, RWKV_CHUNK)
    nblk = t // blk
    tok = pl.BlockSpec((blk, group), lambda bi, pi, ci: (bi * nblk + ci, pi))
    vec = pl.BlockSpec((1, group), lambda bi, pi, ci: (0, pi))
    row2 = lambda x: x.reshape(1, rw)
    vmem = (2 * 7 * _nbytes((blk, group), F32) + 2 * _nbytes((blk, group), BF16)
            + _nbytes((blk, group), F32)
            + 64 * (group // RWKV_PAIR) * _nbytes((2 * RWKV_PAIR, 2 * RWKV_PAIR), F32)
            + 8 * _nbytes((blk, group), F32))
    return pl.pallas_call(
        _rwkv_core_kernel,
        grid=(b, rw // group, nblk),
        in_specs=[tok] * 7 + [vec] * 3,
        out_specs=tok,
        out_shape=jax.ShapeDtypeStruct((n, rw), BF16),
        scratch_shapes=[pltpu.VMEM((group // RWKV_PAIR, RWKV_PAIR, RWKV_PAIR), F32),
                        pltpu.VMEM((blk, group), F32)],
        compiler_params=_params(("parallel", "parallel", "arbitrary"), vmem),
        name="rwkv_core",
    )(r, lw, k, v, p, a, gate, row2(r_k), row2(lnx_w), row2(lnx_b))


def _lru_kernel(y_ref, x_ref, cw_ref, cb_ref, gxw_ref, gxb_ref, gaw_ref, gab_ref, lam_ref,
                o_ref, xtail_ref, h_ref):
    @pl.when(pl.program_id(2) == 0)
    def _():
        xtail_ref[...] = jnp.zeros_like(xtail_ref)
        h_ref[...] = jnp.zeros_like(h_ref)

    lt = x_ref.shape[0]
    tail = xtail_ref.shape[0]
    x = x_ref[...]
    ext = jnp.concatenate([xtail_ref[...], x], axis=0)
    xtail_ref[...] = x[lt - tail:, :]
    cw = cw_ref[...]
    xc = cb_ref[...] + cw[CONV_WIDTH - 1:CONV_WIDTH, :] * x
    for j in range(1, CONV_WIDTH):
        xc = xc + cw[CONV_WIDTH - 1 - j:CONV_WIDTH - j, :] * pltpu.roll(ext, j, 0)[tail:, :]

    xcb = xc.astype(BF16)
    gate_x = _sigmoid(_dot(xcb, gxw_ref[0]) + gxb_ref[...])
    gate_a = _sigmoid(_dot(xcb, gaw_ref[0]) + gab_ref[...])
    neg_lam = -lam_ref[...]
    softplus = jnp.maximum(neg_lam, 0.0) + jnp.log(1.0 + jnp.exp(-jnp.abs(neg_lam)))
    log_a = -LRU_C * gate_a * softplus
    a = jnp.exp(log_a)
    bb = xc * gate_x * jnp.sqrt(-jnp.tanh(log_a) * (a * a + 1.0))

    row = lax.broadcasted_iota(jnp.int32, (lt, 1), 0)
    s = 1
    while s < lt:
        valid = row >= s
        b_prev = jnp.where(valid, pltpu.roll(bb, s, 0), 0.0)
        a_prev = jnp.where(valid, pltpu.roll(a, s, 0), 1.0)
        bb = bb + a * b_prev
        a = a * a_prev
        s *= 2
    hs = a * h_ref[...] + bb
    h_ref[...] = hs[lt - 1:lt, :]

    y = y_ref[...]
    gelu = 0.5 * y * (1.0 + jnp.tanh(math.sqrt(2.0 / math.pi) * (y + 0.044715 * (y * y * y))))
    o_ref[...] = (gelu * hs).astype(BF16)


def _lru(z, b, t, d, conv_w, conv_b, gx_w, gx_b, ga_w, ga_b, lam):
    n = z.shape[0]
    assert d % LRU_BLOCK == 0
    nb = d // LRU_BLOCK
    lt = _tile(t, 256)
    nblk = t // lt
    tail = V7X_SUBLANES
    assert CONV_WIDTH - 1 <= tail <= lt
    tok = lambda off: pl.BlockSpec((lt, LRU_BLOCK), lambda bi, ni, ci: (bi * nblk + ci, off + ni))
    vec = lambda rows: pl.BlockSpec((rows, LRU_BLOCK), lambda bi, ni, ci: (0, ni))
    wsp = pl.BlockSpec((1, LRU_BLOCK, LRU_BLOCK), lambda bi, ni, ci: (ni, 0, 0))
    row2 = lambda x: x.reshape(1, d)
    vmem = (6 * _nbytes((lt, LRU_BLOCK), F32) + 8 * _nbytes((LRU_BLOCK, LRU_BLOCK), BF16)
            + 24 * _nbytes((lt, LRU_BLOCK), F32))
    return pl.pallas_call(
        _lru_kernel,
        grid=(b, nb, nblk),
        in_specs=[tok(0), tok(nb), vec(CONV_WIDTH), vec(1), wsp, vec(1), wsp, vec(1), vec(1)],
        out_specs=tok(0),
        out_shape=jax.ShapeDtypeStruct((n, d), BF16),
        scratch_shapes=[pltpu.VMEM((tail, LRU_BLOCK), F32), pltpu.VMEM((1, LRU_BLOCK), F32)],
        compiler_params=_params(("parallel", "parallel", "arbitrary"), vmem),
        name="rg_lru",
    )(z, z, conv_w, row2(conv_b), gx_w.astype(BF16), row2(gx_b), ga_w.astype(BF16), row2(ga_b),
      row2(lam))


def _even_layer(h, b, t, norm_g, w_in, w_out, mu, w0, w2, a0, a2, g2, k_k, k_a, r_k, lnx_w, lnx_b,
                v_first, v_res, rope):
    d = h.shape[1]
    ret_w = d // 2
    rw = d - ret_w
    ret_in = 4 * ret_w
    n_in = w_in.shape[1]
    n_pad = ret_in + 3 * rw + LORA_PAD
    w_in_p = jnp.pad(w_in, ((0, 0), (0, n_pad - n_in))).astype(BF16)
    z = _norm_matmul(h, norm_g, w_in_p, "even_in_proj")
    out_ret = _retention(z, b, t, ret_w, *rope)
    r, lw, k, v, p, a, gate = _rwkv_prep(z, t, ret_in, rw, mu, w2, a2, g2, w0, a0, k_k, k_a,
                                         v_first, v_res)
    out_rw = _rwkv_core(r, lw, k, v, p, a, gate, r_k.reshape(-1), lnx_w, lnx_b, b, t)
    mixed_in = jnp.concatenate([out_ret, out_rw], axis=-1)
    h = _matmul_residual(mixed_in, w_out.astype(BF16), h, "even_out_proj")
    return h, (v if v_res is None else v_first)


def _odd_layer(h, b, t, norm_g, w_in, conv_w, conv_b, gx_w, gx_b, ga_w, ga_b, lam, w_out):
    d = h.shape[1]
    z = _norm_matmul(h, norm_g, w_in.astype(BF16), "odd_in_proj")
    gated = _lru(z, b, t, d, conv_w, conv_b, gx_w, gx_b, ga_w, ga_b, lam)
    return _matmul_residual(gated, w_out.astype(BF16), h, "odd_out_proj")


def kernel(x, ev_norm, ev_w_in, ev_w_out, rw_mu, rw_w0, rw_w2, rw_a0, rw_a2, rw_g2, rw_k_k, rw_k_a, rw_r_k, rw_lnx_w, rw_lnx_b, rw_v0, rw_v1, rw_v2, od_norm, od_w_in, od_conv_w, od_conv_b, od_gx_w, od_gx_b, od_ga_w, od_ga_b, od_lam, od_w_out, ff_norm, ff_w1, ff_w2, final_norm):
    b, t, d = x.shape
    depth = ff_norm.shape[0]
    h = x.reshape(b * t, d)
    rope = _rope_tables(t, (d // 2) // RET_HEADS)
    v_first = None
    for layer in range(depth):
        if layer % 2 == 0:
            e = layer // 2
            v_res = None if e == 0 else (rw_v0[e - 1], rw_v1[e - 1], rw_v2[e - 1])
            h, v_first = _even_layer(
                h, b, t, ev_norm[e], ev_w_in[e], ev_w_out[e], rw_mu[e], rw_w0[e], rw_w2[e],
                rw_a0[e], rw_a2[e], rw_g2[e], rw_k_k[e], rw_k_a[e], rw_r_k[e], rw_lnx_w[e],
                rw_lnx_b[e], v_first, v_res, rope)
        else:
            o = layer // 2
            h = _odd_layer(h, b, t, od_norm[o], od_w_in[o], od_conv_w[o], od_conv_b[o], od_gx_w[o],
                           od_gx_b[o], od_ga_w[o], od_ga_b[o], od_lam[o], od_w_out[o])
        h = _mlp(h, ff_norm[layer], ff_w1[layer].astype(BF16), ff_w2[layer].astype(BF16),
                 f"mlp_{layer}")
    return _final_norm(h, final_norm).reshape(b, t, d)
```

```python
import functools
import math

import jax
import jax.numpy as jnp
from jax import lax
from jax.experimental import pallas as pl
from jax.experimental.pallas import tpu as pltpu

F32 = jnp.float32
BF16 = jnp.bfloat16

NORM_EPS = 1e-6
RET_HEADS = 4
RET_GN_EPS = 1e-5
ROPE_BASE = 10000.0
STREAM_CHUNK = 64
RWKV_HEAD_DIM = 64
RWKV_LN_EPS = 64e-5
LRU_BLOCK = 256
CONV_WIDTH = 4
LRU_C = 8.0

V7X_LANES = 128
V7X_SUBLANES = 8
V7X_VMEM_BYTES = 64 * 1024 * 1024
VMEM_CAP_BYTES = V7X_VMEM_BYTES - 8 * 1024 * 1024
VMEM_FLOOR_BYTES = 16 * 1024 * 1024

RWKV_CHUNK = 64
RWKV_PAIR = 2 * RWKV_HEAD_DIM
RWKV_GROUP_LANES = 8 * RWKV_PAIR
LORA_PAD = 512


def _tile(n, pref, mult=V7X_SUBLANES):
    if n <= pref:
        return n
    t = (pref // mult) * mult
    while t >= mult:
        if n % t == 0:
            return t
        t -= mult
    raise ValueError(f"no tile for {n} <= {pref}")


def _params(semantics, vmem_bytes):
    limit = int(min(max(vmem_bytes, VMEM_FLOOR_BYTES), VMEM_CAP_BYTES))
    return pltpu.CompilerParams(dimension_semantics=semantics, vmem_limit_bytes=limit)


def _nbytes(shape, dtype):
    return math.prod(shape) * jnp.dtype(dtype).itemsize


def _dot(a, b):
    return jnp.dot(a, b, preferred_element_type=F32)


def _dot_nt(a, b):
    return lax.dot_general(a, b, (((1,), (1,)), ((), ())), preferred_element_type=F32)


def _dot_tn(a, b):
    return lax.dot_general(a, b, (((0,), (0,)), ((), ())), preferred_element_type=F32)


def _split2(x):
    hi = x.astype(BF16)
    lo = (x - hi.astype(F32)).astype(BF16)
    return hi, lo


def _split3(x):
    hi = x.astype(BF16)
    r1 = x - hi.astype(F32)
    mid = r1.astype(BF16)
    lo = (r1 - mid.astype(F32)).astype(BF16)
    return hi, mid, lo


def _seg_sum(x, ones_bd):
    hi, lo = _split2(x)
    return _dot(hi, ones_bd) + _dot(lo, ones_bd)


def _rms(x, g):
    ms = jnp.mean(x * x, axis=-1, keepdims=True)
    return x * lax.rsqrt(ms + NORM_EPS) * g


def _sigmoid(x):
    return jax.nn.sigmoid(x)


def _norm_matmul_kernel(x_ref, g_ref, w_ref, o_ref, xn_ref):
    @pl.when(pl.program_id(1) == 0)
    def _():
        xn_ref[...] = _rms(x_ref[...], g_ref[...]).astype(BF16)

    o_ref[...] = _dot(xn_ref[...], w_ref[...])


def _norm_matmul(h, g, w_bf16, name):
    n, d = h.shape
    m = w_bf16.shape[1]
    tm = _tile(n, 1024)
    tn = _tile(m, 1536, V7X_LANES)
    vmem = (_nbytes((tm, d), F32) + _nbytes((tm, d), BF16) + 2 * _nbytes((d, tn), BF16)
            + 3 * _nbytes((tm, tn), F32) + _nbytes((tm, d), F32))
    return pl.pallas_call(
        _norm_matmul_kernel,
        grid=(n // tm, m // tn),
        in_specs=[pl.BlockSpec((tm, d), lambda i, j: (i, 0), pipeline_mode=pl.Buffered(1)),
                  pl.BlockSpec((1, d), lambda i, j: (0, 0)),
                  pl.BlockSpec((d, tn), lambda i, j: (0, j))],
        out_specs=pl.BlockSpec((tm, tn), lambda i, j: (i, j)),
        out_shape=jax.ShapeDtypeStruct((n, m), F32),
        scratch_shapes=[pltpu.VMEM((tm, d), BF16)],
        compiler_params=_params(("parallel", "arbitrary"), vmem),
        name=name,
    )(h, g.reshape(1, d), w_bf16)


def _matmul_residual_kernel(n_terms, *refs):
    x_refs, w_refs = refs[:n_terms], refs[n_terms:2 * n_terms]
    r_ref, o_ref = refs[2 * n_terms:]
    acc = r_ref[...]
    for x_ref, w_ref in zip(x_refs, w_refs):
        acc = acc + _dot(x_ref[...], w_ref[...])
    o_ref[...] = acc


def _matmul_residual(xs_bf16, ws_bf16, res, name):
    n, m = res.shape
    tm = _tile(n, 1024)
    tn = _tile(m, 1024, V7X_LANES)
    ksum = sum(x.shape[1] for x in xs_bf16)
    vmem = (2 * _nbytes((tm, ksum), BF16) + 2 * _nbytes((ksum, tn), BF16)
            + (5 + len(xs_bf16)) * _nbytes((tm, tn), F32))
    x_specs = [pl.BlockSpec((tm, x.shape[1]), lambda i, j: (i, 0)) for x in xs_bf16]
    w_specs = [pl.BlockSpec((w.shape[0], tn), lambda i, j: (0, j)) for w in ws_bf16]
    return pl.pallas_call(
        functools.partial(_matmul_residual_kernel, len(xs_bf16)),
        grid=(n // tm, m // tn),
        in_specs=x_specs + w_specs + [pl.BlockSpec((tm, tn), lambda i, j: (i, j))],
        out_specs=pl.BlockSpec((tm, tn), lambda i, j: (i, j)),
        out_shape=jax.ShapeDtypeStruct((n, m), F32),
        compiler_params=_params(("parallel", "arbitrary"), vmem),
        name=name,
    )(*xs_bf16, *ws_bf16, res)


def _mlp_kernel(x_ref, g_ref, w1_ref, w2_ref, o_ref, xn_ref):
    @pl.when(pl.program_id(1) == 0)
    def _():
        x = x_ref[...]
        xn_ref[...] = _rms(x, g_ref[...]).astype(BF16)
        o_ref[...] = x

    a = jnp.maximum(_dot(xn_ref[...], w1_ref[...]), 0.0)
    o_ref[...] += _dot((a * a).astype(BF16), w2_ref[...])


def _mlp(h, g, w1_bf16, w2_bf16, name):
    n, d = h.shape
    f = w1_bf16.shape[1]
    tm = _tile(n, 1024)
    tf = _tile(f, 512, V7X_LANES)
    vmem = (3 * _nbytes((tm, d), F32) + _nbytes((tm, d), BF16) + 4 * _nbytes((d, tf), BF16)
            + 3 * _nbytes((tm, tf), F32) + _nbytes((tm, d), F32))
    return pl.pallas_call(
        _mlp_kernel,
        grid=(n // tm, f // tf),
        in_specs=[pl.BlockSpec((tm, d), lambda i, j: (i, 0), pipeline_mode=pl.Buffered(1)),
                  pl.BlockSpec((1, d), lambda i, j: (0, 0)),
                  pl.BlockSpec((d, tf), lambda i, j: (0, j)),
                  pl.BlockSpec((tf, d), lambda i, j: (j, 0))],
        out_specs=pl.BlockSpec((tm, d), lambda i, j: (i, 0)),
        out_shape=jax.ShapeDtypeStruct((n, d), F32),
        scratch_shapes=[pltpu.VMEM((tm, d), BF16)],
        compiler_params=_params(("parallel", "arbitrary"), vmem),
        name=name,
    )(h, g.reshape(1, d), w1_bf16, w2_bf16)


def _final_norm_kernel(x_ref, g_ref, o_ref):
    o_ref[...] = _rms(x_ref[...], g_ref[...])


def _final_norm(h, g):
    n, d = h.shape
    tm = _tile(n, 512)
    vmem = 6 * _nbytes((tm, d), F32)
    return pl.pallas_call(
        _final_norm_kernel,
        grid=(n // tm,),
        in_specs=[pl.BlockSpec((tm, d), lambda i: (i, 0)),
                  pl.BlockSpec((1, d), lambda i: (0, 0))],
        out_specs=pl.BlockSpec((tm, d), lambda i: (i, 0)),
        out_shape=jax.ShapeDtypeStruct((n, d), F32),
        compiler_params=_params(("parallel",), vmem),
        name="final_norm",
    )(h, g.reshape(1, d))


def _retention_tables(blk, dh):
    log_g = jnp.log1p(-jnp.exp2(-5.0 - jnp.arange(RET_HEADS, dtype=F32)))
    pos = jnp.arange(blk, dtype=F32)
    n, m = pos[:, None], pos[None, :]
    cn, cm = jnp.floor(n / STREAM_CHUNK), jnp.floor(m / STREAM_CHUNK)
    dist = jnp.where(cn == cm, jnp.abs(n - m), n - m)
    lg = log_g[:, None, None]
    dmask = jnp.where((cm <= cn)[None], jnp.exp(lg * dist[None]), 0.0)
    qdec = jnp.broadcast_to(jnp.exp(lg * (pos + 1.0)[None, :, None]), (RET_HEADS, blk, dh))
    kdec = jnp.broadcast_to(jnp.exp(lg * (blk - 1.0 - pos)[None, :, None]), (RET_HEADS, blk, dh))
    cdec = jnp.broadcast_to(jnp.exp(lg * blk), (RET_HEADS, 1, dh))
    return dmask, qdec, kdec, cdec


def _rope_tables(t, dh):
    inv = 1.0 / (ROPE_BASE ** (jnp.arange(0, dh, 2, dtype=F32) / dh))
    ang = jnp.arange(t, dtype=F32)[:, None] * inv[None, :]
    return jnp.cos(ang), jnp.sin(ang)


def _retention_kernel(q_ref, k_ref, v_ref, g_ref, cos_ref, sin_ref, dm_ref, qd_ref, kd_ref,
                      cd_ref, o_ref, s_ref):
    @pl.when(pl.program_id(2) == 0)
    def _():
        s_ref[...] = jnp.zeros_like(s_ref)

    dh = q_ref.shape[1]
    half = dh // 2
    cos, sin = cos_ref[...], sin_ref[...]

    def rot(t):
        t1, t2 = t[:, :half], t[:, half:]
        return jnp.concatenate([t1 * cos - t2 * sin, t2 * cos + t1 * sin], axis=-1)

    q = rot(q_ref[...])
    k = rot(k_ref[...]) * (dh ** -0.5)
    vb = v_ref[...].astype(BF16)
    scores = _dot_nt(q.astype(BF16), k.astype(BF16)) * dm_ref[0]
    state = s_ref[...]
    out = _dot(scores.astype(BF16), vb) + _dot((q * qd_ref[0]).astype(BF16), state.astype(BF16))
    s_ref[...] = state * cd_ref[0] + _dot_tn((k * kd_ref[0]).astype(BF16), vb)

    mu = jnp.mean(out, axis=-1, keepdims=True)
    cen = out - mu
    var = jnp.mean(cen * cen, axis=-1, keepdims=True)
    y = cen * lax.rsqrt(var + RET_GN_EPS)
    g = g_ref[...]
    o_ref[...] = (g * _sigmoid(g) * y).astype(BF16)


def _retention(z, b, t, ret_w, cos, sin):
    n = z.shape[0]
    dh = ret_w // RET_HEADS
    blk = _tile(t, 256, STREAM_CHUNK)
    nblk = t // blk
    dmask, qdec, kdec, cdec = _retention_tables(blk, dh)
    row = lambda bi, hi, ci: bi * nblk + ci
    col_spec = lambda off: pl.BlockSpec((blk, dh), lambda bi, hi, ci: (row(bi, hi, ci), off + hi))
    head_spec = lambda shape: pl.BlockSpec((1,) + shape, lambda bi, hi, ci: (hi, 0, 0))
    vmem = (2 * 4 * _nbytes((blk, dh), F32) + 4 * _nbytes((blk, dh // 2), F32)
            + 2 * _nbytes((blk, blk), F32) + 4 * _nbytes((blk, dh), F32) + _nbytes((dh, dh), F32)
            + 12 * _nbytes((blk, max(blk, dh)), F32))
    return pl.pallas_call(
        _retention_kernel,
        grid=(b, RET_HEADS, nblk),
        in_specs=[col_spec(0), col_spec(RET_HEADS), col_spec(2 * RET_HEADS), col_spec(3 * RET_HEADS),
                  pl.BlockSpec((blk, dh // 2), lambda bi, hi, ci: (ci, 0)),
                  pl.BlockSpec((blk, dh // 2), lambda bi, hi, ci: (ci, 0)),
                  head_spec((blk, blk)), head_spec((blk, dh)), head_spec((blk, dh)),
                  head_spec((1, dh))],
        out_specs=pl.BlockSpec((blk, dh), lambda bi, hi, ci: (row(bi, hi, ci), hi)),
        out_shape=jax.ShapeDtypeStruct((n, ret_w), BF16),
        scratch_shapes=[pltpu.VMEM((dh, dh), F32)],
        compiler_params=_params(("parallel", "parallel", "arbitrary"), vmem),
        name="retention",
    )(z, z, z, z, cos, sin, dmask, qdec, kdec, cdec)


def _rwkv_prep_kernel(has_vres, t_len, lora_ranks, *refs):
    if has_vres:
        (zr_ref, zk_ref, zv_ref, zl_ref, mur_ref, muk_ref, muv_ref, mul_ref, wl_ref, w0_ref, a0_ref,
         kk_ref, ka_ref, ones_ref, vf_ref, v0_ref, v1_ref, v2_ref,
         r_out, lw_out, k_out, v_out, p_out, a_out, g_out, cr, ck, cv, cl) = refs
    else:
        (zr_ref, zk_ref, zv_ref, zl_ref, mur_ref, muk_ref, muv_ref, mul_ref, wl_ref, w0_ref, a0_ref,
         kk_ref, ka_ref, ones_ref,
         r_out, lw_out, k_out, v_out, p_out, a_out, g_out, cr, ck, cv, cl) = refs
    tm = zr_ref.shape[0]
    rw = zr_ref.shape[1]
    at_start = (pl.program_id(0) * tm) % t_len == 0
    row0 = lax.broadcasted_iota(jnp.int32, (tm, 1), 0) == 0

    def shift_mix(x_ref, carry_ref, mu_ref):
        x = x_ref[...]
        last = jnp.where(at_start, 0.0, carry_ref[...])
        prev = jnp.where(row0, last, pltpu.roll(x, 1, 0))
        carry_ref[...] = x[tm - 1:tm, :]
        return x + mu_ref[...] * (prev - x)

    r = shift_mix(zr_ref, cr, mur_ref)
    kr = shift_mix(zk_ref, ck, muk_ref)
    vr = shift_mix(zv_ref, cv, muv_ref)
    lo = shift_mix(zl_ref, cl, mul_ref)

    rank_w, rank_a = lora_ranks
    lane = lax.broadcasted_iota(jnp.int32, (1, lo.shape[1]), 1)
    feat = jnp.where(lane < rank_w, jnp.tanh(lo), jnp.where(lane < rank_w + rank_a, lo, _sigmoid(lo)))
    proj = _dot(feat.astype(BF16), wl_ref[...])
    log_w = -math.exp(-0.5) * _sigmoid(w0_ref[...] + proj[:, :rw])
    iclr = _sigmoid(a0_ref[...] + proj[:, rw:2 * rw])
    gate = proj[:, 2 * rw:]

    if has_vres:
        low = _dot(vr.astype(BF16), v1_ref[...])
        mix = _sigmoid(v0_ref[...] + _dot(low.astype(BF16), v2_ref[...]))
        vr = vr + (vf_ref[...] - vr) * mix

    kk = kr * kk_ref[...]
    norm = jnp.sqrt(_seg_sum(kk * kk, ones_ref[...]))
    kk = kk / jnp.maximum(norm, 1e-12)
    k2 = kr * (1.0 + (iclr - 1.0) * ka_ref[...])

    r_out[...] = r
    lw_out[...] = log_w
    k_out[...] = k2
    v_out[...] = vr
    p_out[...] = kk
    a_out[...] = iclr
    g_out[...] = gate


def _head_ones(width, head):
    idx = jnp.arange(width) // head
    return (idx[:, None] == idx[None, :]).astype(BF16)


def _rwkv_prep(z, t, ret_in, rw, mu, w2, a2, g2, w0, a0, k_k, k_a, v_first, v_res):
    n = z.shape[0]
    has_vres = v_res is not None
    rank_w, rank_a, rank_g = w2.shape[0], a2.shape[0], g2.shape[0]
    lora = rank_w + rank_a + rank_g
    assert lora <= LORA_PAD and ret_in % rw == 0 and (ret_in + 3 * rw) % LORA_PAD == 0
    tm = _tile(t, 256)
    row2 = lambda v: v.reshape(1, -1)
    wl = jnp.zeros((LORA_PAD, 3 * rw), F32)
    wl = wl.at[:rank_w, :rw].set(w2).at[rank_w:rank_w + rank_a, rw:2 * rw].set(a2)
    wl = wl.at[rank_w + rank_a:lora, 2 * rw:].set(g2).astype(BF16)
    mu_l = jnp.zeros((1, LORA_PAD), F32).at[0, :lora].set(mu[3 * rw:])
    c0 = ret_in // rw
    zspec = lambda ci: pl.BlockSpec((tm, rw), lambda i: (i, ci))
    vec = pl.BlockSpec((1, rw), lambda i: (0, 0))
    full = lambda a: pl.BlockSpec(a.shape, lambda i: (0,) * a.ndim)
    ones_bd = _head_ones(rw, RWKV_HEAD_DIM)
    args = [z, z, z, z, row2(mu[:rw]), row2(mu[rw:2 * rw]), row2(mu[2 * rw:3 * rw]), mu_l, wl,
            row2(w0), row2(a0), row2(k_k), row2(k_a), ones_bd]
    specs = [zspec(c0), zspec(c0 + 1), zspec(c0 + 2),
             pl.BlockSpec((tm, LORA_PAD), lambda i: (i, (ret_in + 3 * rw) // LORA_PAD)),
             vec, vec, vec, full(mu_l), full(wl), vec, vec, vec, vec, full(ones_bd)]
    if has_vres:
        v0, v1, v2 = v_res
        rank_v = v1.shape[1]
        rank_pad = -(-rank_v // V7X_LANES) * V7X_LANES
        v1p = jnp.zeros((rw, rank_pad), F32).at[:, :rank_v].set(v1).astype(BF16)
        v2p = jnp.zeros((rank_pad, rw), F32).at[:rank_v, :].set(v2).astype(BF16)
        args += [v_first, row2(v0), v1p, v2p]
        specs += [pl.BlockSpec((tm, rw), lambda i: (i, 0)), vec, full(v1p), full(v2p)]
    out_spec = pl.BlockSpec((tm, rw), lambda i: (i, 0))
    vmem = (2 * (3 + has_vres) * _nbytes((tm, rw), F32) + 2 * _nbytes((tm, LORA_PAD), F32)
            + 2 * _nbytes(wl.shape, BF16) + 2 * _nbytes((rw, rw), BF16)
            + 14 * _nbytes((tm, rw), F32) + 12 * _nbytes((tm, rw), F32))
    return pl.pallas_call(
        functools.partial(_rwkv_prep_kernel, has_vres, t, (rank_w, rank_a)),
        grid=(n // tm,),
        in_specs=specs,
        out_specs=[out_spec] * 7,
        out_shape=[jax.ShapeDtypeStruct((n, rw), F32)] * 7,
        scratch_shapes=[pltpu.VMEM((1, rw), F32)] * 3 + [pltpu.VMEM((1, LORA_PAD), F32)],
        compiler_params=_params(("arbitrary",), vmem),
        name="rwkv_prep",
    )(*args)


def _rwkv_core_kernel(r_ref, lw_ref, k_ref, v_ref, p_ref, a_ref, g_ref, rk_ref, lnw_ref, lnb_ref,
                      o_ref, h_ref, y_ref):
    @pl.when(pl.program_id(2) == 0)
    def _():
        h_ref[...] = jnp.zeros_like(h_ref)

    L = RWKV_CHUNK
    W = RWKV_PAIR
    hd = RWKV_HEAD_DIM
    n_chunks = r_ref.shape[0] // L
    n_pairs = r_ref.shape[1] // W

    ri = lax.broadcasted_iota(jnp.int32, (W, W), 0)
    ci = lax.broadcasted_iota(jnp.int32, (W, W), 1)
    same_head = (ri // hd) == (ci // hd)
    strict = same_head & ((ri % hd) > (ci % hd))
    incl = same_head & ((ri % hd) >= (ci % hd))
    eye = (ri == ci).astype(F32)
    merge_masks = []
    s = 1
    while s < hd:
        merge_masks.append(((ri // (2 * s)) == (ci // (2 * s))) & ((ri // s) != (ci // s)))
        s *= 2
    ones_bd = same_head.astype(BF16)
    ti = lax.broadcasted_iota(jnp.int32, (L, L), 0)
    si = lax.broadcasted_iota(jnp.int32, (L, L), 1)
    tri = (ti >= si).astype(BF16)
    bd_mask = (lax.broadcasted_iota(jnp.int32, (2 * L, W), 0) // L) == (
        lax.broadcasted_iota(jnp.int32, (2 * L, W), 1) // hd)

    def bd(x):
        return jnp.where(bd_mask, jnp.concatenate([x, x], axis=0), 0.0)

    def fold(x):
        return x[:L, :] + x[L:, :]

    def pair_chunk(sl, g):
        ln = slice(g * W, (g + 1) * W)
        r, lw, k, v, p = r_ref[sl, ln], lw_ref[sl, ln], k_ref[sl, ln], v_ref[sl, ln], p_ref[sl, ln]
        q = p * a_ref[sl, ln]
        l_hi, l_mid, l_lo = _split3(lw)
        c = _dot(tri, l_hi) + _dot(tri, l_mid) + _dot(tri, l_lo)
        yield
        c_last = c[L - 1:L, :]
        e_in = jnp.exp(c)
        e_out = jnp.exp(-c)
        e_end = jnp.exp(c_last - c)
        rt, kt, qt = r * e_in, k * e_out, q * e_out
        pt = p * jnp.exp(c - lw)
        kh, qh = k * e_end, q * e_end

        pt_bd, rt_bd = bd(pt), bd(rt)
        lhs = jnp.concatenate([pt_bd, rt_bd], axis=0).astype(BF16)
        rhs = jnp.concatenate([bd(qt), bd(kt)], axis=0).astype(BF16)
        m = _dot_nt(lhs, rhs)
        yield
        a_pq = jnp.where(strict, m[:W, :W], 0.0)
        a_pk = jnp.where(strict, m[:W, W:], 0.0).astype(BF16)
        a_rq = jnp.where(incl, m[W:, :W], 0.0).astype(BF16)
        a_rk = jnp.where(incl, m[W:, W:], 0.0).astype(BF16)
        v_bd = bd(v).astype(BF16)
        pk_v = _dot(a_pk, v_bd)
        rk_v = _dot(a_rk, v_bd)
        yield

        t_inv = eye - jnp.where(merge_masks[0], a_pq, 0.0)
        for mask in merge_masks[1:]:
            tb = t_inv.astype(BF16)
            inner = _dot(jnp.where(mask, a_pq, 0.0).astype(BF16), tb)
            yield
            t_inv = t_inv - _dot(tb, inner.astype(BF16))
            yield
        t_inv = t_inv.astype(BF16)

        sol = _dot(t_inv, jnp.concatenate([pt_bd, pk_v], axis=1).astype(BF16))
        yield
        p_hat = sol[:, :W].astype(BF16)
        u0 = sol[:, W:].astype(BF16)
        r_hat = fold(rt_bd - _dot(a_rq, p_hat))
        y0 = fold(rk_v - _dot(a_rq, u0))
        qh_bd = bd(qh).astype(BF16)
        kh_bd = bd(kh).astype(BF16)
        g_mat = eye * jnp.exp(c_last) - _dot_tn(qh_bd, p_hat)
        h_add = _dot_tn(kh_bd, v_bd) - _dot_tn(qh_bd, u0)
        yield

        h = h_ref[g]
        h_hi, h_lo = _split2(h)
        y_ref[sl, ln] = _dot(r_hat.astype(BF16), h_hi) + y0
        g_hi, g_lo = _split2(g_mat)
        h_ref[g] = _dot(g_hi, h_hi) + _dot(g_hi, h_lo) + _dot(g_lo, h_hi) + h_add

    def chunk(i, carry):
        sl = pl.ds(pl.multiple_of(i * L, L), L)
        live = [pair_chunk(sl, g) for g in range(n_pairs)]
        while live:
            live = [gen for gen in live if next(gen, True) is None]
        return carry

    lax.fori_loop(0, n_chunks, chunk, 0)

    inv_hd = 1.0 / hd
    for g in range(n_pairs):
        ln = slice(g * W, (g + 1) * W)
        y = y_ref[:, ln]
        mu = _seg_sum(y, ones_bd) * inv_hd
        cen = y - mu
        var = _seg_sum(cen * cen, ones_bd) * inv_hd
        yn = cen * lax.rsqrt(var + RWKV_LN_EPS) * lnw_ref[:, ln] + lnb_ref[:, ln]
        bonus = _seg_sum(r_ref[:, ln] * k_ref[:, ln] * rk_ref[:, ln], ones_bd) * v_ref[:, ln]
        o_ref[:, ln] = ((yn + bonus) * g_ref[:, ln]).astype(BF16)


def _rwkv_core(r, lw, k, v, p, a, gate, r_k, lnx_w, lnx_b, b, t):
    n, rw = r.shape
    assert rw % RWKV_PAIR == 0 and t % RWKV_CHUNK == 0
    group = _tile(rw, RWKV_GROUP_LANES, RWKV_PAIR)
    blk = _tile(t, 256, RWKV_CHUNK)
    nblk = t // blk
    tok = pl.BlockSpec((blk, group), lambda bi, pi, ci: (bi * nblk + ci, pi))
    vec = pl.BlockSpec((1, group), lambda bi, pi, ci: (0, pi))
    row2 = lambda x: x.reshape(1, rw)
    vmem = (2 * 7 * _nbytes((blk, group), F32) + 2 * _nbytes((blk, group), BF16)
            + _nbytes((blk, group), F32)
            + 64 * (group // RWKV_PAIR) * _nbytes((2 * RWKV_PAIR, 2 * RWKV_PAIR), F32)
            + 8 * _nbytes((blk, group), F32))
    return pl.pallas_call(
        _rwkv_core_kernel,
        grid=(b, rw // group, nblk),
        in_specs=[tok] * 7 + [vec] * 3,
        out_specs=tok,
        out_shape=jax.ShapeDtypeStruct((n, rw), BF16),
        scratch_shapes=[pltpu.VMEM((group // RWKV_PAIR, RWKV_PAIR, RWKV_PAIR), F32),
                        pltpu.VMEM((blk, group), F32)],
        compiler_params=_params(("parallel", "parallel", "arbitrary"), vmem),
        name="rwkv_core",
    )(r, lw, k, v, p, a, gate, row2(r_k), row2(lnx_w), row2(lnx_b))


def _lru_kernel(y_ref, x_ref, cw_ref, cb_ref, gxw_ref, gxb_ref, gaw_ref, gab_ref, lam_ref,
                o_ref, xtail_ref, h_ref):
    @pl.when(pl.program_id(2) == 0)
    def _():
        xtail_ref[...] = jnp.zeros_like(xtail_ref)
        h_ref[...] = jnp.zeros_like(h_ref)

    lt = x_ref.shape[0]
    tail = xtail_ref.shape[0]
    x = x_ref[...]
    ext = jnp.concatenate([xtail_ref[...], x], axis=0)
    xtail_ref[...] = x[lt - tail:, :]
    cw = cw_ref[...]
    xc = cb_ref[...] + cw[CONV_WIDTH - 1:CONV_WIDTH, :] * x
    for j in range(1, CONV_WIDTH):
        xc = xc + cw[CONV_WIDTH - 1 - j:CONV_WIDTH - j, :] * pltpu.roll(ext, j, 0)[tail:, :]

    xcb = xc.astype(BF16)
    gate_x = _sigmoid(_dot(xcb, gxw_ref[0]) + gxb_ref[...])
    gate_a = _sigmoid(_dot(xcb, gaw_ref[0]) + gab_ref[...])
    neg_lam = -lam_ref[...]
    softplus = jnp.maximum(neg_lam, 0.0) + jnp.log(1.0 + jnp.exp(-jnp.abs(neg_lam)))
    log_a = -LRU_C * gate_a * softplus
    a = jnp.exp(log_a)
    bb = xc * gate_x * jnp.sqrt(-jnp.tanh(log_a) * (a * a + 1.0))

    sub = V7X_SUBLANES
    n_groups = lt // sub
    a = a.reshape(n_groups, sub, a.shape[1])
    bb = bb.reshape(n_groups, sub, bb.shape[1])
    row_in_group = lax.broadcasted_iota(jnp.int32, (1, sub, 1), 1)
    s = 1
    while s < sub:
        valid = row_in_group >= s
        b_prev = jnp.where(valid, pltpu.roll(bb, s, 1), 0.0)
        a_prev = jnp.where(valid, pltpu.roll(a, s, 1), 1.0)
        bb = bb + a * b_prev
        a = a * a_prev
        s *= 2
    h_prev = h_ref[...]
    groups = []
    for gi in range(n_groups):
        hg = a[gi] * h_prev + bb[gi]
        groups.append(hg)
        h_prev = hg[sub - 1:sub, :]
    hs = jnp.concatenate(groups, axis=0)
    h_ref[...] = h_prev

    y = y_ref[...]
    gelu = 0.5 * y * (1.0 + jnp.tanh(math.sqrt(2.0 / math.pi) * (y + 0.044715 * (y * y * y))))
    o_ref[...] = (gelu * hs).astype(BF16)


def _lru(z, b, t, d, conv_w, conv_b, gx_w, gx_b, ga_w, ga_b, lam):
    n = z.shape[0]
    assert d % LRU_BLOCK == 0
    nb = d // LRU_BLOCK
    lt = _tile(t, 256)
    nblk = t // lt
    tail = V7X_SUBLANES
    assert CONV_WIDTH - 1 <= tail <= lt
    tok = lambda off: pl.BlockSpec((lt, LRU_BLOCK), lambda bi, ni, ci: (bi * nblk + ci, off + ni))
    vec = lambda rows: pl.BlockSpec((rows, LRU_BLOCK), lambda bi, ni, ci: (0, ni))
    wsp = pl.BlockSpec((1, LRU_BLOCK, LRU_BLOCK), lambda bi, ni, ci: (ni, 0, 0))
    row2 = lambda x: x.reshape(1, d)
    vmem = (6 * _nbytes((lt, LRU_BLOCK), F32) + 8 * _nbytes((LRU_BLOCK, LRU_BLOCK), BF16)
            + 24 * _nbytes((lt, LRU_BLOCK), F32))
    return pl.pallas_call(
        _lru_kernel,
        grid=(b, nb, nblk),
        in_specs=[tok(0), tok(nb), vec(CONV_WIDTH), vec(1), wsp, vec(1), wsp, vec(1), vec(1)],
        out_specs=tok(0),
        out_shape=jax.ShapeDtypeStruct((n, d), BF16),
        scratch_shapes=[pltpu.VMEM((tail, LRU_BLOCK), F32), pltpu.VMEM((1, LRU_BLOCK), F32)],
        compiler_params=_params(("parallel", "parallel", "arbitrary"), vmem),
        name="rg_lru",
    )(z, z, conv_w, row2(conv_b), gx_w.astype(BF16), row2(gx_b), ga_w.astype(BF16), row2(ga_b),
      row2(lam))


def _even_layer(h, b, t, norm_g, w_in, w_out, mu, w0, w2, a0, a2, g2, k_k, k_a, r_k, lnx_w, lnx_b,
                v_first, v_res, rope):
    d = h.shape[1]
    ret_w = d // 2
    rw = d - ret_w
    ret_in = 4 * ret_w
    n_in = w_in.shape[1]
    n_pad = ret_in + 3 * rw + LORA_PAD
    w_in_p = jnp.pad(w_in, ((0, 0), (0, n_pad - n_in))).astype(BF16)
    z = _norm_matmul(h, norm_g, w_in_p, "even_in_proj")
    out_ret = _retention(z, b, t, ret_w, *rope)
    r, lw, k, v, p, a, gate = _rwkv_prep(z, t, ret_in, rw, mu, w2, a2, g2, w0, a0, k_k, k_a,
                                         v_first, v_res)
    out_rw = _rwkv_core(r, lw, k, v, p, a, gate, r_k.reshape(-1), lnx_w, lnx_b, b, t)
    w_out_b = w_out.astype(BF16)
    h = _matmul_residual([out_ret, out_rw], [w_out_b[:ret_w], w_out_b[ret_w:]], h, "even_out_proj")
    return h, (v if v_res is None else v_first)


def _odd_layer(h, b, t, norm_g, w_in, conv_w, conv_b, gx_w, gx_b, ga_w, ga_b, lam, w_out):
    d = h.shape[1]
    z = _norm_matmul(h, norm_g, w_in.astype(BF16), "odd_in_proj")
    gated = _lru(z, b, t, d, conv_w, conv_b, gx_w, gx_b, ga_w, ga_b, lam)
    return _matmul_residual([gated], [w_out.astype(BF16)], h, "odd_out_proj")


def kernel(x, ev_norm, ev_w_in, ev_w_out, rw_mu, rw_w0, rw_w2, rw_a0, rw_a2, rw_g2, rw_k_k, rw_k_a, rw_r_k, rw_lnx_w, rw_lnx_b, rw_v0, rw_v1, rw_v2, od_norm, od_w_in, od_conv_w, od_conv_b, od_gx_w, od_gx_b, od_ga_w, od_ga_b, od_lam, od_w_out, ff_norm, ff_w1, ff_w2, final_norm):
    b, t, d = x.shape
    depth = ff_norm.shape[0]
    h = x.reshape(b * t, d)
    rope = _rope_tables(t, (d // 2) // RET_HEADS)
    v_first = None
    for layer in range(depth):
        if layer % 2 == 0:
            e = layer // 2
            v_res = None if e == 0 else (rw_v0[e - 1], rw_v1[e - 1], rw_v2[e - 1])
            h, v_first = _even_layer(
                h, b, t, ev_norm[e], ev_w_in[e], ev_w_out[e], rw_mu[e], rw_w0[e], rw_w2[e],
                rw_a0[e], rw_a2[e], rw_g2[e], rw_k_k[e], rw_k_a[e], rw_r_k[e], rw_lnx_w[e],
                rw_lnx_b[e], v_first, v_res, rope)
        else:
            o = layer // 2
            h = _odd_layer(h, b, t, od_norm[o], od_w_in[o], od_conv_w[o], od_conv_b[o], od_gx_w[o],
                           od_gx_b[o], od_ga_w[o], od_ga_b[o], od_lam[o], od_w_out[o])
        h = _mlp(h, ff_norm[layer], ff_w1[layer].astype(BF16), ff_w2[layer].astype(BF16),
                 f"mlp_{layer}")
    return _final_norm(h, final_norm).reshape(b, t, d)
```

```python
import functools
import math

import jax
import jax.numpy as jnp
from jax import lax
from jax.experimental import pallas as pl
from jax.experimental.pallas import tpu as pltpu

F32 = jnp.float32
BF16 = jnp.bfloat16

NORM_EPS = 1e-6
RET_HEADS = 4
RET_GN_EPS = 1e-5
ROPE_BASE = 10000.0
STREAM_CHUNK = 64
RWKV_HEAD_DIM = 64
RWKV_LN_EPS = 64e-5
LRU_BLOCK = 256
CONV_WIDTH = 4
LRU_C = 8.0

V7X_LANES = 128
V7X_SUBLANES = 8
V7X_VMEM_BYTES = 64 * 1024 * 1024
VMEM_CAP_BYTES = V7X_VMEM_BYTES - 8 * 1024 * 1024
VMEM_FLOOR_BYTES = 16 * 1024 * 1024

RWKV_CHUNK = 64
RWKV_PAIR = 2 * RWKV_HEAD_DIM
RWKV_GROUP_LANES = 8 * RWKV_PAIR
LORA_PAD = 512


def _tile(n, pref, mult=V7X_SUBLANES):
    if n <= pref:
        return n
    t = (pref // mult) * mult
    while t >= mult:
        if n % t == 0:
            return t
        t -= mult
    raise ValueError(f"no tile for {n} <= {pref}")


def _params(semantics, vmem_bytes):
    limit = int(min(max(vmem_bytes, VMEM_FLOOR_BYTES), VMEM_CAP_BYTES))
    return pltpu.CompilerParams(dimension_semantics=semantics, vmem_limit_bytes=limit)


def _nbytes(shape, dtype):
    return math.prod(shape) * jnp.dtype(dtype).itemsize


def _dot(a, b):
    return jnp.dot(a, b, preferred_element_type=F32)


def _dot_nt(a, b):
    return lax.dot_general(a, b, (((1,), (1,)), ((), ())), preferred_element_type=F32)


def _dot_tn(a, b):
    return lax.dot_general(a, b, (((0,), (0,)), ((), ())), preferred_element_type=F32)


def _split2(x):
    hi = x.astype(BF16)
    lo = (x - hi.astype(F32)).astype(BF16)
    return hi, lo


def _split3(x):
    hi = x.astype(BF16)
    r1 = x - hi.astype(F32)
    mid = r1.astype(BF16)
    lo = (r1 - mid.astype(F32)).astype(BF16)
    return hi, mid, lo


def _seg_sum(x, ones_bd):
    hi, lo = _split2(x)
    return _dot(hi, ones_bd) + _dot(lo, ones_bd)


def _rms(x, g):
    ms = jnp.mean(x * x, axis=-1, keepdims=True)
    return x * lax.rsqrt(ms + NORM_EPS) * g


def _sigmoid(x):
    return jax.nn.sigmoid(x)


def _norm_matmul_kernel(n_main, has_extra, *refs):
    if has_extra:
        x_ref, g_ref, w_ref, we_ref, o_ref, oe_ref, xn_ref = refs
    else:
        x_ref, g_ref, w_ref, o_ref, xn_ref = refs
    j = pl.program_id(1)

    @pl.when(j == 0)
    def _():
        xn_ref[...] = _rms(x_ref[...], g_ref[...]).astype(BF16)

    @pl.when(j < n_main)
    def _():
        o_ref[...] = _dot(xn_ref[...], w_ref[...].astype(BF16))

    if has_extra:
        @pl.when(j == n_main)
        def _():
            oe_ref[...] = _dot(xn_ref[...], we_ref[...].astype(BF16))


def _norm_matmul(h, g, w, layer, m_main, name, w_extra=None):
    n, d = h.shape
    tm = _tile(n, 1024)
    tn = _tile(m_main, 1024, V7X_LANES)
    n_main = m_main // tn
    has_extra = w_extra is not None
    last = n_main - 1
    in_specs = [pl.BlockSpec((tm, d), lambda i, j: (i, 0), pipeline_mode=pl.Buffered(1)),
                pl.BlockSpec((1, d), lambda i, j: (0, 0)),
                pl.BlockSpec((None, d, tn), lambda i, j: (layer, 0, jnp.minimum(j, last)))]
    out_specs = [pl.BlockSpec((tm, tn), lambda i, j: (i, jnp.minimum(j, last)))]
    out_shape = [jax.ShapeDtypeStruct((n, m_main), F32)]
    args = [h, g.reshape(1, d), w]
    vmem = (_nbytes((tm, d), F32) + _nbytes((tm, d), BF16) + 2 * _nbytes((d, tn), w.dtype)
            + _nbytes((d, tn), BF16) + 3 * _nbytes((tm, tn), F32) + _nbytes((tm, d), F32))
    if has_extra:
        me = w_extra.shape[1]
        in_specs.append(pl.BlockSpec((d, me), lambda i, j: (0, 0)))
        out_specs.append(pl.BlockSpec((tm, me), lambda i, j: (i, 0)))
        out_shape.append(jax.ShapeDtypeStruct((n, me), F32))
        args.append(w_extra)
        vmem += 2 * _nbytes((d, me), w_extra.dtype) + 3 * _nbytes((tm, me), F32)
    outs = pl.pallas_call(
        functools.partial(_norm_matmul_kernel, n_main, has_extra),
        grid=(n // tm, n_main + has_extra),
        in_specs=in_specs,
        out_specs=out_specs,
        out_shape=out_shape,
        scratch_shapes=[pltpu.VMEM((tm, d), BF16)],
        compiler_params=_params(("parallel", "arbitrary"), vmem),
        name=name,
    )(*args)
    return outs if has_extra else outs[0]


def _matmul_residual_kernel(n_terms, *refs):
    x_refs, w_refs = refs[:n_terms], refs[n_terms:2 * n_terms]
    r_ref, o_ref = refs[2 * n_terms:]
    acc = r_ref[...]
    for x_ref, w_ref in zip(x_refs, w_refs):
        acc = acc + _dot(x_ref[...], w_ref[...].astype(BF16))
    o_ref[...] = acc


def _matmul_residual(xs_bf16, w, layer, res, name):
    n, m = res.shape
    k = xs_bf16[0].shape[1]
    assert all(x.shape[1] == k for x in xs_bf16) and w.shape[1] == k * len(xs_bf16)
    tm = _tile(n, 1024)
    tn = _tile(m, 512, V7X_LANES)
    ksum = w.shape[1]
    vmem = (2 * _nbytes((tm, ksum), BF16) + 2 * _nbytes((ksum, tn), w.dtype) + _nbytes((ksum, tn), BF16)
            + (5 + len(xs_bf16)) * _nbytes((tm, tn), F32))
    x_specs = [pl.BlockSpec((tm, k), lambda i, j: (i, 0)) for _ in xs_bf16]
    w_specs = [pl.BlockSpec((None, k, tn), functools.partial(lambda i, j, r: (layer, r, j), r=r))
               for r in range(len(xs_bf16))]
    return pl.pallas_call(
        functools.partial(_matmul_residual_kernel, len(xs_bf16)),
        grid=(n // tm, m // tn),
        in_specs=x_specs + w_specs + [pl.BlockSpec((tm, tn), lambda i, j: (i, j))],
        out_specs=pl.BlockSpec((tm, tn), lambda i, j: (i, j)),
        out_shape=jax.ShapeDtypeStruct((n, m), F32),
        compiler_params=_params(("parallel", "arbitrary"), vmem),
        name=name,
    )(*xs_bf16, *([w] * len(xs_bf16)), res)


def _mlp_kernel(has_final, *refs):
    if has_final:
        x_ref, g_ref, w1_ref, w2_ref, gf_ref, o_ref, xn_ref = refs
    else:
        x_ref, g_ref, w1_ref, w2_ref, o_ref, xn_ref = refs
    j = pl.program_id(1)

    @pl.when(j == 0)
    def _():
        x = x_ref[...]
        xn_ref[...] = _rms(x, g_ref[...]).astype(BF16)
        o_ref[...] = x

    a = jnp.maximum(_dot(xn_ref[...], w1_ref[...]), 0.0)
    o_ref[...] += _dot((a * a).astype(BF16), w2_ref[...])

    if has_final:
        @pl.when(j == pl.num_programs(1) - 1)
        def _():
            o_ref[...] = _rms(o_ref[...], gf_ref[...])


def _mlp(h, g, w1, w2, layer, name, final_g=None):
    n, d = h.shape
    f = w1.shape[2]
    tm = _tile(n, 1024)
    tf = _tile(f, 512, V7X_LANES)
    has_final = final_g is not None
    vec = pl.BlockSpec((1, d), lambda i, j: (0, 0))
    in_specs = [pl.BlockSpec((tm, d), lambda i, j: (i, 0), pipeline_mode=pl.Buffered(1)), vec,
                pl.BlockSpec((None, d, tf), lambda i, j: (layer, 0, j)),
                pl.BlockSpec((None, tf, d), lambda i, j: (layer, j, 0))]
    args = [h, g.reshape(1, d), w1, w2]
    if has_final:
        in_specs.append(vec)
        args.append(final_g.reshape(1, d))
    vmem = (3 * _nbytes((tm, d), F32) + _nbytes((tm, d), BF16) + 4 * _nbytes((d, tf), w1.dtype)
            + 3 * _nbytes((tm, tf), F32) + _nbytes((tm, d), F32))
    return pl.pallas_call(
        functools.partial(_mlp_kernel, has_final),
        grid=(n // tm, f // tf),
        in_specs=in_specs,
        out_specs=pl.BlockSpec((tm, d), lambda i, j: (i, 0)),
        out_shape=jax.ShapeDtypeStruct((n, d), F32),
        scratch_shapes=[pltpu.VMEM((tm, d), BF16)],
        compiler_params=_params(("parallel", "arbitrary"), vmem),
        name=name,
    )(*args)


def _retention_tables(blk, dh):
    log_g = jnp.log1p(-jnp.exp2(-5.0 - jnp.arange(RET_HEADS, dtype=F32)))
    pos = jnp.arange(blk, dtype=F32)
    n, m = pos[:, None], pos[None, :]
    cn, cm = jnp.floor(n / STREAM_CHUNK), jnp.floor(m / STREAM_CHUNK)
    dist = jnp.where(cn == cm, jnp.abs(n - m), n - m)
    lg = log_g[:, None, None]
    dmask = jnp.where((cm <= cn)[None], jnp.exp(lg * dist[None]), 0.0)
    qdec = jnp.broadcast_to(jnp.exp(lg * (pos + 1.0)[None, :, None]), (RET_HEADS, blk, dh))
    kdec = jnp.broadcast_to(jnp.exp(lg * (blk - 1.0 - pos)[None, :, None]), (RET_HEADS, blk, dh))
    cdec = jnp.broadcast_to(jnp.exp(lg * blk), (RET_HEADS, 1, dh))
    return dmask, qdec, kdec, cdec


def _rope_tables(t, dh):
    inv = 1.0 / (ROPE_BASE ** (jnp.arange(0, dh, 2, dtype=F32) / dh))
    ang = jnp.arange(t, dtype=F32)[:, None] * inv[None, :]
    return jnp.cos(ang), jnp.sin(ang)


def _retention_kernel(q_ref, k_ref, v_ref, g_ref, cos_ref, sin_ref, dm_ref, qd_ref, kd_ref,
                      cd_ref, o_ref, s_ref):
    @pl.when(pl.program_id(2) == 0)
    def _():
        s_ref[...] = jnp.zeros_like(s_ref)

    dh = q_ref.shape[1]
    half = dh // 2
    cos, sin = cos_ref[...], sin_ref[...]

    def rot(t):
        t1, t2 = t[:, :half], t[:, half:]
        return jnp.concatenate([t1 * cos - t2 * sin, t2 * cos + t1 * sin], axis=-1)

    q = rot(q_ref[...])
    k = rot(k_ref[...]) * (dh ** -0.5)
    vb = v_ref[...].astype(BF16)
    scores = _dot_nt(q.astype(BF16), k.astype(BF16)) * dm_ref[0]
    state = s_ref[...]
    out = _dot(scores.astype(BF16), vb) + _dot((q * qd_ref[0]).astype(BF16), state.astype(BF16))
    s_ref[...] = state * cd_ref[0] + _dot_tn((k * kd_ref[0]).astype(BF16), vb)

    mu = jnp.mean(out, axis=-1, keepdims=True)
    cen = out - mu
    var = jnp.mean(cen * cen, axis=-1, keepdims=True)
    y = cen * lax.rsqrt(var + RET_GN_EPS)
    g = g_ref[...]
    o_ref[...] = (g * _sigmoid(g) * y).astype(BF16)


def _retention(z, b, t, ret_w, cos, sin):
    n = z.shape[0]
    dh = ret_w // RET_HEADS
    blk = _tile(t, 512, STREAM_CHUNK)
    nblk = t // blk
    dmask, qdec, kdec, cdec = _retention_tables(blk, dh)
    row = lambda bi, hi, ci: bi * nblk + ci
    col_spec = lambda off: pl.BlockSpec((blk, dh), lambda bi, hi, ci: (row(bi, hi, ci), off + hi))
    head_spec = lambda shape: pl.BlockSpec((1,) + shape, lambda bi, hi, ci: (hi, 0, 0))
    vmem = (2 * 4 * _nbytes((blk, dh), F32) + 4 * _nbytes((blk, dh // 2), F32)
            + 2 * _nbytes((blk, blk), F32) + 4 * _nbytes((blk, dh), F32) + _nbytes((dh, dh), F32)
            + 12 * _nbytes((blk, max(blk, dh)), F32))
    return pl.pallas_call(
        _retention_kernel,
        grid=(b, RET_HEADS, nblk),
        in_specs=[col_spec(0), col_spec(RET_HEADS), col_spec(2 * RET_HEADS), col_spec(3 * RET_HEADS),
                  pl.BlockSpec((blk, dh // 2), lambda bi, hi, ci: (ci, 0)),
                  pl.BlockSpec((blk, dh // 2), lambda bi, hi, ci: (ci, 0)),
                  head_spec((blk, blk)), head_spec((blk, dh)), head_spec((blk, dh)),
                  head_spec((1, dh))],
        out_specs=pl.BlockSpec((blk, dh), lambda bi, hi, ci: (row(bi, hi, ci), hi)),
        out_shape=jax.ShapeDtypeStruct((n, ret_w), BF16),
        scratch_shapes=[pltpu.VMEM((dh, dh), F32)],
        compiler_params=_params(("parallel", "parallel", "arbitrary"), vmem),
        name="retention",
    )(z, z, z, z, cos, sin, dmask, qdec, kdec, cdec)


def _rwkv_prep_kernel(has_vres, t_len, lora_ranks, *refs):
    if has_vres:
        (zr_ref, zk_ref, zv_ref, zl_ref, mur_ref, muk_ref, muv_ref, mul_ref, wl_ref, w0_ref, a0_ref,
         kk_ref, ka_ref, ones_ref, tri_ref, vf_ref, v0_ref, v1_ref, v2_ref,
         r_out, lw_out, c_out, k_out, v_out, p_out, a_out, g_out, cr, ck, cv, cl) = refs
    else:
        (zr_ref, zk_ref, zv_ref, zl_ref, mur_ref, muk_ref, muv_ref, mul_ref, wl_ref, w0_ref, a0_ref,
         kk_ref, ka_ref, ones_ref, tri_ref,
         r_out, lw_out, c_out, k_out, v_out, p_out, a_out, g_out, cr, ck, cv, cl) = refs
    tm = zr_ref.shape[0]
    rw = zr_ref.shape[1]
    at_start = (pl.program_id(0) * tm) % t_len == 0
    row0 = lax.broadcasted_iota(jnp.int32, (tm, 1), 0) == 0

    def shift_mix(x_ref, carry_ref, mu_ref):
        x = x_ref[...]
        last = jnp.where(at_start, 0.0, carry_ref[...])
        prev = jnp.where(row0, last, pltpu.roll(x, 1, 0))
        carry_ref[...] = x[tm - 1:tm, :]
        return x + mu_ref[...] * (prev - x)

    r = shift_mix(zr_ref, cr, mur_ref)
    kr = shift_mix(zk_ref, ck, muk_ref)
    vr = shift_mix(zv_ref, cv, muv_ref)
    lo = shift_mix(zl_ref, cl, mul_ref)

    rank_w, rank_a = lora_ranks
    lane = lax.broadcasted_iota(jnp.int32, (1, lo.shape[1]), 1)
    feat = jnp.where(lane < rank_w, jnp.tanh(lo), jnp.where(lane < rank_w + rank_a, lo, _sigmoid(lo)))
    proj = _dot(feat.astype(BF16), wl_ref[...])
    log_w = -math.exp(-0.5) * _sigmoid(w0_ref[...] + proj[:, :rw])
    iclr = _sigmoid(a0_ref[...] + proj[:, rw:2 * rw])
    gate = proj[:, 2 * rw:]

    if has_vres:
        low = _dot(vr.astype(BF16), v1_ref[...])
        mix = _sigmoid(v0_ref[...] + _dot(low.astype(BF16), v2_ref[...]))
        vr = vr + (vf_ref[...] - vr) * mix

    kk = kr * kk_ref[...]
    norm = jnp.sqrt(_seg_sum(kk * kk, ones_ref[...]))
    kk = kk / jnp.maximum(norm, 1e-12)
    k2 = kr * (1.0 + (iclr - 1.0) * ka_ref[...])

    l_hi, l_mid, l_lo = _split3(log_w)
    tri = tri_ref[...]
    c_out[...] = _dot(tri, l_hi) + _dot(tri, l_mid) + _dot(tri, l_lo)

    r_out[...] = r
    lw_out[...] = log_w
    k_out[...] = k2
    v_out[...] = vr
    p_out[...] = kk
    a_out[...] = iclr
    g_out[...] = gate


def _head_ones(width, head):
    idx = jnp.arange(width) // head
    return (idx[:, None] == idx[None, :]).astype(BF16)


def _rwkv_prep(z, z_lora, t, ret_in, rw, mu, w2, a2, g2, w0, a0, k_k, k_a, v_first, v_res):
    n = z.shape[0]
    has_vres = v_res is not None
    rank_w, rank_a, rank_g = w2.shape[0], a2.shape[0], g2.shape[0]
    lora = rank_w + rank_a + rank_g
    assert lora <= LORA_PAD == z_lora.shape[1] and ret_in % rw == 0
    tm = _tile(t, 256, RWKV_CHUNK)
    pos = jnp.arange(tm)
    tri = ((pos[:, None] // RWKV_CHUNK == pos[None, :] // RWKV_CHUNK)
           & (pos[:, None] >= pos[None, :])).astype(BF16)
    row2 = lambda v: v.reshape(1, -1)
    wl = jnp.zeros((LORA_PAD, 3 * rw), F32)
    wl = wl.at[:rank_w, :rw].set(w2).at[rank_w:rank_w + rank_a, rw:2 * rw].set(a2)
    wl = wl.at[rank_w + rank_a:lora, 2 * rw:].set(g2).astype(BF16)
    mu_l = jnp.zeros((1, LORA_PAD), F32).at[0, :lora].set(mu[3 * rw:])
    c0 = ret_in // rw
    zspec = lambda ci: pl.BlockSpec((tm, rw), lambda i: (i, ci))
    vec = pl.BlockSpec((1, rw), lambda i: (0, 0))
    full = lambda a: pl.BlockSpec(a.shape, lambda i: (0,) * a.ndim)
    ones_bd = _head_ones(rw, RWKV_HEAD_DIM)
    args = [z, z, z, z_lora, row2(mu[:rw]), row2(mu[rw:2 * rw]), row2(mu[2 * rw:3 * rw]), mu_l, wl,
            row2(w0), row2(a0), row2(k_k), row2(k_a), ones_bd, tri]
    specs = [zspec(c0), zspec(c0 + 1), zspec(c0 + 2),
             pl.BlockSpec((tm, LORA_PAD), lambda i: (i, 0)),
             vec, vec, vec, full(mu_l), full(wl), vec, vec, vec, vec, full(ones_bd), full(tri)]
    if has_vres:
        v0, v1, v2 = v_res
        rank_v = v1.shape[1]
        rank_pad = -(-rank_v // V7X_LANES) * V7X_LANES
        v1p = jnp.zeros((rw, rank_pad), F32).at[:, :rank_v].set(v1).astype(BF16)
        v2p = jnp.zeros((rank_pad, rw), F32).at[:rank_v, :].set(v2).astype(BF16)
        args += [v_first, row2(v0), v1p, v2p]
        specs += [pl.BlockSpec((tm, rw), lambda i: (i, 0)), vec, full(v1p), full(v2p)]
    out_spec = pl.BlockSpec((tm, rw), lambda i: (i, 0))
    vmem = (2 * (3 + has_vres) * _nbytes((tm, rw), F32) + 2 * _nbytes((tm, LORA_PAD), F32)
            + 2 * _nbytes(wl.shape, BF16) + 2 * _nbytes((rw, rw), BF16)
            + 16 * _nbytes((tm, rw), F32) + 12 * _nbytes((tm, rw), F32))
    return pl.pallas_call(
        functools.partial(_rwkv_prep_kernel, has_vres, t, (rank_w, rank_a)),
        grid=(n // tm,),
        in_specs=specs,
        out_specs=[out_spec] * 8,
        out_shape=[jax.ShapeDtypeStruct((n, rw), F32)] * 8,
        scratch_shapes=[pltpu.VMEM((1, rw), F32)] * 3 + [pltpu.VMEM((1, LORA_PAD), F32)],
        compiler_params=_params(("arbitrary",), vmem),
        name="rwkv_prep",
    )(*args)


def _rwkv_core_kernel(r_ref, lw_ref, c_ref, k_ref, v_ref, p_ref, a_ref, g_ref, rk_ref, lnw_ref,
                      lnb_ref, o_ref, h_ref, y_ref):
    @pl.when(pl.program_id(2) == 0)
    def _():
        h_ref[...] = jnp.zeros_like(h_ref)

    L = RWKV_CHUNK
    W = RWKV_PAIR
    hd = RWKV_HEAD_DIM
    n_chunks = r_ref.shape[0] // L
    n_pairs = r_ref.shape[1] // W

    ri = lax.broadcasted_iota(jnp.int32, (W, W), 0)
    ci = lax.broadcasted_iota(jnp.int32, (W, W), 1)
    same_head = (ri // hd) == (ci // hd)
    strict = same_head & ((ri % hd) > (ci % hd))
    incl = same_head & ((ri % hd) >= (ci % hd))
    eye = (ri == ci).astype(F32)
    merge_masks = []
    s = 1
    while s < hd:
        merge_masks.append(((ri // (2 * s)) == (ci // (2 * s))) & ((ri // s) != (ci // s)))
        s *= 2
    ones_bd = same_head.astype(BF16)
    bd_mask = (lax.broadcasted_iota(jnp.int32, (2 * L, W), 0) // L) == (
        lax.broadcasted_iota(jnp.int32, (2 * L, W), 1) // hd)

    def bd(x):
        return jnp.where(bd_mask, jnp.concatenate([x, x], axis=0), 0.0)

    def fold(x):
        return x[:L, :] + x[L:, :]

    def pair_chunk(sl, g):
        ln = slice(g * W, (g + 1) * W)
        r, lw, k, v, p = r_ref[sl, ln], lw_ref[sl, ln], k_ref[sl, ln], v_ref[sl, ln], p_ref[sl, ln]
        q = p * a_ref[sl, ln]
        c = c_ref[sl, ln]
        c_last = c[L - 1:L, :]
        e_in = jnp.exp(c)
        e_out = jnp.exp(-c)
        e_end = jnp.exp(c_last - c)
        rt, kt, qt = r * e_in, k * e_out, q * e_out
        pt = p * jnp.exp(c - lw)
        kh, qh = k * e_end, q * e_end

        pt_bd, rt_bd = bd(pt), bd(rt)
        lhs = jnp.concatenate([pt_bd, rt_bd], axis=0).astype(BF16)
        rhs = jnp.concatenate([bd(qt), bd(kt)], axis=0).astype(BF16)
        m = _dot_nt(lhs, rhs)
        yield
        a_pq = jnp.where(strict, m[:W, :W], 0.0)
        a_pk = jnp.where(strict, m[:W, W:], 0.0).astype(BF16)
        a_rq = jnp.where(incl, m[W:, :W], 0.0).astype(BF16)
        a_rk = jnp.where(incl, m[W:, W:], 0.0).astype(BF16)
        v_bd = bd(v).astype(BF16)
        pk_v = _dot(a_pk, v_bd)
        rk_v = _dot(a_rk, v_bd)
        yield

        t_inv = eye - jnp.where(merge_masks[0], a_pq, 0.0)
        for mask in merge_masks[1:]:
            tb = t_inv.astype(BF16)
            inner = _dot(jnp.where(mask, a_pq, 0.0).astype(BF16), tb)
            yield
            t_inv = t_inv - _dot(tb, inner.astype(BF16))
            yield
        t_inv = t_inv.astype(BF16)

        sol = _dot(t_inv, jnp.concatenate([pt_bd, pk_v], axis=1).astype(BF16)).astype(BF16)
        yield
        rq_sol = _dot(a_rq, sol)
        r_hat = fold(rt_bd - rq_sol[:, :W])
        y0 = fold(rk_v - rq_sol[:, W:])
        qh_sol = _dot_tn(bd(qh).astype(BF16), sol)
        g_mat = eye * jnp.exp(c_last) - qh_sol[:, :W]
        h_add = _dot_tn(bd(kh).astype(BF16), v_bd) - qh_sol[:, W:]
        yield

        h_hi, h_lo = _split2(h_ref[g])
        y_ref[sl, ln] = _dot(r_hat.astype(BF16), h_hi) + y0
        g_hi, g_lo = _split2(g_mat)
        gh = _dot(g_hi, jnp.concatenate([h_hi, h_lo], axis=1))
        h_ref[g] = gh[:, :W] + gh[:, W:] + _dot(g_lo, h_hi) + h_add

    def chunk(i, carry):
        sl = pl.ds(pl.multiple_of(i * L, L), L)
        live = [pair_chunk(sl, g) for g in range(n_pairs)]
        while live:
            live = [gen for gen in live if next(gen, True) is None]
        return carry

    lax.fori_loop(0, n_chunks, chunk, 0)

    inv_hd = 1.0 / hd
    for g in range(n_pairs):
        ln = slice(g * W, (g + 1) * W)
        y = y_ref[:, ln]
        mu = _seg_sum(y, ones_bd) * inv_hd
        cen = y - mu
        var = _seg_sum(cen * cen, ones_bd) * inv_hd
        yn = cen * lax.rsqrt(var + RWKV_LN_EPS) * lnw_ref[:, ln] + lnb_ref[:, ln]
        bonus = _seg_sum(r_ref[:, ln] * k_ref[:, ln] * rk_ref[:, ln], ones_bd) * v_ref[:, ln]
        o_ref[:, ln] = ((yn + bonus) * g_ref[:, ln]).astype(BF16)


def _rwkv_core(r, lw, c, k, v, p, a, gate, r_k, lnx_w, lnx_b, b, t):
    n, rw = r.shape
    assert rw % RWKV_PAIR == 0 and t % RWKV_CHUNK == 0
    group = _tile(rw, RWKV_GROUP_LANES, RWKV_PAIR)
    blk = _tile(t, 256, RWKV_CHUNK)
    nblk = t // blk
    tok = pl.BlockSpec((blk, group), lambda bi, pi, ci: (bi * nblk + ci, pi))
    vec = pl.BlockSpec((1, group), lambda bi, pi, ci: (0, pi))
    row2 = lambda x: x.reshape(1, rw)
    vmem = (2 * 8 * _nbytes((blk, group), F32) + 2 * _nbytes((blk, group), BF16)
            + _nbytes((blk, group), F32)
            + 64 * (group // RWKV_PAIR) * _nbytes((2 * RWKV_PAIR, 2 * RWKV_PAIR), F32)
            + 8 * _nbytes((blk, group), F32))
    return pl.pallas_call(
        _rwkv_core_kernel,
        grid=(b, rw // group, nblk),
        in_specs=[tok] * 8 + [vec] * 3,
        out_specs=tok,
        out_shape=jax.ShapeDtypeStruct((n, rw), BF16),
        scratch_shapes=[pltpu.VMEM((group // RWKV_PAIR, RWKV_PAIR, RWKV_PAIR), F32),
                        pltpu.VMEM((blk, group), F32)],
        compiler_params=_params(("parallel", "parallel", "arbitrary"), vmem),
        name="rwkv_core",
    )(r, lw, c, k, v, p, a, gate, row2(r_k), row2(lnx_w), row2(lnx_b))


def _lru_kernel(y_ref, x_ref, cw_ref, cb_ref, gxw_ref, gxb_ref, gaw_ref, gab_ref, lam_ref,
                o_ref, xtail_ref, h_ref):
    @pl.when(pl.program_id(2) == 0)
    def _():
        xtail_ref[...] = jnp.zeros_like(xtail_ref)
        h_ref[...] = jnp.zeros_like(h_ref)

    lt = x_ref.shape[0]
    tail = xtail_ref.shape[0]
    x = x_ref[...]
    ext = jnp.concatenate([xtail_ref[...], x], axis=0)
    xtail_ref[...] = x[lt - tail:, :]
    cw = cw_ref[...]
    xc = cb_ref[...] + cw[CONV_WIDTH - 1:CONV_WIDTH, :] * x
    for j in range(1, CONV_WIDTH):
        xc = xc + cw[CONV_WIDTH - 1 - j:CONV_WIDTH - j, :] * pltpu.roll(ext, j, 0)[tail:, :]

    xcb = xc.astype(BF16)
    gate_x = _sigmoid(_dot(xcb, gxw_ref[0]) + gxb_ref[...])
    gate_a = _sigmoid(_dot(xcb, gaw_ref[0]) + gab_ref[...])
    neg_lam = -lam_ref[...]
    softplus = jnp.maximum(neg_lam, 0.0) + jnp.log(1.0 + jnp.exp(-jnp.abs(neg_lam)))
    log_a = -LRU_C * gate_a * softplus
    a = jnp.exp(log_a)
    bb = xc * gate_x * jnp.sqrt(-jnp.tanh(log_a) * (a * a + 1.0))

    sub = V7X_SUBLANES
    n_groups = lt // sub
    a = a.reshape(n_groups, sub, a.shape[1])
    bb = bb.reshape(n_groups, sub, bb.shape[1])
    row_in_group = lax.broadcasted_iota(jnp.int32, (1, sub, 1), 1)
    s = 1
    while s < sub:
        valid = row_in_group >= s
        b_prev = jnp.where(valid, pltpu.roll(bb, s, 1), 0.0)
        a_prev = jnp.where(valid, pltpu.roll(a, s, 1), 1.0)
        bb = bb + a * b_prev
        a = a * a_prev
        s *= 2
    h_prev = h_ref[...]
    groups = []
    for gi in range(n_groups):
        hg = a[gi] * h_prev + bb[gi]
        groups.append(hg)
        h_prev = hg[sub - 1:sub, :]
    hs = jnp.concatenate(groups, axis=0)
    h_ref[...] = h_prev

    y = y_ref[...]
    gelu = 0.5 * y * (1.0 + jnp.tanh(math.sqrt(2.0 / math.pi) * (y + 0.044715 * (y * y * y))))
    o_ref[...] = (gelu * hs).astype(BF16)


def _lru(z, b, t, d, conv_w, conv_b, gx_w, gx_b, ga_w, ga_b, lam):
    n = z.shape[0]
    assert d % LRU_BLOCK == 0
    nb = d // LRU_BLOCK
    lt = _tile(t, 256)
    nblk = t // lt
    tail = V7X_SUBLANES
    assert CONV_WIDTH - 1 <= tail <= lt
    tok = lambda off: pl.BlockSpec((lt, LRU_BLOCK), lambda bi, ni, ci: (bi * nblk + ci, off + ni))
    vec = lambda rows: pl.BlockSpec((rows, LRU_BLOCK), lambda bi, ni, ci: (0, ni))
    wsp = pl.BlockSpec((1, LRU_BLOCK, LRU_BLOCK), lambda bi, ni, ci: (ni, 0, 0))
    row2 = lambda x: x.reshape(1, d)
    vmem = (6 * _nbytes((lt, LRU_BLOCK), F32) + 8 * _nbytes((LRU_BLOCK, LRU_BLOCK), BF16)
            + 24 * _nbytes((lt, LRU_BLOCK), F32))
    return pl.pallas_call(
        _lru_kernel,
        grid=(b, nb, nblk),
        in_specs=[tok(0), tok(nb), vec(CONV_WIDTH), vec(1), wsp, vec(1), wsp, vec(1), vec(1)],
        out_specs=tok(0),
        out_shape=jax.ShapeDtypeStruct((n, d), BF16),
        scratch_shapes=[pltpu.VMEM((tail, LRU_BLOCK), F32), pltpu.VMEM((1, LRU_BLOCK), F32)],
        compiler_params=_params(("parallel", "parallel", "arbitrary"), vmem),
        name="rg_lru",
    )(z, z, conv_w, row2(conv_b), gx_w.astype(BF16), row2(gx_b), ga_w.astype(BF16), row2(ga_b),
      row2(lam))


def _even_layer(h, b, t, e, norm_g, w_in_all, w_out_all, mu, w0, w2, a0, a2, g2, k_k, k_a, r_k, lnx_w,
                lnx_b, v_first, v_res, rope):
    d = h.shape[1]
    ret_w = d // 2
    rw = d - ret_w
    ret_in = 4 * ret_w
    m_main = ret_in + 3 * rw
    lora = w_in_all.shape[2] - m_main
    w_lora = jnp.pad(w_in_all[e, :, m_main:], ((0, 0), (0, LORA_PAD - lora)))
    z, z_lora = _norm_matmul(h, norm_g, w_in_all, e, m_main, "even_in_proj", w_extra=w_lora)
    out_ret = _retention(z, b, t, ret_w, *rope)
    r, lw, c, k, v, p, a, gate = _rwkv_prep(z, z_lora, t, ret_in, rw, mu, w2, a2, g2, w0, a0, k_k, k_a,
                                            v_first, v_res)
    out_rw = _rwkv_core(r, lw, c, k, v, p, a, gate, r_k.reshape(-1), lnx_w, lnx_b, b, t)
    h = _matmul_residual([out_ret, out_rw], w_out_all, e, h, "even_out_proj")
    return h, (v if v_res is None else v_first)


def _odd_layer(h, b, t, o, norm_g, w_in_all, conv_w, conv_b, gx_w, gx_b, ga_w, ga_b, lam, w_out_all):
    d = h.shape[1]
    z = _norm_matmul(h, norm_g, w_in_all, o, w_in_all.shape[2], "odd_in_proj")
    gated = _lru(z, b, t, d, conv_w, conv_b, gx_w, gx_b, ga_w, ga_b, lam)
    return _matmul_residual([gated], w_out_all, o, h, "odd_out_proj")


def kernel(x, ev_norm, ev_w_in, ev_w_out, rw_mu, rw_w0, rw_w2, rw_a0, rw_a2, rw_g2, rw_k_k, rw_k_a, rw_r_k, rw_lnx_w, rw_lnx_b, rw_v0, rw_v1, rw_v2, od_norm, od_w_in, od_conv_w, od_conv_b, od_gx_w, od_gx_b, od_ga_w, od_ga_b, od_lam, od_w_out, ff_norm, ff_w1, ff_w2, final_norm):
    b, t, d = x.shape
    depth = ff_norm.shape[0]
    h = x.reshape(b * t, d)
    rope = _rope_tables(t, (d // 2) // RET_HEADS)
    ff_w1_b, ff_w2_b = ff_w1.astype(BF16), ff_w2.astype(BF16)
    v_first = None
    for layer in range(depth):
        if layer % 2 == 0:
            e = layer // 2
            v_res = None if e == 0 else (rw_v0[e - 1], rw_v1[e - 1], rw_v2[e - 1])
            h, v_first = _even_layer(
                h, b, t, e, ev_norm[e], ev_w_in, ev_w_out, rw_mu[e], rw_w0[e], rw_w2[e],
                rw_a0[e], rw_a2[e], rw_g2[e], rw_k_k[e], rw_k_a[e], rw_r_k[e], rw_lnx_w[e],
                rw_lnx_b[e], v_first, v_res, rope)
        else:
            o = layer // 2
            h = _odd_layer(h, b, t, o, od_norm[o], od_w_in, od_conv_w[o], od_conv_b[o], od_gx_w[o],
                           od_gx_b[o], od_ga_w[o], od_ga_b[o], od_lam[o], od_w_out)
        h = _mlp(h, ff_norm[layer], ff_w1_b, ff_w2_b, layer, f"mlp_{layer}",
                 final_g=final_norm if layer == depth - 1 else None)
    return h.reshape(b, t, d)
```

```python
import functools
import math

import jax
import jax.numpy as jnp
from jax import lax
from jax.experimental import pallas as pl
from jax.experimental.pallas import tpu as pltpu

F32 = jnp.float32
BF16 = jnp.bfloat16

NORM_EPS = 1e-6
RET_HEADS = 4
RET_GN_EPS = 1e-5
ROPE_BASE = 10000.0
STREAM_CHUNK = 64
RWKV_HEAD_DIM = 64
RWKV_LN_EPS = 64e-5
LRU_BLOCK = 256
CONV_WIDTH = 4
LRU_C = 8.0

V7X_LANES = 128
V7X_SUBLANES = 8
V7X_VMEM_BYTES = 64 * 1024 * 1024
VMEM_CAP_BYTES = V7X_VMEM_BYTES - 8 * 1024 * 1024
VMEM_FLOOR_BYTES = 16 * 1024 * 1024

RWKV_CHUNK = 64
RWKV_PAIR = 2 * RWKV_HEAD_DIM
RWKV_GROUP_LANES = 8 * RWKV_PAIR
LORA_PAD = 512


def _tile(n, pref, mult=V7X_SUBLANES):
    if n <= pref:
        return n
    t = (pref // mult) * mult
    while t >= mult:
        if n % t == 0:
            return t
        t -= mult
    raise ValueError(f"no tile for {n} <= {pref}")


def _params(semantics, vmem_bytes):
    limit = int(min(max(vmem_bytes, VMEM_FLOOR_BYTES), VMEM_CAP_BYTES))
    return pltpu.CompilerParams(dimension_semantics=semantics, vmem_limit_bytes=limit)


def _nbytes(shape, dtype):
    return math.prod(shape) * jnp.dtype(dtype).itemsize


def _dot(a, b):
    return jnp.dot(a, b, preferred_element_type=F32)


def _dot_nt(a, b):
    return lax.dot_general(a, b, (((1,), (1,)), ((), ())), preferred_element_type=F32)


def _dot_tn(a, b):
    return lax.dot_general(a, b, (((0,), (0,)), ((), ())), preferred_element_type=F32)


def _split2(x):
    hi = x.astype(BF16)
    lo = (x - hi.astype(F32)).astype(BF16)
    return hi, lo


def _split3(x):
    hi = x.astype(BF16)
    r1 = x - hi.astype(F32)
    mid = r1.astype(BF16)
    lo = (r1 - mid.astype(F32)).astype(BF16)
    return hi, mid, lo


def _seg_sum(x, ones_bd):
    hi, lo = _split2(x)
    return _dot(hi, ones_bd) + _dot(lo, ones_bd)


def _rms(x, g):
    ms = jnp.mean(x * x, axis=-1, keepdims=True)
    return x * lax.rsqrt(ms + NORM_EPS) * g


def _sigmoid(x):
    return jax.nn.sigmoid(x)


def _norm_matmul_kernel(n_main, has_extra, *refs):
    if has_extra:
        x_ref, g_ref, w_ref, we_ref, o_ref, oe_ref, xn_ref = refs
    else:
        x_ref, g_ref, w_ref, o_ref, xn_ref = refs
    j = pl.program_id(1)

    @pl.when(j == 0)
    def _():
        xn_ref[...] = _rms(x_ref[...], g_ref[...]).astype(BF16)

    @pl.when(j < n_main)
    def _():
        o_ref[...] = _dot(xn_ref[...], w_ref[...].astype(BF16))

    if has_extra:
        @pl.when(j == n_main)
        def _():
            oe_ref[...] = _dot(xn_ref[...], we_ref[...].astype(BF16))


def _norm_matmul(h, g, w, layer, m_main, name, w_extra=None):
    n, d = h.shape
    tm = _tile(n, 1024)
    tn = _tile(m_main, 1024, V7X_LANES)
    n_main = m_main // tn
    has_extra = w_extra is not None
    last = n_main - 1
    in_specs = [pl.BlockSpec((tm, d), lambda i, j: (i, 0), pipeline_mode=pl.Buffered(1)),
                pl.BlockSpec((1, d), lambda i, j: (0, 0)),
                pl.BlockSpec((None, d, tn), lambda i, j: (layer, 0, jnp.minimum(j, last)))]
    out_specs = [pl.BlockSpec((tm, tn), lambda i, j: (i, jnp.minimum(j, last)))]
    out_shape = [jax.ShapeDtypeStruct((n, m_main), F32)]
    args = [h, g.reshape(1, d), w]
    vmem = (_nbytes((tm, d), F32) + _nbytes((tm, d), BF16) + 2 * _nbytes((d, tn), w.dtype)
            + _nbytes((d, tn), BF16) + 3 * _nbytes((tm, tn), F32) + _nbytes((tm, d), F32))
    if has_extra:
        me = w_extra.shape[1]
        in_specs.append(pl.BlockSpec((d, me), lambda i, j: (0, 0)))
        out_specs.append(pl.BlockSpec((tm, me), lambda i, j: (i, 0)))
        out_shape.append(jax.ShapeDtypeStruct((n, me), F32))
        args.append(w_extra)
        vmem += 2 * _nbytes((d, me), w_extra.dtype) + 3 * _nbytes((tm, me), F32)
    outs = pl.pallas_call(
        functools.partial(_norm_matmul_kernel, n_main, has_extra),
        grid=(n // tm, n_main + has_extra),
        in_specs=in_specs,
        out_specs=out_specs,
        out_shape=out_shape,
        scratch_shapes=[pltpu.VMEM((tm, d), BF16)],
        compiler_params=_params(("parallel", "arbitrary"), vmem),
        name=name,
    )(*args)
    return outs if has_extra else outs[0]


def _matmul_residual_kernel(n_terms, *refs):
    x_refs, w_refs = refs[:n_terms], refs[n_terms:2 * n_terms]
    r_ref, o_ref = refs[2 * n_terms:]
    acc = r_ref[...]
    for x_ref, w_ref in zip(x_refs, w_refs):
        acc = acc + _dot(x_ref[...], w_ref[...])
    o_ref[...] = acc


def _matmul_residual(xs_bf16, w_bf16, layer, res, name):
    n, m = res.shape
    k = xs_bf16[0].shape[1]
    assert all(x.shape[1] == k for x in xs_bf16) and w_bf16.shape[1] == k * len(xs_bf16)
    tm = _tile(n, 512)
    ksum = w_bf16.shape[1]
    vmem = (2 * _nbytes((tm, ksum), BF16) + _nbytes((ksum, m), BF16)
            + (5 + len(xs_bf16)) * _nbytes((tm, m), F32))
    x_specs = [pl.BlockSpec((tm, k), lambda i: (i, 0)) for _ in xs_bf16]
    w_specs = [pl.BlockSpec((None, k, m), functools.partial(lambda i, r: (layer, r, 0), r=r),
                            pipeline_mode=pl.Buffered(1))
               for r in range(len(xs_bf16))]
    return pl.pallas_call(
        functools.partial(_matmul_residual_kernel, len(xs_bf16)),
        grid=(n // tm,),
        in_specs=x_specs + w_specs + [pl.BlockSpec((tm, m), lambda i: (i, 0))],
        out_specs=pl.BlockSpec((tm, m), lambda i: (i, 0)),
        out_shape=jax.ShapeDtypeStruct((n, m), F32),
        compiler_params=_params(("parallel",), vmem),
        name=name,
    )(*xs_bf16, *([w_bf16] * len(xs_bf16)), res)


def _mlp_kernel(has_final, *refs):
    if has_final:
        x_ref, g_ref, w1_ref, w2_ref, gf_ref, o_ref, xn_ref = refs
    else:
        x_ref, g_ref, w1_ref, w2_ref, o_ref, xn_ref = refs
    j = pl.program_id(1)

    @pl.when(j == 0)
    def _():
        x = x_ref[...]
        xn_ref[...] = _rms(x, g_ref[...]).astype(BF16)
        o_ref[...] = x

    a = jnp.maximum(_dot(xn_ref[...], w1_ref[...]), 0.0)
    o_ref[...] += _dot((a * a).astype(BF16), w2_ref[...])

    if has_final:
        @pl.when(j == pl.num_programs(1) - 1)
        def _():
            o_ref[...] = _rms(o_ref[...], gf_ref[...])


def _mlp(h, g, w1, w2, layer, name, final_g=None):
    n, d = h.shape
    f = w1.shape[2]
    tm = _tile(n, 1024)
    tf = _tile(f, 512, V7X_LANES)
    has_final = final_g is not None
    vec = pl.BlockSpec((1, d), lambda i, j: (0, 0))
    in_specs = [pl.BlockSpec((tm, d), lambda i, j: (i, 0), pipeline_mode=pl.Buffered(1)), vec,
                pl.BlockSpec((None, d, tf), lambda i, j: (layer, 0, j)),
                pl.BlockSpec((None, tf, d), lambda i, j: (layer, j, 0))]
    args = [h, g.reshape(1, d), w1, w2]
    if has_final:
        in_specs.append(vec)
        args.append(final_g.reshape(1, d))
    vmem = (3 * _nbytes((tm, d), F32) + _nbytes((tm, d), BF16) + 4 * _nbytes((d, tf), w1.dtype)
            + 3 * _nbytes((tm, tf), F32) + _nbytes((tm, d), F32))
    return pl.pallas_call(
        functools.partial(_mlp_kernel, has_final),
        grid=(n // tm, f // tf),
        in_specs=in_specs,
        out_specs=pl.BlockSpec((tm, d), lambda i, j: (i, 0)),
        out_shape=jax.ShapeDtypeStruct((n, d), F32),
        scratch_shapes=[pltpu.VMEM((tm, d), BF16)],
        compiler_params=_params(("parallel", "arbitrary"), vmem),
        name=name,
    )(*args)


def _retention_tables(blk, dh):
    log_g = jnp.log1p(-jnp.exp2(-5.0 - jnp.arange(RET_HEADS, dtype=F32)))
    pos = jnp.arange(blk, dtype=F32)
    n, m = pos[:, None], pos[None, :]
    cn, cm = jnp.floor(n / STREAM_CHUNK), jnp.floor(m / STREAM_CHUNK)
    dist = jnp.where(cn == cm, jnp.abs(n - m), n - m)
    lg = log_g[:, None, None]
    dmask = jnp.where((cm <= cn)[None], jnp.exp(lg * dist[None]), 0.0)
    qdec = jnp.broadcast_to(jnp.exp(lg * (pos + 1.0)[None, :, None]), (RET_HEADS, blk, dh))
    kdec = jnp.broadcast_to(jnp.exp(lg * (blk - 1.0 - pos)[None, :, None]), (RET_HEADS, blk, dh))
    cdec = jnp.broadcast_to(jnp.exp(lg * blk), (RET_HEADS, 1, dh))
    return dmask, qdec, kdec, cdec


def _rope_tables(t, dh):
    inv = 1.0 / (ROPE_BASE ** (jnp.arange(0, dh, 2, dtype=F32) / dh))
    ang = jnp.arange(t, dtype=F32)[:, None] * inv[None, :]
    return jnp.cos(ang), jnp.sin(ang)


def _retention_kernel(q_ref, k_ref, v_ref, g_ref, cos_ref, sin_ref, dm_ref, qd_ref, kd_ref,
                      cd_ref, o_ref, s_ref):
    @pl.when(pl.program_id(2) == 0)
    def _():
        s_ref[...] = jnp.zeros_like(s_ref)

    dh = q_ref.shape[1]
    half = dh // 2
    cos, sin = cos_ref[...], sin_ref[...]

    def rot(t):
        t1, t2 = t[:, :half], t[:, half:]
        return jnp.concatenate([t1 * cos - t2 * sin, t2 * cos + t1 * sin], axis=-1)

    q = rot(q_ref[...])
    k = rot(k_ref[...]) * (dh ** -0.5)
    vb = v_ref[...].astype(BF16)
    scores = _dot_nt(q.astype(BF16), k.astype(BF16)) * dm_ref[0]
    state = s_ref[...]
    out = _dot(scores.astype(BF16), vb) + _dot((q * qd_ref[0]).astype(BF16), state.astype(BF16))
    s_ref[...] = state * cd_ref[0] + _dot_tn((k * kd_ref[0]).astype(BF16), vb)

    mu = jnp.mean(out, axis=-1, keepdims=True)
    cen = out - mu
    var = jnp.mean(cen * cen, axis=-1, keepdims=True)
    y = cen * lax.rsqrt(var + RET_GN_EPS)
    g = g_ref[...]
    o_ref[...] = (g * _sigmoid(g) * y).astype(BF16)


def _retention(z, b, t, ret_w, cos, sin):
    n = z.shape[0]
    dh = ret_w // RET_HEADS
    blk = _tile(t, 512, STREAM_CHUNK)
    nblk = t // blk
    dmask, qdec, kdec, cdec = _retention_tables(blk, dh)
    row = lambda bi, hi, ci: bi * nblk + ci
    col_spec = lambda off: pl.BlockSpec((blk, dh), lambda bi, hi, ci: (row(bi, hi, ci), off + hi))
    head_spec = lambda shape: pl.BlockSpec((1,) + shape, lambda bi, hi, ci: (hi, 0, 0))
    vmem = (2 * 4 * _nbytes((blk, dh), F32) + 4 * _nbytes((blk, dh // 2), F32)
            + 2 * _nbytes((blk, blk), F32) + 4 * _nbytes((blk, dh), F32) + _nbytes((dh, dh), F32)
            + 12 * _nbytes((blk, max(blk, dh)), F32))
    return pl.pallas_call(
        _retention_kernel,
        grid=(b, RET_HEADS, nblk),
        in_specs=[col_spec(0), col_spec(RET_HEADS), col_spec(2 * RET_HEADS), col_spec(3 * RET_HEADS),
                  pl.BlockSpec((blk, dh // 2), lambda bi, hi, ci: (ci, 0)),
                  pl.BlockSpec((blk, dh // 2), lambda bi, hi, ci: (ci, 0)),
                  head_spec((blk, blk)), head_spec((blk, dh)), head_spec((blk, dh)),
                  head_spec((1, dh))],
        out_specs=pl.BlockSpec((blk, dh), lambda bi, hi, ci: (row(bi, hi, ci), hi)),
        out_shape=jax.ShapeDtypeStruct((n, ret_w), BF16),
        scratch_shapes=[pltpu.VMEM((dh, dh), F32)],
        compiler_params=_params(("parallel", "parallel", "arbitrary"), vmem),
        name="retention",
    )(z, z, z, z, cos, sin, dmask, qdec, kdec, cdec)


def _rwkv_prep_kernel(has_vres, t_len, lora_ranks, *refs):
    if has_vres:
        (zr_ref, zk_ref, zv_ref, zl_ref, mur_ref, muk_ref, muv_ref, mul_ref, wl_ref, w0_ref, a0_ref,
         kk_ref, ka_ref, ones_ref, tri_ref, vf_ref, v0_ref, v1_ref, v2_ref,
         r_out, lw_out, c_out, k_out, v_out, p_out, a_out, g_out, cr, ck, cv, cl) = refs
    else:
        (zr_ref, zk_ref, zv_ref, zl_ref, mur_ref, muk_ref, muv_ref, mul_ref, wl_ref, w0_ref, a0_ref,
         kk_ref, ka_ref, ones_ref, tri_ref,
         r_out, lw_out, c_out, k_out, v_out, p_out, a_out, g_out, cr, ck, cv, cl) = refs
    tm = zr_ref.shape[0]
    rw = zr_ref.shape[1]
    at_start = (pl.program_id(0) * tm) % t_len == 0
    row0 = lax.broadcasted_iota(jnp.int32, (tm, 1), 0) == 0

    def shift_mix(x_ref, carry_ref, mu_ref):
        x = x_ref[...]
        last = jnp.where(at_start, 0.0, carry_ref[...])
        prev = jnp.where(row0, last, pltpu.roll(x, 1, 0))
        carry_ref[...] = x[tm - 1:tm, :]
        return x + mu_ref[...] * (prev - x)

    r = shift_mix(zr_ref, cr, mur_ref)
    kr = shift_mix(zk_ref, ck, muk_ref)
    vr = shift_mix(zv_ref, cv, muv_ref)
    lo = shift_mix(zl_ref, cl, mul_ref)

    rank_w, rank_a = lora_ranks
    lane = lax.broadcasted_iota(jnp.int32, (1, lo.shape[1]), 1)
    feat = jnp.where(lane < rank_w, jnp.tanh(lo), jnp.where(lane < rank_w + rank_a, lo, _sigmoid(lo)))
    proj = _dot(feat.astype(BF16), wl_ref[...])
    log_w = -math.exp(-0.5) * _sigmoid(w0_ref[...] + proj[:, :rw])
    iclr = _sigmoid(a0_ref[...] + proj[:, rw:2 * rw])
    gate = proj[:, 2 * rw:]

    if has_vres:
        low = _dot(vr.astype(BF16), v1_ref[...])
        mix = _sigmoid(v0_ref[...] + _dot(low.astype(BF16), v2_ref[...]))
        vr = vr + (vf_ref[...] - vr) * mix

    kk = kr * kk_ref[...]
    norm = jnp.sqrt(_seg_sum(kk * kk, ones_ref[...]))
    kk = kk / jnp.maximum(norm, 1e-12)
    k2 = kr * (1.0 + (iclr - 1.0) * ka_ref[...])

    l_hi, l_mid, l_lo = _split3(log_w)
    tri = tri_ref[...]
    c_out[...] = _dot(tri, l_hi) + _dot(tri, l_mid) + _dot(tri, l_lo)

    r_out[...] = r
    lw_out[...] = log_w
    k_out[...] = k2
    v_out[...] = vr
    p_out[...] = kk
    a_out[...] = iclr
    g_out[...] = gate


def _head_ones(width, head):
    idx = jnp.arange(width) // head
    return (idx[:, None] == idx[None, :]).astype(BF16)


def _rwkv_prep(z, z_lora, t, ret_in, rw, mu, w2, a2, g2, w0, a0, k_k, k_a, v_first, v_res):
    n = z.shape[0]
    has_vres = v_res is not None
    rank_w, rank_a, rank_g = w2.shape[0], a2.shape[0], g2.shape[0]
    lora = rank_w + rank_a + rank_g
    assert lora <= LORA_PAD == z_lora.shape[1] and ret_in % rw == 0
    tm = _tile(t, 256, RWKV_CHUNK)
    pos = jnp.arange(tm)
    tri = ((pos[:, None] // RWKV_CHUNK == pos[None, :] // RWKV_CHUNK)
           & (pos[:, None] >= pos[None, :])).astype(BF16)
    row2 = lambda v: v.reshape(1, -1)
    wl = jnp.zeros((LORA_PAD, 3 * rw), F32)
    wl = wl.at[:rank_w, :rw].set(w2).at[rank_w:rank_w + rank_a, rw:2 * rw].set(a2)
    wl = wl.at[rank_w + rank_a:lora, 2 * rw:].set(g2).astype(BF16)
    mu_l = jnp.zeros((1, LORA_PAD), F32).at[0, :lora].set(mu[3 * rw:])
    c0 = ret_in // rw
    zspec = lambda ci: pl.BlockSpec((tm, rw), lambda i: (i, ci))
    vec = pl.BlockSpec((1, rw), lambda i: (0, 0))
    full = lambda a: pl.BlockSpec(a.shape, lambda i: (0,) * a.ndim)
    ones_bd = _head_ones(rw, RWKV_HEAD_DIM)
    args = [z, z, z, z_lora, row2(mu[:rw]), row2(mu[rw:2 * rw]), row2(mu[2 * rw:3 * rw]), mu_l, wl,
            row2(w0), row2(a0), row2(k_k), row2(k_a), ones_bd, tri]
    specs = [zspec(c0), zspec(c0 + 1), zspec(c0 + 2),
             pl.BlockSpec((tm, LORA_PAD), lambda i: (i, 0)),
             vec, vec, vec, full(mu_l), full(wl), vec, vec, vec, vec, full(ones_bd), full(tri)]
    if has_vres:
        v0, v1, v2 = v_res
        rank_v = v1.shape[1]
        rank_pad = -(-rank_v // V7X_LANES) * V7X_LANES
        v1p = jnp.zeros((rw, rank_pad), F32).at[:, :rank_v].set(v1).astype(BF16)
        v2p = jnp.zeros((rank_pad, rw), F32).at[:rank_v, :].set(v2).astype(BF16)
        args += [v_first, row2(v0), v1p, v2p]
        specs += [pl.BlockSpec((tm, rw), lambda i: (i, 0)), vec, full(v1p), full(v2p)]
    out_spec = pl.BlockSpec((tm, rw), lambda i: (i, 0))
    vmem = (2 * (3 + has_vres) * _nbytes((tm, rw), F32) + 2 * _nbytes((tm, LORA_PAD), F32)
            + 2 * _nbytes(wl.shape, BF16) + 2 * _nbytes((rw, rw), BF16)
            + 16 * _nbytes((tm, rw), F32) + 12 * _nbytes((tm, rw), F32))
    return pl.pallas_call(
        functools.partial(_rwkv_prep_kernel, has_vres, t, (rank_w, rank_a)),
        grid=(n // tm,),
        in_specs=specs,
        out_specs=[out_spec] * 8,
        out_shape=[jax.ShapeDtypeStruct((n, rw), F32)] * 8,
        scratch_shapes=[pltpu.VMEM((1, rw), F32)] * 3 + [pltpu.VMEM((1, LORA_PAD), F32)],
        compiler_params=_params(("arbitrary",), vmem),
        name="rwkv_prep",
    )(*args)


def _rwkv_core_kernel(r_ref, lw_ref, c_ref, k_ref, v_ref, p_ref, a_ref, g_ref, rk_ref, lnw_ref,
                      lnb_ref, o_ref, h_ref, y_ref):
    @pl.when(pl.program_id(2) == 0)
    def _():
        h_ref[...] = jnp.zeros_like(h_ref)

    L = RWKV_CHUNK
    W = RWKV_PAIR
    hd = RWKV_HEAD_DIM
    n_chunks = r_ref.shape[0] // L
    n_pairs = r_ref.shape[1] // W

    ri = lax.broadcasted_iota(jnp.int32, (W, W), 0)
    ci = lax.broadcasted_iota(jnp.int32, (W, W), 1)
    same_head = (ri // hd) == (ci // hd)
    strict = same_head & ((ri % hd) > (ci % hd))
    incl = same_head & ((ri % hd) >= (ci % hd))
    eye = (ri == ci).astype(F32)
    merge_masks = []
    s = 1
    while s < hd:
        merge_masks.append(((ri // (2 * s)) == (ci // (2 * s))) & ((ri // s) != (ci // s)))
        s *= 2
    ones_bd = same_head.astype(BF16)
    bd_mask = (lax.broadcasted_iota(jnp.int32, (2 * L, W), 0) // L) == (
        lax.broadcasted_iota(jnp.int32, (2 * L, W), 1) // hd)

    def bd(x):
        return jnp.where(bd_mask, jnp.concatenate([x, x], axis=0), 0.0)

    def fold(x):
        return x[:L, :] + x[L:, :]

    def pair_chunk(sl, g):
        ln = slice(g * W, (g + 1) * W)
        r, lw, k, v, p = r_ref[sl, ln], lw_ref[sl, ln], k_ref[sl, ln], v_ref[sl, ln], p_ref[sl, ln]
        q = p * a_ref[sl, ln]
        c = c_ref[sl, ln]
        c_last = c[L - 1:L, :]
        e_in = jnp.exp(c)
        e_out = jnp.exp(-c)
        e_end = jnp.exp(c_last - c)
        rt, kt, qt = r * e_in, k * e_out, q * e_out
        pt = p * jnp.exp(c - lw)
        kh, qh = k * e_end, q * e_end

        pt_bd, rt_bd = bd(pt), bd(rt)
        lhs = jnp.concatenate([pt_bd, rt_bd], axis=0).astype(BF16)
        rhs = jnp.concatenate([bd(qt), bd(kt)], axis=0).astype(BF16)
        m = _dot_nt(lhs, rhs)
        yield
        a_pq = jnp.where(strict, m[:W, :W], 0.0)
        a_pk = jnp.where(strict, m[:W, W:], 0.0).astype(BF16)
        a_rq = jnp.where(incl, m[W:, :W], 0.0).astype(BF16)
        a_rk = jnp.where(incl, m[W:, W:], 0.0).astype(BF16)
        v_bd = bd(v).astype(BF16)
        pk_v = _dot(a_pk, v_bd)
        rk_v = _dot(a_rk, v_bd)
        yield

        t_inv = eye - jnp.where(merge_masks[0], a_pq, 0.0)
        for mask in merge_masks[1:]:
            tb = t_inv.astype(BF16)
            inner = _dot(jnp.where(mask, a_pq, 0.0).astype(BF16), tb)
            yield
            t_inv = t_inv - _dot(tb, inner.astype(BF16))
            yield
        t_inv = t_inv.astype(BF16)

        sol = _dot(t_inv, jnp.concatenate([pt_bd, pk_v], axis=1).astype(BF16)).astype(BF16)
        yield
        rq_sol = _dot(a_rq, sol)
        r_hat = fold(rt_bd - rq_sol[:, :W])
        y0 = fold(rk_v - rq_sol[:, W:])
        qh_sol = _dot_tn(bd(qh).astype(BF16), sol)
        g_mat = eye * jnp.exp(c_last) - qh_sol[:, :W]
        h_add = _dot_tn(bd(kh).astype(BF16), v_bd) - qh_sol[:, W:]
        yield

        h_hi, h_lo = _split2(h_ref[g])
        y_ref[sl, ln] = _dot(r_hat.astype(BF16), h_hi) + y0
        g_hi, g_lo = _split2(g_mat)
        gh = _dot(g_hi, jnp.concatenate([h_hi, h_lo], axis=1))
        h_ref[g] = gh[:, :W] + gh[:, W:] + _dot(g_lo, h_hi) + h_add

    def chunk(i, carry):
        sl = pl.ds(pl.multiple_of(i * L, L), L)
        live = [pair_chunk(sl, g) for g in range(n_pairs)]
        while live:
            live = [gen for gen in live if next(gen, True) is None]
        return carry

    lax.fori_loop(0, n_chunks, chunk, 0)

    inv_hd = 1.0 / hd
    for g in range(n_pairs):
        ln = slice(g * W, (g + 1) * W)
        y = y_ref[:, ln]
        mu = _seg_sum(y, ones_bd) * inv_hd
        cen = y - mu
        var = _seg_sum(cen * cen, ones_bd) * inv_hd
        yn = cen * lax.rsqrt(var + RWKV_LN_EPS) * lnw_ref[:, ln] + lnb_ref[:, ln]
        bonus = _seg_sum(r_ref[:, ln] * k_ref[:, ln] * rk_ref[:, ln], ones_bd) * v_ref[:, ln]
        o_ref[:, ln] = ((yn + bonus) * g_ref[:, ln]).astype(BF16)


def _rwkv_core(r, lw, c, k, v, p, a, gate, r_k, lnx_w, lnx_b, b, t):
    n, rw = r.shape
    assert rw % RWKV_PAIR == 0 and t % RWKV_CHUNK == 0
    group = _tile(rw, RWKV_GROUP_LANES, RWKV_PAIR)
    blk = _tile(t, 256, RWKV_CHUNK)
    nblk = t // blk
    tok = pl.BlockSpec((blk, group), lambda bi, pi, ci: (bi * nblk + ci, pi))
    vec = pl.BlockSpec((1, group), lambda bi, pi, ci: (0, pi))
    row2 = lambda x: x.reshape(1, rw)
    vmem = (2 * 8 * _nbytes((blk, group), F32) + 2 * _nbytes((blk, group), BF16)
            + _nbytes((blk, group), F32)
            + 64 * (group // RWKV_PAIR) * _nbytes((2 * RWKV_PAIR, 2 * RWKV_PAIR), F32)
            + 8 * _nbytes((blk, group), F32))
    return pl.pallas_call(
        _rwkv_core_kernel,
        grid=(b, rw // group, nblk),
        in_specs=[tok] * 8 + [vec] * 3,
        out_specs=tok,
        out_shape=jax.ShapeDtypeStruct((n, rw), BF16),
        scratch_shapes=[pltpu.VMEM((group // RWKV_PAIR, RWKV_PAIR, RWKV_PAIR), F32),
                        pltpu.VMEM((blk, group), F32)],
        compiler_params=_params(("parallel", "parallel", "arbitrary"), vmem),
        name="rwkv_core",
    )(r, lw, c, k, v, p, a, gate, row2(r_k), row2(lnx_w), row2(lnx_b))


def _lru_conv_gates(x, tail_prev, cw, cb, gxw, gxb, gaw, gab):
    lt, tail = x.shape[0], tail_prev.shape[0]
    ext = jnp.concatenate([tail_prev, x], axis=0)
    xc = cb + cw[CONV_WIDTH - 1:CONV_WIDTH, :] * x
    for j in range(1, CONV_WIDTH):
        xc = xc + cw[CONV_WIDTH - 1 - j:CONV_WIDTH - j, :] * pltpu.roll(ext, j, 0)[tail:, :]
    xcb = xc.astype(BF16)
    return xc, _dot(xcb, gxw) + gxb, _dot(xcb, gaw) + gab, x[lt - tail:, :]


def _lru_scan_gate(y, xc, pre_x, pre_a, lam, h_prev):
    lt = xc.shape[0]
    gate_x = _sigmoid(pre_x)
    gate_a = _sigmoid(pre_a)
    neg_lam = -lam
    softplus = jnp.maximum(neg_lam, 0.0) + jnp.log(1.0 + jnp.exp(-jnp.abs(neg_lam)))
    log_a = -LRU_C * gate_a * softplus
    a = jnp.exp(log_a)
    bb = xc * gate_x * jnp.sqrt(-jnp.tanh(log_a) * (a * a + 1.0))

    sub = V7X_SUBLANES
    n_groups = lt // sub
    a = a.reshape(n_groups, sub, a.shape[1])
    bb = bb.reshape(n_groups, sub, bb.shape[1])
    row_in_group = lax.broadcasted_iota(jnp.int32, (1, sub, 1), 1)
    s = 1
    while s < sub:
        valid = row_in_group >= s
        b_prev = jnp.where(valid, pltpu.roll(bb, s, 1), 0.0)
        a_prev = jnp.where(valid, pltpu.roll(a, s, 1), 1.0)
        bb = bb + a * b_prev
        a = a * a_prev
        s *= 2
    groups = []
    for gi in range(n_groups):
        hg = a[gi] * h_prev + bb[gi]
        groups.append(hg)
        h_prev = hg[sub - 1:sub, :]
    hs = jnp.concatenate(groups, axis=0)
    gelu = 0.5 * y * (1.0 + jnp.tanh(math.sqrt(2.0 / math.pi) * (y + 0.044715 * (y * y * y))))
    return (gelu * hs).astype(BF16), h_prev


def _odd_mixer_kernel(tiles_per_seq, nb, x_ref, g_ref, wy_ref, wx_ref, cw_ref, cb_ref, gxw_ref,
                      gxb_ref, gaw_ref, gab_ref, lam_ref, o_ref, xn_ref, zy0, zx0, zy1, zx1,
                      xtail_ref, h_ref):
    s = pl.program_id(0)
    n_proj = pl.num_programs(0) - 1
    cur = jnp.minimum(s, n_proj - 1)
    prev = jnp.maximum(s - 1, 0)
    blk_prev = prev % nb
    seq_start = (prev // nb) % tiles_per_seq == 0

    @pl.when(s == 0)
    def _():
        zy1[...] = jnp.zeros_like(zy1)
        zx1[...] = jnp.zeros_like(zx1)
        xtail_ref[...] = jnp.zeros_like(xtail_ref)
        h_ref[...] = jnp.zeros_like(h_ref)

    @pl.when((cur % nb == 0) & (s < n_proj))
    def _():
        xn_ref[...] = _rms(x_ref[...], g_ref[...]).astype(BF16)

    def step(proj_y, proj_x, scan_y, scan_x):
        tail_prev = jnp.where(seq_start, 0.0, xtail_ref[blk_prev])
        h_prev = jnp.where(seq_start, 0.0, h_ref[blk_prev])
        xc, pre_x, pre_a, tail_new = _lru_conv_gates(
            scan_x[...], tail_prev, cw_ref[...], cb_ref[...], gxw_ref[0], gxb_ref[...], gaw_ref[0],
            gab_ref[...])
        xn = xn_ref[...]
        proj_y[...] = _dot(xn, wy_ref[...])
        proj_x[...] = _dot(xn, wx_ref[...])
        out, h_new = _lru_scan_gate(scan_y[...], xc, pre_x, pre_a, lam_ref[...], h_prev)
        o_ref[...] = out
        xtail_ref[blk_prev] = tail_new
        h_ref[blk_prev] = h_new

    @pl.when(s % 2 == 0)
    def _():
        step(zy0, zx0, zy1, zx1)

    @pl.when(s % 2 == 1)
    def _():
        step(zy1, zx1, zy0, zx0)


def _odd_mixer(h, norm_g, w_in_bf16, layer, t, conv_w, conv_b, gx_w, gx_b, ga_w, ga_b, lam):
    n, d = h.shape
    assert d % LRU_BLOCK == 0 and w_in_bf16.shape[2] == 2 * d
    nb = d // LRU_BLOCK
    tm = _tile(t, 512)
    n_proj = (n // tm) * nb
    tail = V7X_SUBLANES
    assert CONV_WIDTH - 1 <= tail <= tm and nb >= 2
    cur = lambda s: jnp.minimum(s, n_proj - 1)
    prev = lambda s: jnp.maximum(s - 1, 0)
    vec = lambda rows: pl.BlockSpec((rows, LRU_BLOCK), lambda s: (0, prev(s) % nb))
    wsp = pl.BlockSpec((1, LRU_BLOCK, LRU_BLOCK), lambda s: (prev(s) % nb, 0, 0))
    row2 = lambda x: x.reshape(1, d)
    vmem = (2 * _nbytes((tm, d), F32) + _nbytes((tm, d), BF16) + 4 * _nbytes((d, LRU_BLOCK), BF16)
            + 8 * _nbytes((LRU_BLOCK, LRU_BLOCK), BF16) + 4 * _nbytes((tm, LRU_BLOCK), F32)
            + 28 * _nbytes((tm, LRU_BLOCK), F32) + 2 * _nbytes((tm, d), F32))
    return pl.pallas_call(
        functools.partial(_odd_mixer_kernel, t // tm, nb),
        grid=(n_proj + 1,),
        in_specs=[pl.BlockSpec((tm, d), lambda s: (cur(s) // nb, 0)),
                  pl.BlockSpec((1, d), lambda s: (0, 0)),
                  pl.BlockSpec((None, d, LRU_BLOCK), lambda s: (layer, 0, cur(s) % nb)),
                  pl.BlockSpec((None, d, LRU_BLOCK), lambda s: (layer, 0, nb + cur(s) % nb)),
                  vec(CONV_WIDTH), vec(1), wsp, vec(1), wsp, vec(1), vec(1)],
        out_specs=pl.BlockSpec((tm, LRU_BLOCK), lambda s: (prev(s) // nb, prev(s) % nb)),
        out_shape=jax.ShapeDtypeStruct((n, d), BF16),
        scratch_shapes=[pltpu.VMEM((tm, d), BF16)] + [pltpu.VMEM((tm, LRU_BLOCK), F32)] * 4
        + [pltpu.VMEM((nb, tail, LRU_BLOCK), F32), pltpu.VMEM((nb, 1, LRU_BLOCK), F32)],
        compiler_params=_params(("arbitrary",), vmem),
        name="odd_mixer",
    )(h, norm_g.reshape(1, d), w_in_bf16, w_in_bf16, conv_w, row2(conv_b), gx_w.astype(BF16),
      row2(gx_b), ga_w.astype(BF16), row2(ga_b), row2(lam))


def _even_layer(h, b, t, e, norm_g, w_in_all, w_out_all, mu, w0, w2, a0, a2, g2, k_k, k_a, r_k, lnx_w,
                lnx_b, v_first, v_res, rope):
    d = h.shape[1]
    ret_w = d // 2
    rw = d - ret_w
    ret_in = 4 * ret_w
    m_main = ret_in + 3 * rw
    lora = w_in_all.shape[2] - m_main
    w_lora = jnp.pad(w_in_all[e, :, m_main:], ((0, 0), (0, LORA_PAD - lora)))
    z, z_lora = _norm_matmul(h, norm_g, w_in_all, e, m_main, "even_in_proj", w_extra=w_lora)
    out_ret = _retention(z, b, t, ret_w, *rope)
    r, lw, c, k, v, p, a, gate = _rwkv_prep(z, z_lora, t, ret_in, rw, mu, w2, a2, g2, w0, a0, k_k, k_a,
                                            v_first, v_res)
    out_rw = _rwkv_core(r, lw, c, k, v, p, a, gate, r_k.reshape(-1), lnx_w, lnx_b, b, t)
    h = _matmul_residual([out_ret, out_rw], w_out_all, e, h, "even_out_proj")
    return h, (v if v_res is None else v_first)


def _odd_layer(h, t, o, norm_g, w_in_b, conv_w, conv_b, gx_w, gx_b, ga_w, ga_b, lam, w_out_b):
    gated = _odd_mixer(h, norm_g, w_in_b, o, t, conv_w, conv_b, gx_w, gx_b, ga_w, ga_b, lam)
    return _matmul_residual([gated], w_out_b, o, h, "odd_out_proj")


def kernel(x, ev_norm, ev_w_in, ev_w_out, rw_mu, rw_w0, rw_w2, rw_a0, rw_a2, rw_g2, rw_k_k, rw_k_a, rw_r_k, rw_lnx_w, rw_lnx_b, rw_v0, rw_v1, rw_v2, od_norm, od_w_in, od_conv_w, od_conv_b, od_gx_w, od_gx_b, od_ga_w, od_ga_b, od_lam, od_w_out, ff_norm, ff_w1, ff_w2, final_norm):
    b, t, d = x.shape
    depth = ff_norm.shape[0]
    h = x.reshape(b * t, d)
    rope = _rope_tables(t, (d // 2) // RET_HEADS)
    ff_w1_b, ff_w2_b = ff_w1.astype(BF16), ff_w2.astype(BF16)
    od_w_in_b, od_w_out_b, ev_w_out_b = od_w_in.astype(BF16), od_w_out.astype(BF16), ev_w_out.astype(BF16)
    v_first = None
    for layer in range(depth):
        if layer % 2 == 0:
            e = layer // 2
            v_res = None if e == 0 else (rw_v0[e - 1], rw_v1[e - 1], rw_v2[e - 1])
            h, v_first = _even_layer(
                h, b, t, e, ev_norm[e], ev_w_in, ev_w_out_b, rw_mu[e], rw_w0[e], rw_w2[e],
                rw_a0[e], rw_a2[e], rw_g2[e], rw_k_k[e], rw_k_a[e], rw_r_k[e], rw_lnx_w[e],
                rw_lnx_b[e], v_first, v_res, rope)
        else:
            o = layer // 2
            h = _odd_layer(h, t, o, od_norm[o], od_w_in_b, od_conv_w[o], od_conv_b[o], od_gx_w[o],
                           od_gx_b[o], od_ga_w[o], od_ga_b[o], od_lam[o], od_w_out_b)
        h = _mlp(h, ff_norm[layer], ff_w1_b, ff_w2_b, layer, f"mlp_{layer}",
                 final_g=final_norm if layer == depth - 1 else None)
    return h.reshape(b, t, d)
```

```python
import functools
import math

import jax
import jax.numpy as jnp
from jax import lax
from jax.experimental import pallas as pl
from jax.experimental.pallas import tpu as pltpu

F32 = jnp.float32
BF16 = jnp.bfloat16

NORM_EPS = 1e-6
RET_HEADS = 4
RET_GN_EPS = 1e-5
ROPE_BASE = 10000.0
STREAM_CHUNK = 64
RWKV_HEAD_DIM = 64
RWKV_LN_EPS = 64e-5
LRU_BLOCK = 256
CONV_WIDTH = 4
LRU_C = 8.0

V7X_LANES = 128
V7X_SUBLANES = 8
V7X_VMEM_BYTES = 64 * 1024 * 1024
VMEM_CAP_BYTES = V7X_VMEM_BYTES - 8 * 1024 * 1024
VMEM_FLOOR_BYTES = 16 * 1024 * 1024

RWKV_CHUNK = 64
RWKV_PAIR = 2 * RWKV_HEAD_DIM
RWKV_GROUP_LANES = 8 * RWKV_PAIR
LORA_PAD = 512


def _tile(n, pref, mult=V7X_SUBLANES):
    if n <= pref:
        return n
    t = (pref // mult) * mult
    while t >= mult:
        if n % t == 0:
            return t
        t -= mult
    raise ValueError(f"no tile for {n} <= {pref}")


def _params(semantics, vmem_bytes):
    limit = int(min(max(vmem_bytes, VMEM_FLOOR_BYTES), VMEM_CAP_BYTES))
    return pltpu.CompilerParams(dimension_semantics=semantics, vmem_limit_bytes=limit)


def _nbytes(shape, dtype):
    return math.prod(shape) * jnp.dtype(dtype).itemsize


def _dot(a, b):
    return jnp.dot(a, b, preferred_element_type=F32)


def _dot_nt(a, b):
    return lax.dot_general(a, b, (((1,), (1,)), ((), ())), preferred_element_type=F32)


def _dot_tn(a, b):
    return lax.dot_general(a, b, (((0,), (0,)), ((), ())), preferred_element_type=F32)


def _split2(x):
    hi = x.astype(BF16)
    lo = (x - hi.astype(F32)).astype(BF16)
    return hi, lo


def _split3(x):
    hi = x.astype(BF16)
    r1 = x - hi.astype(F32)
    mid = r1.astype(BF16)
    lo = (r1 - mid.astype(F32)).astype(BF16)
    return hi, mid, lo


def _seg_sum(x, ones_bd):
    hi, lo = _split2(x)
    return _dot(hi, ones_bd) + _dot(lo, ones_bd)


def _rms(x, g):
    ms = jnp.mean(x * x, axis=-1, keepdims=True)
    return x * lax.rsqrt(ms + NORM_EPS) * g


def _sigmoid(x):
    return jax.nn.sigmoid(x)


def _norm_matmul_kernel(n_main, has_extra, *refs):
    if has_extra:
        x_ref, g_ref, w_ref, we_ref, o_ref, oe_ref, xn_ref = refs
    else:
        x_ref, g_ref, w_ref, o_ref, xn_ref = refs
    j = pl.program_id(1)

    @pl.when(j == 0)
    def _():
        xn_ref[...] = _rms(x_ref[...], g_ref[...]).astype(BF16)

    @pl.when(j < n_main)
    def _():
        o_ref[...] = _dot(xn_ref[...], w_ref[...].astype(BF16))

    if has_extra:
        @pl.when(j == n_main)
        def _():
            oe_ref[...] = _dot(xn_ref[...], we_ref[...].astype(BF16))


def _norm_matmul(h, g, w, layer, m_main, name, w_extra=None):
    n, d = h.shape
    tm = _tile(n, 1024)
    tn = _tile(m_main, 1024, V7X_LANES)
    n_main = m_main // tn
    has_extra = w_extra is not None
    last = n_main - 1
    in_specs = [pl.BlockSpec((tm, d), lambda i, j: (i, 0), pipeline_mode=pl.Buffered(1)),
                pl.BlockSpec((1, d), lambda i, j: (0, 0)),
                pl.BlockSpec((None, d, tn), lambda i, j: (layer, 0, jnp.minimum(j, last)))]
    out_specs = [pl.BlockSpec((tm, tn), lambda i, j: (i, jnp.minimum(j, last)))]
    out_shape = [jax.ShapeDtypeStruct((n, m_main), F32)]
    args = [h, g.reshape(1, d), w]
    vmem = (_nbytes((tm, d), F32) + _nbytes((tm, d), BF16) + 2 * _nbytes((d, tn), w.dtype)
            + _nbytes((d, tn), BF16) + 3 * _nbytes((tm, tn), F32) + _nbytes((tm, d), F32))
    if has_extra:
        me = w_extra.shape[1]
        in_specs.append(pl.BlockSpec((d, me), lambda i, j: (0, 0)))
        out_specs.append(pl.BlockSpec((tm, me), lambda i, j: (i, 0)))
        out_shape.append(jax.ShapeDtypeStruct((n, me), F32))
        args.append(w_extra)
        vmem += 2 * _nbytes((d, me), w_extra.dtype) + 3 * _nbytes((tm, me), F32)
    outs = pl.pallas_call(
        functools.partial(_norm_matmul_kernel, n_main, has_extra),
        grid=(n // tm, n_main + has_extra),
        in_specs=in_specs,
        out_specs=out_specs,
        out_shape=out_shape,
        scratch_shapes=[pltpu.VMEM((tm, d), BF16)],
        compiler_params=_params(("parallel", "arbitrary"), vmem),
        name=name,
    )(*args)
    return outs if has_extra else outs[0]


def _matmul_residual_kernel(n_terms, *refs):
    x_refs, w_refs = refs[:n_terms], refs[n_terms:2 * n_terms]
    r_ref, o_ref = refs[2 * n_terms:]
    acc = r_ref[...]
    for x_ref, w_ref in zip(x_refs, w_refs):
        acc = acc + _dot(x_ref[...], w_ref[...])
    o_ref[...] = acc


def _matmul_residual(xs_bf16, w_bf16, layer, res, name):
    n, m = res.shape
    k = xs_bf16[0].shape[1]
    assert all(x.shape[1] == k for x in xs_bf16) and w_bf16.shape[1] == k * len(xs_bf16)
    tm = _tile(n, 512)
    ksum = w_bf16.shape[1]
    vmem = (2 * _nbytes((tm, ksum), BF16) + _nbytes((ksum, m), BF16)
            + (5 + len(xs_bf16)) * _nbytes((tm, m), F32))
    x_specs = [pl.BlockSpec((tm, k), lambda i: (i, 0)) for _ in xs_bf16]
    w_specs = [pl.BlockSpec((None, k, m), functools.partial(lambda i, r: (layer, r, 0), r=r),
                            pipeline_mode=pl.Buffered(1))
               for r in range(len(xs_bf16))]
    return pl.pallas_call(
        functools.partial(_matmul_residual_kernel, len(xs_bf16)),
        grid=(n // tm,),
        in_specs=x_specs + w_specs + [pl.BlockSpec((tm, m), lambda i: (i, 0))],
        out_specs=pl.BlockSpec((tm, m), lambda i: (i, 0)),
        out_shape=jax.ShapeDtypeStruct((n, m), F32),
        compiler_params=_params(("parallel",), vmem),
        name=name,
    )(*xs_bf16, *([w_bf16] * len(xs_bf16)), res)


def _mlp_kernel(has_final, *refs):
    if has_final:
        x_ref, g_ref, w1_ref, w2_ref, gf_ref, o_ref, xn_ref = refs
    else:
        x_ref, g_ref, w1_ref, w2_ref, o_ref, xn_ref = refs
    j = pl.program_id(1)

    @pl.when(j == 0)
    def _():
        x = x_ref[...]
        xn_ref[...] = _rms(x, g_ref[...]).astype(BF16)
        o_ref[...] = x

    a = jnp.maximum(_dot(xn_ref[...], w1_ref[...]), 0.0)
    o_ref[...] += _dot((a * a).astype(BF16), w2_ref[...])

    if has_final:
        @pl.when(j == pl.num_programs(1) - 1)
        def _():
            o_ref[...] = _rms(o_ref[...], gf_ref[...])


def _mlp(h, g, w1, w2, layer, name, final_g=None):
    n, d = h.shape
    f = w1.shape[2]
    tm = _tile(n, 1024)
    tf = _tile(f, 512, V7X_LANES)
    has_final = final_g is not None
    vec = pl.BlockSpec((1, d), lambda i, j: (0, 0))
    in_specs = [pl.BlockSpec((tm, d), lambda i, j: (i, 0), pipeline_mode=pl.Buffered(1)), vec,
                pl.BlockSpec((None, d, tf), lambda i, j: (layer, 0, j)),
                pl.BlockSpec((None, tf, d), lambda i, j: (layer, j, 0))]
    args = [h, g.reshape(1, d), w1, w2]
    if has_final:
        in_specs.append(vec)
        args.append(final_g.reshape(1, d))
    vmem = (3 * _nbytes((tm, d), F32) + _nbytes((tm, d), BF16) + 4 * _nbytes((d, tf), w1.dtype)
            + 3 * _nbytes((tm, tf), F32) + _nbytes((tm, d), F32))
    return pl.pallas_call(
        functools.partial(_mlp_kernel, has_final),
        grid=(n // tm, f // tf),
        in_specs=in_specs,
        out_specs=pl.BlockSpec((tm, d), lambda i, j: (i, 0)),
        out_shape=jax.ShapeDtypeStruct((n, d), F32),
        scratch_shapes=[pltpu.VMEM((tm, d), BF16)],
        compiler_params=_params(("parallel", "arbitrary"), vmem),
        name=name,
    )(*args)


def _retention_tables(blk, dh):
    log_g = jnp.log1p(-jnp.exp2(-5.0 - jnp.arange(RET_HEADS, dtype=F32)))
    pos = jnp.arange(blk, dtype=F32)
    n, m = pos[:, None], pos[None, :]
    cn, cm = jnp.floor(n / STREAM_CHUNK), jnp.floor(m / STREAM_CHUNK)
    dist = jnp.where(cn == cm, jnp.abs(n - m), n - m)
    lg = log_g[:, None, None]
    dmask = jnp.where((cm <= cn)[None], jnp.exp(lg * dist[None]), 0.0)
    qdec = jnp.broadcast_to(jnp.exp(lg * (pos + 1.0)[None, :, None]), (RET_HEADS, blk, dh))
    kdec = jnp.broadcast_to(jnp.exp(lg * (blk - 1.0 - pos)[None, :, None]), (RET_HEADS, blk, dh))
    cdec = jnp.broadcast_to(jnp.exp(lg * blk), (RET_HEADS, 1, dh))
    return dmask, qdec, kdec, cdec


def _rope_tables(t, dh):
    inv = 1.0 / (ROPE_BASE ** (jnp.arange(0, dh, 2, dtype=F32) / dh))
    ang = jnp.arange(t, dtype=F32)[:, None] * inv[None, :]
    return jnp.cos(ang), jnp.sin(ang)


def _retention_kernel(q_ref, k_ref, v_ref, g_ref, cos_ref, sin_ref, dm_ref, qd_ref, kd_ref,
                      cd_ref, o_ref, s_ref):
    @pl.when(pl.program_id(2) == 0)
    def _():
        s_ref[...] = jnp.zeros_like(s_ref)

    dh = q_ref.shape[1]
    half = dh // 2
    cos, sin = cos_ref[...], sin_ref[...]

    def rot(t):
        t1, t2 = t[:, :half], t[:, half:]
        return jnp.concatenate([t1 * cos - t2 * sin, t2 * cos + t1 * sin], axis=-1)

    q = rot(q_ref[...])
    k = rot(k_ref[...]) * (dh ** -0.5)
    vb = v_ref[...].astype(BF16)
    scores = _dot_nt(q.astype(BF16), k.astype(BF16)) * dm_ref[0]
    state = s_ref[...]
    out = _dot(scores.astype(BF16), vb) + _dot((q * qd_ref[0]).astype(BF16), state.astype(BF16))
    s_ref[...] = state * cd_ref[0] + _dot_tn((k * kd_ref[0]).astype(BF16), vb)

    mu = jnp.mean(out, axis=-1, keepdims=True)
    cen = out - mu
    var = jnp.mean(cen * cen, axis=-1, keepdims=True)
    y = cen * lax.rsqrt(var + RET_GN_EPS)
    g = g_ref[...]
    o_ref[...] = (g * _sigmoid(g) * y).astype(BF16)


def _retention(z, b, t, ret_w, cos, sin):
    n = z.shape[0]
    dh = ret_w // RET_HEADS
    blk = _tile(t, 512, STREAM_CHUNK)
    nblk = t // blk
    dmask, qdec, kdec, cdec = _retention_tables(blk, dh)
    row = lambda bi, hi, ci: bi * nblk + ci
    col_spec = lambda off: pl.BlockSpec((blk, dh), lambda bi, hi, ci: (row(bi, hi, ci), off + hi))
    head_spec = lambda shape: pl.BlockSpec((1,) + shape, lambda bi, hi, ci: (hi, 0, 0))
    vmem = (2 * 4 * _nbytes((blk, dh), F32) + 4 * _nbytes((blk, dh // 2), F32)
            + 2 * _nbytes((blk, blk), F32) + 4 * _nbytes((blk, dh), F32) + _nbytes((dh, dh), F32)
            + 12 * _nbytes((blk, max(blk, dh)), F32))
    return pl.pallas_call(
        _retention_kernel,
        grid=(b, RET_HEADS, nblk),
        in_specs=[col_spec(0), col_spec(RET_HEADS), col_spec(2 * RET_HEADS), col_spec(3 * RET_HEADS),
                  pl.BlockSpec((blk, dh // 2), lambda bi, hi, ci: (ci, 0)),
                  pl.BlockSpec((blk, dh // 2), lambda bi, hi, ci: (ci, 0)),
                  head_spec((blk, blk)), head_spec((blk, dh)), head_spec((blk, dh)),
                  head_spec((1, dh))],
        out_specs=pl.BlockSpec((blk, dh), lambda bi, hi, ci: (row(bi, hi, ci), hi)),
        out_shape=jax.ShapeDtypeStruct((n, ret_w), BF16),
        scratch_shapes=[pltpu.VMEM((dh, dh), F32)],
        compiler_params=_params(("parallel", "parallel", "arbitrary"), vmem),
        name="retention",
    )(z, z, z, z, cos, sin, dmask, qdec, kdec, cdec)


def _rwkv_prep_kernel(has_vres, t_len, lora_ranks, *refs):
    if has_vres:
        (zr_ref, zk_ref, zv_ref, zl_ref, mur_ref, muk_ref, muv_ref, mul_ref, wl_ref, w0_ref, a0_ref,
         kk_ref, ka_ref, ones_ref, tri_ref, vf_ref, v0_ref, v1_ref, v2_ref,
         r_out, lw_out, c_out, k_out, v_out, p_out, a_out, g_out, cr, ck, cv, cl) = refs
    else:
        (zr_ref, zk_ref, zv_ref, zl_ref, mur_ref, muk_ref, muv_ref, mul_ref, wl_ref, w0_ref, a0_ref,
         kk_ref, ka_ref, ones_ref, tri_ref,
         r_out, lw_out, c_out, k_out, v_out, p_out, a_out, g_out, cr, ck, cv, cl) = refs
    tm = zr_ref.shape[0]
    rw = zr_ref.shape[1]
    at_start = (pl.program_id(0) * tm) % t_len == 0
    row0 = lax.broadcasted_iota(jnp.int32, (tm, 1), 0) == 0

    def shift_mix(x_ref, carry_ref, mu_ref):
        x = x_ref[...]
        last = jnp.where(at_start, 0.0, carry_ref[...])
        prev = jnp.where(row0, last, pltpu.roll(x, 1, 0))
        carry_ref[...] = x[tm - 1:tm, :]
        return x + mu_ref[...] * (prev - x)

    r = shift_mix(zr_ref, cr, mur_ref)
    kr = shift_mix(zk_ref, ck, muk_ref)
    vr = shift_mix(zv_ref, cv, muv_ref)
    lo = shift_mix(zl_ref, cl, mul_ref)

    rank_w, rank_a = lora_ranks
    lane = lax.broadcasted_iota(jnp.int32, (1, lo.shape[1]), 1)
    feat = jnp.where(lane < rank_w, jnp.tanh(lo), jnp.where(lane < rank_w + rank_a, lo, _sigmoid(lo)))
    proj = _dot(feat.astype(BF16), wl_ref[...])
    log_w = -math.exp(-0.5) * _sigmoid(w0_ref[...] + proj[:, :rw])
    iclr = _sigmoid(a0_ref[...] + proj[:, rw:2 * rw])
    gate = proj[:, 2 * rw:]

    if has_vres:
        low = _dot(vr.astype(BF16), v1_ref[...])
        mix = _sigmoid(v0_ref[...] + _dot(low.astype(BF16), v2_ref[...]))
        vr = vr + (vf_ref[...] - vr) * mix

    kk = kr * kk_ref[...]
    norm = jnp.sqrt(_seg_sum(kk * kk, ones_ref[...]))
    kk = kk / jnp.maximum(norm, 1e-12)
    k2 = kr * (1.0 + (iclr - 1.0) * ka_ref[...])

    l_hi, l_mid, l_lo = _split3(log_w)
    tri = tri_ref[...]
    c_out[...] = _dot(tri, l_hi) + _dot(tri, l_mid) + _dot(tri, l_lo)

    r_out[...] = r
    lw_out[...] = log_w
    k_out[...] = k2
    v_out[...] = vr
    p_out[...] = kk
    a_out[...] = iclr
    g_out[...] = gate


def _head_ones(width, head):
    idx = jnp.arange(width) // head
    return (idx[:, None] == idx[None, :]).astype(BF16)


def _rwkv_prep(z, z_lora, t, ret_in, rw, mu, w2, a2, g2, w0, a0, k_k, k_a, v_first, v_res):
    n = z.shape[0]
    has_vres = v_res is not None
    rank_w, rank_a, rank_g = w2.shape[0], a2.shape[0], g2.shape[0]
    lora = rank_w + rank_a + rank_g
    assert lora <= LORA_PAD == z_lora.shape[1] and ret_in % rw == 0
    tm = _tile(t, 256, RWKV_CHUNK)
    pos = jnp.arange(tm)
    tri = ((pos[:, None] // RWKV_CHUNK == pos[None, :] // RWKV_CHUNK)
           & (pos[:, None] >= pos[None, :])).astype(BF16)
    row2 = lambda v: v.reshape(1, -1)
    wl = jnp.zeros((LORA_PAD, 3 * rw), F32)
    wl = wl.at[:rank_w, :rw].set(w2).at[rank_w:rank_w + rank_a, rw:2 * rw].set(a2)
    wl = wl.at[rank_w + rank_a:lora, 2 * rw:].set(g2).astype(BF16)
    mu_l = jnp.zeros((1, LORA_PAD), F32).at[0, :lora].set(mu[3 * rw:])
    c0 = ret_in // rw
    zspec = lambda ci: pl.BlockSpec((tm, rw), lambda i: (i, ci))
    vec = pl.BlockSpec((1, rw), lambda i: (0, 0))
    full = lambda a: pl.BlockSpec(a.shape, lambda i: (0,) * a.ndim)
    ones_bd = _head_ones(rw, RWKV_HEAD_DIM)
    args = [z, z, z, z_lora, row2(mu[:rw]), row2(mu[rw:2 * rw]), row2(mu[2 * rw:3 * rw]), mu_l, wl,
            row2(w0), row2(a0), row2(k_k), row2(k_a), ones_bd, tri]
    specs = [zspec(c0), zspec(c0 + 1), zspec(c0 + 2),
             pl.BlockSpec((tm, LORA_PAD), lambda i: (i, 0)),
             vec, vec, vec, full(mu_l), full(wl), vec, vec, vec, vec, full(ones_bd), full(tri)]
    if has_vres:
        v0, v1, v2 = v_res
        rank_v = v1.shape[1]
        rank_pad = -(-rank_v // V7X_LANES) * V7X_LANES
        v1p = jnp.zeros((rw, rank_pad), F32).at[:, :rank_v].set(v1).astype(BF16)
        v2p = jnp.zeros((rank_pad, rw), F32).at[:rank_v, :].set(v2).astype(BF16)
        args += [v_first, row2(v0), v1p, v2p]
        specs += [pl.BlockSpec((tm, rw), lambda i: (i, 0)), vec, full(v1p), full(v2p)]
    out_spec = pl.BlockSpec((tm, rw), lambda i: (i, 0))
    vmem = (2 * (3 + has_vres) * _nbytes((tm, rw), F32) + 2 * _nbytes((tm, LORA_PAD), F32)
            + 2 * _nbytes(wl.shape, BF16) + 2 * _nbytes((rw, rw), BF16)
            + 16 * _nbytes((tm, rw), F32) + 12 * _nbytes((tm, rw), F32))
    return pl.pallas_call(
        functools.partial(_rwkv_prep_kernel, has_vres, t, (rank_w, rank_a)),
        grid=(n // tm,),
        in_specs=specs,
        out_specs=[out_spec] * 8,
        out_shape=[jax.ShapeDtypeStruct((n, rw), F32)] * 8,
        scratch_shapes=[pltpu.VMEM((1, rw), F32)] * 3 + [pltpu.VMEM((1, LORA_PAD), F32)],
        compiler_params=_params(("arbitrary",), vmem),
        name="rwkv_prep",
    )(*args)


def _rwkv_core_kernel(r_ref, lw_ref, c_ref, k_ref, v_ref, p_ref, a_ref, g_ref, rk_ref, lnw_ref,
                      lnb_ref, o_ref, h_ref, y_ref):
    @pl.when(pl.program_id(2) == 0)
    def _():
        h_ref[...] = jnp.zeros_like(h_ref)

    L = RWKV_CHUNK
    W = RWKV_PAIR
    hd = RWKV_HEAD_DIM
    n_chunks = r_ref.shape[0] // L
    n_pairs = r_ref.shape[1] // W

    ri = lax.broadcasted_iota(jnp.int32, (W, W), 0)
    ci = lax.broadcasted_iota(jnp.int32, (W, W), 1)
    same_head = (ri // hd) == (ci // hd)
    strict = same_head & ((ri % hd) > (ci % hd))
    incl = same_head & ((ri % hd) >= (ci % hd))
    eye = (ri == ci).astype(F32)
    merge_masks = []
    s = 1
    while s < hd:
        merge_masks.append(((ri // (2 * s)) == (ci // (2 * s))) & ((ri // s) != (ci // s)))
        s *= 2
    ones_bd = same_head.astype(BF16)
    bd_mask = (lax.broadcasted_iota(jnp.int32, (2 * L, W), 0) // L) == (
        lax.broadcasted_iota(jnp.int32, (2 * L, W), 1) // hd)

    def bd(x):
        return jnp.where(bd_mask, jnp.concatenate([x, x], axis=0), 0.0)

    def fold(x):
        return x[:L, :] + x[L:, :]

    def pair_chunk(sl, g):
        ln = slice(g * W, (g + 1) * W)
        r, lw, k, v, p = r_ref[sl, ln], lw_ref[sl, ln], k_ref[sl, ln], v_ref[sl, ln], p_ref[sl, ln]
        q = p * a_ref[sl, ln]
        c = c_ref[sl, ln]
        c_last = c[L - 1:L, :]
        e_in = jnp.exp(c)
        e_out = jnp.exp(-c)
        e_end = jnp.exp(c_last - c)
        rt, kt, qt = r * e_in, k * e_out, q * e_out
        pt = p * jnp.exp(c - lw)
        kh, qh = k * e_end, q * e_end

        pt_bd, rt_bd = bd(pt), bd(rt)
        lhs = jnp.concatenate([pt_bd, rt_bd], axis=0).astype(BF16)
        rhs = jnp.concatenate([bd(qt), bd(kt)], axis=0).astype(BF16)
        m = _dot_nt(lhs, rhs)
        yield
        a_pq = jnp.where(strict, m[:W, :W], 0.0)
        a_pk = jnp.where(strict, m[:W, W:], 0.0).astype(BF16)
        a_rq = jnp.where(incl, m[W:, :W], 0.0).astype(BF16)
        a_rk = jnp.where(incl, m[W:, W:], 0.0).astype(BF16)
        v_bd = bd(v).astype(BF16)
        pk_v = _dot(a_pk, v_bd)
        rk_v = _dot(a_rk, v_bd)
        yield

        t_inv = eye - jnp.where(merge_masks[0], a_pq, 0.0)
        for mask in merge_masks[1:]:
            tb = t_inv.astype(BF16)
            inner = _dot(jnp.where(mask, a_pq, 0.0).astype(BF16), tb)
            yield
            t_inv = t_inv - _dot(tb, inner.astype(BF16))
            yield
        t_inv = t_inv.astype(BF16)

        sol = _dot(t_inv, jnp.concatenate([pt_bd, pk_v], axis=1).astype(BF16)).astype(BF16)
        yield
        rq_sol = _dot(a_rq, sol)
        r_hat = fold(rt_bd - rq_sol[:, :W])
        y0 = fold(rk_v - rq_sol[:, W:])
        qh_sol = _dot_tn(bd(qh).astype(BF16), sol)
        g_mat = eye * jnp.exp(c_last) - qh_sol[:, :W]
        h_add = _dot_tn(bd(kh).astype(BF16), v_bd) - qh_sol[:, W:]
        yield

        h_hi, h_lo = _split2(h_ref[g])
        y_ref[sl, ln] = _dot(r_hat.astype(BF16), h_hi) + y0
        g_hi, g_lo = _split2(g_mat)
        gh = _dot(g_hi, jnp.concatenate([h_hi, h_lo], axis=1))
        h_ref[g] = gh[:, :W] + gh[:, W:] + _dot(g_lo, h_hi) + h_add

    def chunk(i, carry):
        sl = pl.ds(pl.multiple_of(i * L, L), L)
        live = [pair_chunk(sl, g) for g in range(n_pairs)]
        while live:
            live = [gen for gen in live if next(gen, True) is None]
        return carry

    lax.fori_loop(0, n_chunks, chunk, 0, unroll=2)

    inv_hd = 1.0 / hd
    for g in range(n_pairs):
        ln = slice(g * W, (g + 1) * W)
        y = y_ref[:, ln]
        mu = _seg_sum(y, ones_bd) * inv_hd
        cen = y - mu
        var = _seg_sum(cen * cen, ones_bd) * inv_hd
        yn = cen * lax.rsqrt(var + RWKV_LN_EPS) * lnw_ref[:, ln] + lnb_ref[:, ln]
        bonus = _seg_sum(r_ref[:, ln] * k_ref[:, ln] * rk_ref[:, ln], ones_bd) * v_ref[:, ln]
        o_ref[:, ln] = ((yn + bonus) * g_ref[:, ln]).astype(BF16)


def _rwkv_core(r, lw, c, k, v, p, a, gate, r_k, lnx_w, lnx_b, b, t):
    n, rw = r.shape
    assert rw % RWKV_PAIR == 0 and t % RWKV_CHUNK == 0
    group = _tile(rw, RWKV_GROUP_LANES, RWKV_PAIR)
    blk = _tile(t, 256, RWKV_CHUNK)
    nblk = t // blk
    tok = pl.BlockSpec((blk, group), lambda bi, pi, ci: (bi * nblk + ci, pi))
    vec = pl.BlockSpec((1, group), lambda bi, pi, ci: (0, pi))
    row2 = lambda x: x.reshape(1, rw)
    vmem = (2 * 8 * _nbytes((blk, group), F32) + 2 * _nbytes((blk, group), BF16)
            + _nbytes((blk, group), F32)
            + 64 * (group // RWKV_PAIR) * _nbytes((2 * RWKV_PAIR, 2 * RWKV_PAIR), F32)
            + 8 * _nbytes((blk, group), F32))
    return pl.pallas_call(
        _rwkv_core_kernel,
        grid=(b, rw // group, nblk),
        in_specs=[tok] * 8 + [vec] * 3,
        out_specs=tok,
        out_shape=jax.ShapeDtypeStruct((n, rw), BF16),
        scratch_shapes=[pltpu.VMEM((group // RWKV_PAIR, RWKV_PAIR, RWKV_PAIR), F32),
                        pltpu.VMEM((blk, group), F32)],
        compiler_params=_params(("parallel", "parallel", "arbitrary"), vmem),
        name="rwkv_core",
    )(r, lw, c, k, v, p, a, gate, row2(r_k), row2(lnx_w), row2(lnx_b))


def _lru_conv_gates(x, tail_prev, cw, cb, gxw, gxb, gaw, gab):
    lt, tail = x.shape[0], tail_prev.shape[0]
    ext = jnp.concatenate([tail_prev, x], axis=0)
    xc = cb + cw[CONV_WIDTH - 1:CONV_WIDTH, :] * x
    for j in range(1, CONV_WIDTH):
        xc = xc + cw[CONV_WIDTH - 1 - j:CONV_WIDTH - j, :] * pltpu.roll(ext, j, 0)[tail:, :]
    xcb = xc.astype(BF16)
    return xc, _dot(xcb, gxw) + gxb, _dot(xcb, gaw) + gab, x[lt - tail:, :]


def _lru_scan(xc, pre_x, pre_a, lam, h_prev):
    lt = xc.shape[0]
    gate_x = _sigmoid(pre_x)
    gate_a = _sigmoid(pre_a)
    neg_lam = -lam
    softplus = jnp.maximum(neg_lam, 0.0) + jnp.log(1.0 + jnp.exp(-jnp.abs(neg_lam)))
    log_a = -LRU_C * gate_a * softplus
    a = jnp.exp(log_a)
    bb = xc * gate_x * jnp.sqrt(-jnp.tanh(log_a) * (a * a + 1.0))

    sub = V7X_SUBLANES
    n_groups = lt // sub
    a = a.reshape(n_groups, sub, a.shape[1])
    bb = bb.reshape(n_groups, sub, bb.shape[1])
    row_in_group = lax.broadcasted_iota(jnp.int32, (1, sub, 1), 1)
    s = 1
    while s < sub:
        valid = row_in_group >= s
        b_prev = jnp.where(valid, pltpu.roll(bb, s, 1), 0.0)
        a_prev = jnp.where(valid, pltpu.roll(a, s, 1), 1.0)
        bb = bb + a * b_prev
        a = a * a_prev
        s *= 2
    groups = []
    for gi in range(n_groups):
        hg = a[gi] * h_prev + bb[gi]
        groups.append(hg)
        h_prev = hg[sub - 1:sub, :]
    return jnp.concatenate(groups, axis=0), h_prev


def _odd_mixer_kernel(tiles_per_seq, x_ref, g_ref, w_ref, cw_ref, cb_ref, gxw_ref, gxb_ref,
                      gaw_ref, gab_ref, lam_ref, o_ref, xn_ref, zy_ref, zx_ref, xtail_ref, h_ref):
    s = pl.program_id(0)
    n_tiles = pl.num_programs(0) - 1
    d = zy_ref.shape[1]
    seq_start = jnp.maximum(s - 1, 0) % tiles_per_seq == 0

    @pl.when(s == 0)
    def _():
        zy_ref[...] = jnp.zeros_like(zy_ref)
        zx_ref[...] = jnp.zeros_like(zx_ref)
        xtail_ref[...] = jnp.zeros_like(xtail_ref)
        h_ref[...] = jnp.zeros_like(h_ref)

    @pl.when(s < n_tiles)
    def _():
        xn_ref[...] = _rms(x_ref[...], g_ref[...]).astype(BF16)

    for blk in range(d // LRU_BLOCK):
        cols = slice(blk * LRU_BLOCK, (blk + 1) * LRU_BLOCK)
        xcols = slice(d + blk * LRU_BLOCK, d + (blk + 1) * LRU_BLOCK)
        tail_prev = jnp.where(seq_start, 0.0, xtail_ref[:, cols])
        h_prev = jnp.where(seq_start, 0.0, h_ref[:, cols])
        y = zy_ref[:, cols]
        xc, pre_x, pre_a, tail_new = _lru_conv_gates(
            zx_ref[:, cols], tail_prev, cw_ref[:, cols], cb_ref[:, cols], gxw_ref[blk],
            gxb_ref[:, cols], gaw_ref[blk], gab_ref[:, cols])
        gelu = 0.5 * y * (1.0 + jnp.tanh(math.sqrt(2.0 / math.pi) * (y + 0.044715 * (y * y * y))))
        xn = xn_ref[...]
        zy_ref[:, cols] = _dot(xn, w_ref[:, cols])
        zx_ref[:, cols] = _dot(xn, w_ref[:, xcols])
        hs, h_new = _lru_scan(xc, pre_x, pre_a, lam_ref[:, cols], h_prev)
        o_ref[:, cols] = (gelu * hs).astype(BF16)
        xtail_ref[:, cols] = tail_new
        h_ref[:, cols] = h_new


def _odd_mixer(h, norm_g, w_in_bf16, layer, t, conv_w, conv_b, gx_w, gx_b, ga_w, ga_b, lam):
    n, d = h.shape
    assert d % LRU_BLOCK == 0 and w_in_bf16.shape[2] == 2 * d
    nb = d // LRU_BLOCK
    tm = _tile(t, 256)
    n_tiles = n // tm
    tail = V7X_SUBLANES
    assert CONV_WIDTH - 1 <= tail <= tm
    once = pl.Buffered(1)
    vec = lambda rows: pl.BlockSpec((rows, d), lambda s: (0, 0), pipeline_mode=once)
    wsp = pl.BlockSpec((nb, LRU_BLOCK, LRU_BLOCK), lambda s: (0, 0, 0), pipeline_mode=once)
    row2 = lambda x: x.reshape(1, d)
    vmem = (_nbytes((d, 2 * d), BF16) + 2 * _nbytes((nb, LRU_BLOCK, LRU_BLOCK), BF16)
            + 2 * _nbytes((tm, d), F32) + _nbytes((tm, d), BF16) + 2 * _nbytes((tm, d), BF16)
            + 2 * _nbytes((tm, d), F32) + 2 * _nbytes((tm, d), F32)
            + 28 * _nbytes((tm, LRU_BLOCK), F32))
    return pl.pallas_call(
        functools.partial(_odd_mixer_kernel, t // tm),
        grid=(n_tiles + 1,),
        in_specs=[pl.BlockSpec((tm, d), lambda s: (jnp.minimum(s, n_tiles - 1), 0)),
                  vec(1),
                  pl.BlockSpec((None, d, 2 * d), lambda s: (layer, 0, 0), pipeline_mode=once),
                  vec(CONV_WIDTH), vec(1), wsp, vec(1), wsp, vec(1), vec(1)],
        out_specs=pl.BlockSpec((tm, d), lambda s: (jnp.maximum(s - 1, 0), 0)),
        out_shape=jax.ShapeDtypeStruct((n, d), BF16),
        scratch_shapes=[pltpu.VMEM((tm, d), BF16), pltpu.VMEM((tm, d), F32), pltpu.VMEM((tm, d), F32),
                        pltpu.VMEM((tail, d), F32), pltpu.VMEM((1, d), F32)],
        compiler_params=_params(("arbitrary",), vmem),
        name="odd_mixer",
    )(h, norm_g.reshape(1, d), w_in_bf16, conv_w, row2(conv_b), gx_w.astype(BF16), row2(gx_b),
      ga_w.astype(BF16), row2(ga_b), row2(lam))


def _even_layer(h, b, t, e, norm_g, w_in_all, w_out_all, mu, w0, w2, a0, a2, g2, k_k, k_a, r_k, lnx_w,
                lnx_b, v_first, v_res, rope):
    d = h.shape[1]
    ret_w = d // 2
    rw = d - ret_w
    ret_in = 4 * ret_w
    m_main = ret_in + 3 * rw
    lora = w_in_all.shape[2] - m_main
    w_lora = jnp.pad(w_in_all[e, :, m_main:], ((0, 0), (0, LORA_PAD - lora)))
    z, z_lora = _norm_matmul(h, norm_g, w_in_all, e, m_main, "even_in_proj", w_extra=w_lora)
    out_ret = _retention(z, b, t, ret_w, *rope)
    r, lw, c, k, v, p, a, gate = _rwkv_prep(z, z_lora, t, ret_in, rw, mu, w2, a2, g2, w0, a0, k_k, k_a,
                                            v_first, v_res)
    out_rw = _rwkv_core(r, lw, c, k, v, p, a, gate, r_k.reshape(-1), lnx_w, lnx_b, b, t)
    h = _matmul_residual([out_ret, out_rw], w_out_all, e, h, "even_out_proj")
    return h, (v if v_res is None else v_first)


def _odd_layer(h, t, o, norm_g, w_in_b, conv_w, conv_b, gx_w, gx_b, ga_w, ga_b, lam, w_out_b):
    gated = _odd_mixer(h, norm_g, w_in_b, o, t, conv_w, conv_b, gx_w, gx_b, ga_w, ga_b, lam)
    return _matmul_residual([gated], w_out_b, o, h, "odd_out_proj")


def kernel(x, ev_norm, ev_w_in, ev_w_out, rw_mu, rw_w0, rw_w2, rw_a0, rw_a2, rw_g2, rw_k_k, rw_k_a, rw_r_k, rw_lnx_w, rw_lnx_b, rw_v0, rw_v1, rw_v2, od_norm, od_w_in, od_conv_w, od_conv_b, od_gx_w, od_gx_b, od_ga_w, od_ga_b, od_lam, od_w_out, ff_norm, ff_w1, ff_w2, final_norm):
    b, t, d = x.shape
    depth = ff_norm.shape[0]
    h = x.reshape(b * t, d)
    rope = _rope_tables(t, (d // 2) // RET_HEADS)
    ff_w1_b, ff_w2_b = ff_w1.astype(BF16), ff_w2.astype(BF16)
    od_w_in_b, od_w_out_b, ev_w_out_b = od_w_in.astype(BF16), od_w_out.astype(BF16), ev_w_out.astype(BF16)
    v_first = None
    for layer in range(depth):
        if layer % 2 == 0:
            e = layer // 2
            v_res = None if e == 0 else (rw_v0[e - 1], rw_v1[e - 1], rw_v2[e - 1])
            h, v_first = _even_layer(
                h, b, t, e, ev_norm[e], ev_w_in, ev_w_out_b, rw_mu[e], rw_w0[e], rw_w2[e],
                rw_a0[e], rw_a2[e], rw_g2[e], rw_k_k[e], rw_k_a[e], rw_r_k[e], rw_lnx_w[e],
                rw_lnx_b[e], v_first, v_res, rope)
        else:
            o = layer // 2
            h = _odd_layer(h, t, o, od_norm[o], od_w_in_b, od_conv_w[o], od_conv_b[o], od_gx_w[o],
                           od_gx_b[o], od_ga_w[o], od_ga_b[o], od_lam[o], od_w_out_b)
        h = _mlp(h, ff_norm[layer], ff_w1_b, ff_w2_b, layer, f"mlp_{layer}",
                 final_g=final_norm if layer == depth - 1 else None)
    return h.reshape(b, t, d)
```

```python
import functools
import math

import jax
import jax.numpy as jnp
from jax import lax
from jax.experimental import pallas as pl
from jax.experimental.pallas import tpu as pltpu

F32 = jnp.float32
BF16 = jnp.bfloat16

NORM_EPS = 1e-6
RET_HEADS = 4
RET_GN_EPS = 1e-5
ROPE_BASE = 10000.0
STREAM_CHUNK = 64
RWKV_HEAD_DIM = 64
RWKV_LN_EPS = 64e-5
LRU_BLOCK = 256
CONV_WIDTH = 4
LRU_C = 8.0

V7X_LANES = 128
V7X_SUBLANES = 8
V7X_VMEM_BYTES = 64 * 1024 * 1024
VMEM_CAP_BYTES = V7X_VMEM_BYTES - 8 * 1024 * 1024
VMEM_FLOOR_BYTES = 16 * 1024 * 1024

RWKV_CHUNK = 64
RWKV_PAIR = 2 * RWKV_HEAD_DIM
RWKV_GROUP_LANES = 8 * RWKV_PAIR
LORA_PAD = 512


def _tile(n, pref, mult=V7X_SUBLANES):
    if n <= pref:
        return n
    t = (pref // mult) * mult
    while t >= mult:
        if n % t == 0:
            return t
        t -= mult
    raise ValueError(f"no tile for {n} <= {pref}")


def _params(semantics, vmem_bytes):
    limit = int(min(max(vmem_bytes, VMEM_FLOOR_BYTES), VMEM_CAP_BYTES))
    return pltpu.CompilerParams(dimension_semantics=semantics, vmem_limit_bytes=limit)


def _nbytes(shape, dtype):
    return math.prod(shape) * jnp.dtype(dtype).itemsize


def _dot(a, b):
    return jnp.dot(a, b, preferred_element_type=F32)


def _dot_nt(a, b):
    return lax.dot_general(a, b, (((1,), (1,)), ((), ())), preferred_element_type=F32)


def _dot_tn(a, b):
    return lax.dot_general(a, b, (((0,), (0,)), ((), ())), preferred_element_type=F32)


def _split2(x):
    hi = x.astype(BF16)
    lo = (x - hi.astype(F32)).astype(BF16)
    return hi, lo


def _split3(x):
    hi = x.astype(BF16)
    r1 = x - hi.astype(F32)
    mid = r1.astype(BF16)
    lo = (r1 - mid.astype(F32)).astype(BF16)
    return hi, mid, lo


def _seg_sum(x, ones_bd):
    hi, lo = _split2(x)
    return _dot(hi, ones_bd) + _dot(lo, ones_bd)


def _rms(x, g):
    ms = jnp.mean(x * x, axis=-1, keepdims=True)
    return x * lax.rsqrt(ms + NORM_EPS) * g


def _sigmoid(x):
    return jax.nn.sigmoid(x)


def _norm_matmul_kernel(x_ref, g_ref, w_ref, o_ref, xn_ref):
    @pl.when(pl.program_id(1) == 0)
    def _():
        xn_ref[...] = _rms(x_ref[...], g_ref[...]).astype(BF16)

    o_ref[...] = _dot(xn_ref[...], w_ref[...].astype(BF16))


def _norm_matmul(h, g, w, layer, m_cols, name):
    n, d = h.shape
    tm = _tile(n, 1024)
    tn = _tile(m_cols, 1024, V7X_LANES)
    vmem = (_nbytes((tm, d), F32) + _nbytes((tm, d), BF16) + 2 * _nbytes((d, tn), w.dtype)
            + _nbytes((d, tn), BF16) + 3 * _nbytes((tm, tn), F32) + _nbytes((tm, d), F32))
    return pl.pallas_call(
        _norm_matmul_kernel,
        grid=(n // tm, m_cols // tn),
        in_specs=[pl.BlockSpec((tm, d), lambda i, j: (i, 0), pipeline_mode=pl.Buffered(1)),
                  pl.BlockSpec((1, d), lambda i, j: (0, 0)),
                  pl.BlockSpec((None, d, tn), lambda i, j: (layer, 0, j))],
        out_specs=pl.BlockSpec((tm, tn), lambda i, j: (i, j)),
        out_shape=jax.ShapeDtypeStruct((n, m_cols), F32),
        scratch_shapes=[pltpu.VMEM((tm, d), BF16)],
        compiler_params=_params(("parallel", "arbitrary"), vmem),
        name=name,
    )(h, g.reshape(1, d), w)


def _matmul_residual_kernel(n_terms, *refs):
    x_refs, w_refs = refs[:n_terms], refs[n_terms:2 * n_terms]
    r_ref, o_ref = refs[2 * n_terms:]
    acc = r_ref[...]
    for x_ref, w_ref in zip(x_refs, w_refs):
        acc = acc + _dot(x_ref[...], w_ref[...])
    o_ref[...] = acc


def _matmul_residual(xs_bf16, w_bf16, layer, res, name):
    n, m = res.shape
    k = xs_bf16[0].shape[1]
    assert all(x.shape[1] == k for x in xs_bf16) and w_bf16.shape[1] == k * len(xs_bf16)
    tm = _tile(n, 512)
    ksum = w_bf16.shape[1]
    vmem = (2 * _nbytes((tm, ksum), BF16) + _nbytes((ksum, m), BF16)
            + (5 + len(xs_bf16)) * _nbytes((tm, m), F32))
    x_specs = [pl.BlockSpec((tm, k), lambda i: (i, 0)) for _ in xs_bf16]
    w_specs = [pl.BlockSpec((None, k, m), functools.partial(lambda i, r: (layer, r, 0), r=r),
                            pipeline_mode=pl.Buffered(1))
               for r in range(len(xs_bf16))]
    return pl.pallas_call(
        functools.partial(_matmul_residual_kernel, len(xs_bf16)),
        grid=(n // tm,),
        in_specs=x_specs + w_specs + [pl.BlockSpec((tm, m), lambda i: (i, 0))],
        out_specs=pl.BlockSpec((tm, m), lambda i: (i, 0)),
        out_shape=jax.ShapeDtypeStruct((n, m), F32),
        compiler_params=_params(("parallel",), vmem),
        name=name,
    )(*xs_bf16, *([w_bf16] * len(xs_bf16)), res)


def _mlp_kernel(has_final, *refs):
    if has_final:
        x_ref, g_ref, w1_ref, w2_ref, gf_ref, o_ref, xn_ref = refs
    else:
        x_ref, g_ref, w1_ref, w2_ref, o_ref, xn_ref = refs
    j = pl.program_id(1)

    @pl.when(j == 0)
    def _():
        x = x_ref[...]
        xn_ref[...] = _rms(x, g_ref[...]).astype(BF16)
        o_ref[...] = x

    a = jnp.maximum(_dot(xn_ref[...], w1_ref[...]), 0.0)
    o_ref[...] += _dot((a * a).astype(BF16), w2_ref[...])

    if has_final:
        @pl.when(j == pl.num_programs(1) - 1)
        def _():
            o_ref[...] = _rms(o_ref[...], gf_ref[...])


def _mlp(h, g, w1, w2, layer, name, final_g=None):
    n, d = h.shape
    f = w1.shape[2]
    tm = _tile(n, 1024)
    tf = _tile(f, 512, V7X_LANES)
    has_final = final_g is not None
    vec = pl.BlockSpec((1, d), lambda i, j: (0, 0))
    in_specs = [pl.BlockSpec((tm, d), lambda i, j: (i, 0), pipeline_mode=pl.Buffered(1)), vec,
                pl.BlockSpec((None, d, tf), lambda i, j: (layer, 0, j)),
                pl.BlockSpec((None, tf, d), lambda i, j: (layer, j, 0))]
    args = [h, g.reshape(1, d), w1, w2]
    if has_final:
        in_specs.append(vec)
        args.append(final_g.reshape(1, d))
    vmem = (3 * _nbytes((tm, d), F32) + _nbytes((tm, d), BF16) + 4 * _nbytes((d, tf), w1.dtype)
            + 3 * _nbytes((tm, tf), F32) + _nbytes((tm, d), F32))
    return pl.pallas_call(
        functools.partial(_mlp_kernel, has_final),
        grid=(n // tm, f // tf),
        in_specs=in_specs,
        out_specs=pl.BlockSpec((tm, d), lambda i, j: (i, 0)),
        out_shape=jax.ShapeDtypeStruct((n, d), F32),
        scratch_shapes=[pltpu.VMEM((tm, d), BF16)],
        compiler_params=_params(("parallel", "arbitrary"), vmem),
        name=name,
    )(*args)


def _retention_tables(blk, dh):
    log_g = jnp.log1p(-jnp.exp2(-5.0 - jnp.arange(RET_HEADS, dtype=F32)))
    pos = jnp.arange(blk, dtype=F32)
    n, m = pos[:, None], pos[None, :]
    cn, cm = jnp.floor(n / STREAM_CHUNK), jnp.floor(m / STREAM_CHUNK)
    dist = jnp.where(cn == cm, jnp.abs(n - m), n - m)
    lg = log_g[:, None, None]
    dmask = jnp.where((cm <= cn)[None], jnp.exp(lg * dist[None]), 0.0)
    qdec = jnp.broadcast_to(jnp.exp(lg * (pos + 1.0)[None, :, None]), (RET_HEADS, blk, dh))
    kdec = jnp.broadcast_to(jnp.exp(lg * (blk - 1.0 - pos)[None, :, None]), (RET_HEADS, blk, dh))
    cdec = jnp.broadcast_to(jnp.exp(lg * blk), (RET_HEADS, 1, dh))
    return dmask, qdec, kdec, cdec


def _rope_tables(t, dh):
    inv = 1.0 / (ROPE_BASE ** (jnp.arange(0, dh, 2, dtype=F32) / dh))
    ang = jnp.arange(t, dtype=F32)[:, None] * inv[None, :]
    return jnp.cos(ang), jnp.sin(ang)


def _retention_kernel(q_ref, k_ref, v_ref, g_ref, cos_ref, sin_ref, dm_ref, qd_ref, kd_ref,
                      cd_ref, o_ref, s_ref):
    @pl.when(pl.program_id(2) == 0)
    def _():
        s_ref[...] = jnp.zeros_like(s_ref)

    dh = q_ref.shape[1]
    half = dh // 2
    cos, sin = cos_ref[...], sin_ref[...]

    def rot(t):
        t1, t2 = t[:, :half], t[:, half:]
        return jnp.concatenate([t1 * cos - t2 * sin, t2 * cos + t1 * sin], axis=-1)

    q = rot(q_ref[...])
    k = rot(k_ref[...]) * (dh ** -0.5)
    vb = v_ref[...].astype(BF16)
    scores = _dot_nt(q.astype(BF16), k.astype(BF16)) * dm_ref[0]
    state = s_ref[...]
    out = _dot(scores.astype(BF16), vb) + _dot((q * qd_ref[0]).astype(BF16), state.astype(BF16))
    s_ref[...] = state * cd_ref[0] + _dot_tn((k * kd_ref[0]).astype(BF16), vb)

    mu = jnp.mean(out, axis=-1, keepdims=True)
    cen = out - mu
    var = jnp.mean(cen * cen, axis=-1, keepdims=True)
    y = cen * lax.rsqrt(var + RET_GN_EPS)
    g = g_ref[...]
    o_ref[...] = (g * _sigmoid(g) * y).astype(BF16)


def _retention(z, b, t, ret_w, cos, sin):
    n = z.shape[0]
    dh = ret_w // RET_HEADS
    blk = _tile(t, 512, STREAM_CHUNK)
    nblk = t // blk
    dmask, qdec, kdec, cdec = _retention_tables(blk, dh)
    row = lambda bi, hi, ci: bi * nblk + ci
    col_spec = lambda off: pl.BlockSpec((blk, dh), lambda bi, hi, ci: (row(bi, hi, ci), off + hi))
    head_spec = lambda shape: pl.BlockSpec((1,) + shape, lambda bi, hi, ci: (hi, 0, 0))
    vmem = (2 * 4 * _nbytes((blk, dh), F32) + 4 * _nbytes((blk, dh // 2), F32)
            + 2 * _nbytes((blk, blk), F32) + 4 * _nbytes((blk, dh), F32) + _nbytes((dh, dh), F32)
            + 12 * _nbytes((blk, max(blk, dh)), F32))
    return pl.pallas_call(
        _retention_kernel,
        grid=(b, RET_HEADS, nblk),
        in_specs=[col_spec(0), col_spec(RET_HEADS), col_spec(2 * RET_HEADS), col_spec(3 * RET_HEADS),
                  pl.BlockSpec((blk, dh // 2), lambda bi, hi, ci: (ci, 0)),
                  pl.BlockSpec((blk, dh // 2), lambda bi, hi, ci: (ci, 0)),
                  head_spec((blk, blk)), head_spec((blk, dh)), head_spec((blk, dh)),
                  head_spec((1, dh))],
        out_specs=pl.BlockSpec((blk, dh), lambda bi, hi, ci: (row(bi, hi, ci), hi)),
        out_shape=jax.ShapeDtypeStruct((n, ret_w), BF16),
        scratch_shapes=[pltpu.VMEM((dh, dh), F32)],
        compiler_params=_params(("parallel", "parallel", "arbitrary"), vmem),
        name="retention",
    )(z, z, z, z, cos, sin, dmask, qdec, kdec, cdec)


def _rwkv_in_kernel(has_vres, tiles_per_seq, rw, ranks, windows, *refs):
    if has_vres:
        (x_ref, g_ref, w_ref, mu_ref, wlw_ref, wla_ref, wlg_ref, w0_ref, a0_ref, kk_ref, ka_ref,
         tri_ref, vf_ref, v0_ref, v1_ref, v2_ref,
         r_out, lw_out, c_out, k_out, v_out, p_out, a_out, g_out, xn_ref, z_ref, carry_ref) = refs
    else:
        (x_ref, g_ref, w_ref, mu_ref, wlw_ref, wla_ref, wlg_ref, w0_ref, a0_ref, kk_ref, ka_ref,
         tri_ref,
         r_out, lw_out, c_out, k_out, v_out, p_out, a_out, g_out, xn_ref, z_ref, carry_ref) = refs
    s = pl.program_id(0)
    n_tiles = pl.num_programs(0) - 1
    tm = z_ref.shape[0]
    at_start = jnp.maximum(s - 1, 0) % tiles_per_seq == 0
    row0 = lax.broadcasted_iota(jnp.int32, (tm, 1), 0) == 0

    @pl.when(s == 0)
    def _():
        z_ref[...] = jnp.zeros_like(z_ref)
        carry_ref[...] = jnp.zeros_like(carry_ref)

    @pl.when(s < n_tiles)
    def _():
        xn_ref[...] = _rms(x_ref[...], g_ref[...]).astype(BF16)

    def shift_mix_then_project(lo, hi):
        x = z_ref[:, lo:hi]
        last = jnp.where(at_start, 0.0, carry_ref[:, lo:hi])
        prev = jnp.where(row0, last, pltpu.roll(x, 1, 0))
        carry_ref[:, lo:hi] = x[tm - 1:tm, :]
        mixed = x + mu_ref[:, lo:hi] * (prev - x)
        z_ref[:, lo:hi] = _dot(xn_ref[...], w_ref[:, lo:hi])
        return mixed

    rank_w, rank_a = ranks
    lora = shift_mix_then_project(3 * rw, z_ref.shape[1])
    lane = lax.broadcasted_iota(jnp.int32, (1, lora.shape[1]), 1)
    feat = jnp.where(lane < rank_w, jnp.tanh(lora),
                     jnp.where(lane < rank_w + rank_a, lora, _sigmoid(lora))).astype(BF16)
    (w_lo, w_hi), (a_lo, a_hi), (g_lo, g_hi) = windows
    proj_w = _dot(feat[:, w_lo:w_hi], wlw_ref[...])
    proj_a = _dot(feat[:, a_lo:a_hi], wla_ref[...])
    gate = _dot(feat[:, g_lo:g_hi], wlg_ref[...])

    r = shift_mix_then_project(0, rw)
    log_w = -math.exp(-0.5) * _sigmoid(w0_ref[...] + proj_w)
    iclr = _sigmoid(a0_ref[...] + proj_a)

    vr = shift_mix_then_project(2 * rw, 3 * rw)
    if has_vres:
        low = _dot(vr.astype(BF16), v1_ref[...])
        mix = _sigmoid(v0_ref[...] + _dot(low.astype(BF16), v2_ref[...]))
        vr = vr + (vf_ref[...] - vr) * mix

    kr = shift_mix_then_project(rw, 2 * rw)
    kk = kr * kk_ref[...]
    pair = lax.broadcasted_iota(jnp.int32, (RWKV_PAIR, RWKV_PAIR), 0) // RWKV_HEAD_DIM
    ones_bd = (pair == pair.T).astype(BF16)
    sq = kk * kk
    norm = jnp.sqrt(jnp.concatenate(
        [_seg_sum(sq[:, i:i + RWKV_PAIR], ones_bd) for i in range(0, rw, RWKV_PAIR)], axis=1))
    kk = kk / jnp.maximum(norm, 1e-12)
    k2 = kr * (1.0 + (iclr - 1.0) * ka_ref[...])

    l_hi, l_mid, l_lo = _split3(log_w)
    tri = tri_ref[...]
    c_out[...] = _dot(tri, l_hi) + _dot(tri, l_mid) + _dot(tri, l_lo)

    r_out[...] = r
    lw_out[...] = log_w
    k_out[...] = k2
    v_out[...] = vr
    p_out[...] = kk
    a_out[...] = iclr
    g_out[...] = gate


def _lane_window(lo, hi):
    return (lo // V7X_LANES) * V7X_LANES, -(-hi // V7X_LANES) * V7X_LANES


def _rwkv_in(h, norm_g, w_in, t, ret_in, rw, mu, w2, a2, g2, w0, a0, k_k, k_a, v_first, v_res):
    n, d = h.shape
    has_vres = v_res is not None
    rank_w, rank_a, rank_g = w2.shape[0], a2.shape[0], g2.shape[0]
    lora = rank_w + rank_a + rank_g
    assert lora <= LORA_PAD and rw % RWKV_PAIR == 0 and w_in.shape[1] == ret_in + 3 * rw + lora
    width = 3 * rw + LORA_PAD
    tm = _tile(t, 256, RWKV_CHUNK)
    n_tiles = n // tm
    pos = jnp.arange(tm)
    tri = ((pos[:, None] // RWKV_CHUNK == pos[None, :] // RWKV_CHUNK)
           & (pos[:, None] >= pos[None, :])).astype(BF16)
    row2 = lambda v: v.reshape(1, -1)
    w_rw = jnp.pad(w_in[:, ret_in:], ((0, 0), (0, LORA_PAD - lora))).astype(BF16)
    mu_p = jnp.pad(mu, (0, LORA_PAD - lora)).reshape(1, width)
    bounds = [(0, rank_w), (rank_w, rank_w + rank_a), (rank_w + rank_a, lora)]
    windows = tuple(_lane_window(lo, hi) for lo, hi in bounds)

    def embed(w_low, bound, window):
        rows = jnp.zeros((window[1] - window[0], rw), F32)
        return rows.at[bound[0] - window[0]:bound[1] - window[0]].set(w_low).astype(BF16)

    wl = [embed(wg, bd, win) for wg, bd, win in zip((w2, a2, g2), bounds, windows)]
    once = pl.Buffered(1)
    vec = pl.BlockSpec((1, rw), lambda s: (0, 0), pipeline_mode=once)
    full = lambda a: pl.BlockSpec(a.shape, lambda s: (0,) * a.ndim, pipeline_mode=once)
    prev_tile = pl.BlockSpec((tm, rw), lambda s: (jnp.maximum(s - 1, 0), 0))
    args = [h, row2(norm_g), w_rw, mu_p, *wl, row2(w0), row2(a0), row2(k_k), row2(k_a), tri]
    specs = [pl.BlockSpec((tm, d), lambda s: (jnp.minimum(s, n_tiles - 1), 0)),
             full(row2(norm_g)), full(w_rw), full(mu_p), *[full(w) for w in wl],
             vec, vec, vec, vec, full(tri)]
    resident = _nbytes(w_rw.shape, BF16) + sum(_nbytes(w.shape, BF16) for w in wl) + _nbytes(tri.shape, BF16)
    if has_vres:
        v0, v1, v2 = v_res
        rank_v = v1.shape[1]
        rank_pad = _lane_window(0, rank_v)[1]
        v1p = jnp.zeros((rw, rank_pad), F32).at[:, :rank_v].set(v1).astype(BF16)
        v2p = jnp.zeros((rank_pad, rw), F32).at[:rank_v, :].set(v2).astype(BF16)
        args += [v_first, row2(v0), v1p, v2p]
        specs += [prev_tile, vec, full(v1p), full(v2p)]
        resident += 2 * _nbytes(v1p.shape, BF16)
    vmem = (resident + 2 * _nbytes((tm, d), F32) + 2 * (8 + has_vres) * _nbytes((tm, rw), F32)
            + _nbytes((tm, d), BF16) + _nbytes((tm, width), F32) + 10 * _nbytes((tm, rw), F32))
    return pl.pallas_call(
        functools.partial(_rwkv_in_kernel, has_vres, t // tm, rw, (rank_w, rank_a), windows),
        grid=(n_tiles + 1,),
        in_specs=specs,
        out_specs=[prev_tile] * 8,
        out_shape=[jax.ShapeDtypeStruct((n, rw), F32)] * 8,
        scratch_shapes=[pltpu.VMEM((tm, d), BF16), pltpu.VMEM((tm, width), F32),
                        pltpu.VMEM((1, width), F32)],
        compiler_params=_params(("arbitrary",), vmem),
        name="rwkv_in",
    )(*args)


def _rwkv_core_kernel(r_ref, lw_ref, c_ref, k_ref, v_ref, p_ref, a_ref, g_ref, rk_ref, lnw_ref,
                      lnb_ref, o_ref, h_ref, y_ref):
    @pl.when(pl.program_id(2) == 0)
    def _():
        h_ref[...] = jnp.zeros_like(h_ref)

    L = RWKV_CHUNK
    W = RWKV_PAIR
    hd = RWKV_HEAD_DIM
    n_chunks = r_ref.shape[0] // L
    n_pairs = r_ref.shape[1] // W

    ri = lax.broadcasted_iota(jnp.int32, (W, W), 0)
    ci = lax.broadcasted_iota(jnp.int32, (W, W), 1)
    same_head = (ri // hd) == (ci // hd)
    strict = same_head & ((ri % hd) > (ci % hd))
    incl = same_head & ((ri % hd) >= (ci % hd))
    eye = (ri == ci).astype(F32)
    merge_masks = []
    s = 1
    while s < hd:
        merge_masks.append(((ri // (2 * s)) == (ci // (2 * s))) & ((ri // s) != (ci // s)))
        s *= 2
    ones_bd = same_head.astype(BF16)
    bd_mask = (lax.broadcasted_iota(jnp.int32, (2 * L, W), 0) // L) == (
        lax.broadcasted_iota(jnp.int32, (2 * L, W), 1) // hd)

    def bd(x):
        return jnp.where(bd_mask, jnp.concatenate([x, x], axis=0), 0.0)

    def fold(x):
        return x[:L, :] + x[L:, :]

    def pair_chunk(sl, g):
        ln = slice(g * W, (g + 1) * W)
        r, lw, k, v, p = r_ref[sl, ln], lw_ref[sl, ln], k_ref[sl, ln], v_ref[sl, ln], p_ref[sl, ln]
        q = p * a_ref[sl, ln]
        c = c_ref[sl, ln]
        c_last = c[L - 1:L, :]
        e_in = jnp.exp(c)
        e_out = jnp.exp(-c)
        e_end = jnp.exp(c_last - c)
        rt, kt, qt = r * e_in, k * e_out, q * e_out
        pt = p * jnp.exp(c - lw)
        kh, qh = k * e_end, q * e_end

        pt_bd, rt_bd = bd(pt), bd(rt)
        lhs = jnp.concatenate([pt_bd, rt_bd], axis=0).astype(BF16)
        rhs = jnp.concatenate([bd(qt), bd(kt)], axis=0).astype(BF16)
        m = _dot_nt(lhs, rhs)
        yield
        a_pq = jnp.where(strict, m[:W, :W], 0.0)
        a_pk = jnp.where(strict, m[:W, W:], 0.0).astype(BF16)
        a_rq = jnp.where(incl, m[W:, :W], 0.0).astype(BF16)
        a_rk = jnp.where(incl, m[W:, W:], 0.0).astype(BF16)
        v_bd = bd(v).astype(BF16)
        pk_v = _dot(a_pk, v_bd)
        rk_v = _dot(a_rk, v_bd)
        yield

        t_inv = eye - jnp.where(merge_masks[0], a_pq, 0.0)
        for mask in merge_masks[1:]:
            tb = t_inv.astype(BF16)
            inner = _dot(jnp.where(mask, a_pq, 0.0).astype(BF16), tb)
            yield
            t_inv = t_inv - _dot(tb, inner.astype(BF16))
            yield
        t_inv = t_inv.astype(BF16)

        sol = _dot(t_inv, jnp.concatenate([pt_bd, pk_v], axis=1).astype(BF16)).astype(BF16)
        yield
        rq_sol = _dot(a_rq, sol)
        r_hat = fold(rt_bd - rq_sol[:, :W])
        y0 = fold(rk_v - rq_sol[:, W:])
        qh_sol = _dot_tn(bd(qh).astype(BF16), sol)
        g_mat = eye * jnp.exp(c_last) - qh_sol[:, :W]
        h_add = _dot_tn(bd(kh).astype(BF16), v_bd) - qh_sol[:, W:]
        yield

        h_hi, h_lo = _split2(h_ref[g])
        y_ref[sl, ln] = _dot(r_hat.astype(BF16), h_hi) + y0
        g_hi, g_lo = _split2(g_mat)
        gh = _dot(g_hi, jnp.concatenate([h_hi, h_lo], axis=1))
        h_ref[g] = gh[:, :W] + gh[:, W:] + _dot(g_lo, h_hi) + h_add

    def chunk(i, carry):
        sl = pl.ds(pl.multiple_of(i * L, L), L)
        live = [pair_chunk(sl, g) for g in range(n_pairs)]
        while live:
            live = [gen for gen in live if next(gen, True) is None]
        return carry

    lax.fori_loop(0, n_chunks, chunk, 0, unroll=2)

    inv_hd = 1.0 / hd
    lanes = [slice(g * W, (g + 1) * W) for g in range(n_pairs)]
    ys = [y_ref[:, ln] for ln in lanes]
    mus = [_seg_sum(y, ones_bd) * inv_hd for y in ys]
    cens = [y - mu for y, mu in zip(ys, mus)]
    bonus_sums = [_seg_sum(r_ref[:, ln] * k_ref[:, ln] * rk_ref[:, ln], ones_bd) for ln in lanes]
    variances = [_seg_sum(cen * cen, ones_bd) * inv_hd for cen in cens]
    for ln, cen, var, bsum in zip(lanes, cens, variances, bonus_sums):
        yn = cen * lax.rsqrt(var + RWKV_LN_EPS) * lnw_ref[:, ln] + lnb_ref[:, ln]
        o_ref[:, ln] = ((yn + bsum * v_ref[:, ln]) * g_ref[:, ln]).astype(BF16)


def _rwkv_core(r, lw, c, k, v, p, a, gate, r_k, lnx_w, lnx_b, b, t):
    n, rw = r.shape
    assert rw % RWKV_PAIR == 0 and t % RWKV_CHUNK == 0
    group = _tile(rw, RWKV_GROUP_LANES, RWKV_PAIR)
    blk = _tile(t, 256, RWKV_CHUNK)
    nblk = t // blk
    tok = pl.BlockSpec((blk, group), lambda bi, pi, ci: (bi * nblk + ci, pi))
    vec = pl.BlockSpec((1, group), lambda bi, pi, ci: (0, pi))
    row2 = lambda x: x.reshape(1, rw)
    vmem = (2 * 8 * _nbytes((blk, group), F32) + 2 * _nbytes((blk, group), BF16)
            + _nbytes((blk, group), F32)
            + 64 * (group // RWKV_PAIR) * _nbytes((2 * RWKV_PAIR, 2 * RWKV_PAIR), F32)
            + 8 * _nbytes((blk, group), F32))
    return pl.pallas_call(
        _rwkv_core_kernel,
        grid=(b, rw // group, nblk),
        in_specs=[tok] * 8 + [vec] * 3,
        out_specs=tok,
        out_shape=jax.ShapeDtypeStruct((n, rw), BF16),
        scratch_shapes=[pltpu.VMEM((group // RWKV_PAIR, RWKV_PAIR, RWKV_PAIR), F32),
                        pltpu.VMEM((blk, group), F32)],
        compiler_params=_params(("parallel", "parallel", "arbitrary"), vmem),
        name="rwkv_core",
    )(r, lw, c, k, v, p, a, gate, row2(r_k), row2(lnx_w), row2(lnx_b))


def _lru_conv_gates(x, tail_prev, cw, cb, gxw, gxb, gaw, gab):
    lt, tail = x.shape[0], tail_prev.shape[0]
    ext = jnp.concatenate([tail_prev, x], axis=0)
    xc = cb + cw[CONV_WIDTH - 1:CONV_WIDTH, :] * x
    for j in range(1, CONV_WIDTH):
        xc = xc + cw[CONV_WIDTH - 1 - j:CONV_WIDTH - j, :] * pltpu.roll(ext, j, 0)[tail:, :]
    xcb = xc.astype(BF16)
    return xc, _dot(xcb, gxw) + gxb, _dot(xcb, gaw) + gab, x[lt - tail:, :]


def _lru_scan(xc, pre_x, pre_a, lam, h_prev):
    lt = xc.shape[0]
    gate_x = _sigmoid(pre_x)
    gate_a = _sigmoid(pre_a)
    neg_lam = -lam
    softplus = jnp.maximum(neg_lam, 0.0) + jnp.log(1.0 + jnp.exp(-jnp.abs(neg_lam)))
    log_a = -LRU_C * gate_a * softplus
    a = jnp.exp(log_a)
    bb = xc * gate_x * jnp.sqrt(-jnp.tanh(log_a) * (a * a + 1.0))

    sub = V7X_SUBLANES
    n_groups = lt // sub
    a = a.reshape(n_groups, sub, a.shape[1])
    bb = bb.reshape(n_groups, sub, bb.shape[1])
    row_in_group = lax.broadcasted_iota(jnp.int32, (1, sub, 1), 1)
    s = 1
    while s < sub:
        valid = row_in_group >= s
        b_prev = jnp.where(valid, pltpu.roll(bb, s, 1), 0.0)
        a_prev = jnp.where(valid, pltpu.roll(a, s, 1), 1.0)
        bb = bb + a * b_prev
        a = a * a_prev
        s *= 2
    groups = []
    for gi in range(n_groups):
        hg = a[gi] * h_prev + bb[gi]
        groups.append(hg)
        h_prev = hg[sub - 1:sub, :]
    return jnp.concatenate(groups, axis=0), h_prev


def _odd_mixer_kernel(tiles_per_seq, x_ref, g_ref, w_ref, cw_ref, cb_ref, gxw_ref, gxb_ref,
                      gaw_ref, gab_ref, lam_ref, o_ref, xn_ref, zy_ref, zx_ref, xtail_ref, h_ref):
    s = pl.program_id(0)
    n_tiles = pl.num_programs(0) - 1
    d = zy_ref.shape[1]
    seq_start = jnp.maximum(s - 1, 0) % tiles_per_seq == 0

    @pl.when(s == 0)
    def _():
        zy_ref[...] = jnp.zeros_like(zy_ref)
        zx_ref[...] = jnp.zeros_like(zx_ref)
        xtail_ref[...] = jnp.zeros_like(xtail_ref)
        h_ref[...] = jnp.zeros_like(h_ref)

    @pl.when(s < n_tiles)
    def _():
        xn_ref[...] = _rms(x_ref[...], g_ref[...]).astype(BF16)

    for blk in range(d // LRU_BLOCK):
        cols = slice(blk * LRU_BLOCK, (blk + 1) * LRU_BLOCK)
        xcols = slice(d + blk * LRU_BLOCK, d + (blk + 1) * LRU_BLOCK)
        tail_prev = jnp.where(seq_start, 0.0, xtail_ref[:, cols])
        h_prev = jnp.where(seq_start, 0.0, h_ref[:, cols])
        y = zy_ref[:, cols]
        xc, pre_x, pre_a, tail_new = _lru_conv_gates(
            zx_ref[:, cols], tail_prev, cw_ref[:, cols], cb_ref[:, cols], gxw_ref[blk],
            gxb_ref[:, cols], gaw_ref[blk], gab_ref[:, cols])
        gelu = 0.5 * y * (1.0 + jnp.tanh(math.sqrt(2.0 / math.pi) * (y + 0.044715 * (y * y * y))))
        xn = xn_ref[...]
        zy_ref[:, cols] = _dot(xn, w_ref[:, cols])
        zx_ref[:, cols] = _dot(xn, w_ref[:, xcols])
        hs, h_new = _lru_scan(xc, pre_x, pre_a, lam_ref[:, cols], h_prev)
        o_ref[:, cols] = (gelu * hs).astype(BF16)
        xtail_ref[:, cols] = tail_new
        h_ref[:, cols] = h_new


def _odd_mixer(h, norm_g, w_in_bf16, layer, t, conv_w, conv_b, gx_w, gx_b, ga_w, ga_b, lam):
    n, d = h.shape
    assert d % LRU_BLOCK == 0 and w_in_bf16.shape[2] == 2 * d
    nb = d // LRU_BLOCK
    tm = _tile(t, 256)
    n_tiles = n // tm
    tail = V7X_SUBLANES
    assert CONV_WIDTH - 1 <= tail <= tm
    once = pl.Buffered(1)
    vec = lambda rows: pl.BlockSpec((rows, d), lambda s: (0, 0), pipeline_mode=once)
    wsp = pl.BlockSpec((nb, LRU_BLOCK, LRU_BLOCK), lambda s: (0, 0, 0), pipeline_mode=once)
    row2 = lambda x: x.reshape(1, d)
    vmem = (_nbytes((d, 2 * d), BF16) + 2 * _nbytes((nb, LRU_BLOCK, LRU_BLOCK), BF16)
            + 2 * _nbytes((tm, d), F32) + _nbytes((tm, d), BF16) + 2 * _nbytes((tm, d), BF16)
            + 2 * _nbytes((tm, d), F32) + 2 * _nbytes((tm, d), F32)
            + 28 * _nbytes((tm, LRU_BLOCK), F32))
    return pl.pallas_call(
        functools.partial(_odd_mixer_kernel, t // tm),
        grid=(n_tiles + 1,),
        in_specs=[pl.BlockSpec((tm, d), lambda s: (jnp.minimum(s, n_tiles - 1), 0)),
                  vec(1),
                  pl.BlockSpec((None, d, 2 * d), lambda s: (layer, 0, 0), pipeline_mode=once),
                  vec(CONV_WIDTH), vec(1), wsp, vec(1), wsp, vec(1), vec(1)],
        out_specs=pl.BlockSpec((tm, d), lambda s: (jnp.maximum(s - 1, 0), 0)),
        out_shape=jax.ShapeDtypeStruct((n, d), BF16),
        scratch_shapes=[pltpu.VMEM((tm, d), BF16), pltpu.VMEM((tm, d), F32), pltpu.VMEM((tm, d), F32),
                        pltpu.VMEM((tail, d), F32), pltpu.VMEM((1, d), F32)],
        compiler_params=_params(("arbitrary",), vmem),
        name="odd_mixer",
    )(h, norm_g.reshape(1, d), w_in_bf16, conv_w, row2(conv_b), gx_w.astype(BF16), row2(gx_b),
      ga_w.astype(BF16), row2(ga_b), row2(lam))


def _even_layer(h, b, t, e, norm_g, w_in_all, w_out_all, mu, w0, w2, a0, a2, g2, k_k, k_a, r_k, lnx_w,
                lnx_b, v_first, v_res, rope):
    d = h.shape[1]
    ret_w = d // 2
    rw = d - ret_w
    ret_in = 4 * ret_w
    z_ret = _norm_matmul(h, norm_g, w_in_all, e, ret_in, "even_in_proj")
    out_ret = _retention(z_ret, b, t, ret_w, *rope)
    r, lw, c, k, v, p, a, gate = _rwkv_in(h, norm_g, w_in_all[e], t, ret_in, rw, mu, w2, a2, g2, w0, a0,
                                          k_k, k_a, v_first, v_res)
    out_rw = _rwkv_core(r, lw, c, k, v, p, a, gate, r_k.reshape(-1), lnx_w, lnx_b, b, t)
    h = _matmul_residual([out_ret, out_rw], w_out_all, e, h, "even_out_proj")
    return h, (v if v_res is None else v_first)


def _odd_layer(h, t, o, norm_g, w_in_b, conv_w, conv_b, gx_w, gx_b, ga_w, ga_b, lam, w_out_b):
    gated = _odd_mixer(h, norm_g, w_in_b, o, t, conv_w, conv_b, gx_w, gx_b, ga_w, ga_b, lam)
    return _matmul_residual([gated], w_out_b, o, h, "odd_out_proj")


def kernel(x, ev_norm, ev_w_in, ev_w_out, rw_mu, rw_w0, rw_w2, rw_a0, rw_a2, rw_g2, rw_k_k, rw_k_a, rw_r_k, rw_lnx_w, rw_lnx_b, rw_v0, rw_v1, rw_v2, od_norm, od_w_in, od_conv_w, od_conv_b, od_gx_w, od_gx_b, od_ga_w, od_ga_b, od_lam, od_w_out, ff_norm, ff_w1, ff_w2, final_norm):
    b, t, d = x.shape
    depth = ff_norm.shape[0]
    h = x.reshape(b * t, d)
    rope = _rope_tables(t, (d // 2) // RET_HEADS)
    ff_w1_b, ff_w2_b = ff_w1.astype(BF16), ff_w2.astype(BF16)
    od_w_in_b, od_w_out_b, ev_w_out_b = od_w_in.astype(BF16), od_w_out.astype(BF16), ev_w_out.astype(BF16)
    v_first = None
    for layer in range(depth):
        if layer % 2 == 0:
            e = layer // 2
            v_res = None if e == 0 else (rw_v0[e - 1], rw_v1[e - 1], rw_v2[e - 1])
            h, v_first = _even_layer(
                h, b, t, e, ev_norm[e], ev_w_in, ev_w_out_b, rw_mu[e], rw_w0[e], rw_w2[e],
                rw_a0[e], rw_a2[e], rw_g2[e], rw_k_k[e], rw_k_a[e], rw_r_k[e], rw_lnx_w[e],
                rw_lnx_b[e], v_first, v_res, rope)
        else:
            o = layer // 2
            h = _odd_layer(h, t, o, od_norm[o], od_w_in_b, od_conv_w[o], od_conv_b[o], od_gx_w[o],
                           od_gx_b[o], od_ga_w[o], od_ga_b[o], od_lam[o], od_w_out_b)
        h = _mlp(h, ff_norm[layer], ff_w1_b, ff_w2_b, layer, f"mlp_{layer}",
                 final_g=final_norm if layer == depth - 1 else None)
    return h.reshape(b, t, d)
```

```python
import functools
import math

import jax
import jax.numpy as jnp
from jax import lax
from jax.experimental import pallas as pl
from jax.experimental.pallas import tpu as pltpu

F32 = jnp.float32
BF16 = jnp.bfloat16

NORM_EPS = 1e-6
RET_HEADS = 4
RET_GN_EPS = 1e-5
ROPE_BASE = 10000.0
STREAM_CHUNK = 64
RWKV_HEAD_DIM = 64
RWKV_LN_EPS = 64e-5
LRU_BLOCK = 256
CONV_WIDTH = 4
LRU_C = 8.0

V7X_LANES = 128
V7X_SUBLANES = 8
V7X_VMEM_BYTES = 64 * 1024 * 1024
VMEM_CAP_BYTES = V7X_VMEM_BYTES - 8 * 1024 * 1024
VMEM_FLOOR_BYTES = 16 * 1024 * 1024

RWKV_CHUNK = 64
RWKV_PAIR = 2 * RWKV_HEAD_DIM
RWKV_GROUP_LANES = 8 * RWKV_PAIR
LORA_PAD = 512


def _tile(n, pref, mult=V7X_SUBLANES):
    if n <= pref:
        return n
    t = (pref // mult) * mult
    while t >= mult:
        if n % t == 0:
            return t
        t -= mult
    raise ValueError(f"no tile for {n} <= {pref}")


def _params(semantics, vmem_bytes):
    limit = int(min(max(vmem_bytes, VMEM_FLOOR_BYTES), VMEM_CAP_BYTES))
    return pltpu.CompilerParams(dimension_semantics=semantics, vmem_limit_bytes=limit)


def _nbytes(shape, dtype):
    return math.prod(shape) * jnp.dtype(dtype).itemsize


def _dot(a, b):
    return jnp.dot(a, b, preferred_element_type=F32)


def _dot_nt(a, b):
    return lax.dot_general(a, b, (((1,), (1,)), ((), ())), preferred_element_type=F32)


def _dot_tn(a, b):
    return lax.dot_general(a, b, (((0,), (0,)), ((), ())), preferred_element_type=F32)


def _split2(x):
    hi = x.astype(BF16)
    lo = (x - hi.astype(F32)).astype(BF16)
    return hi, lo


def _split3(x):
    hi = x.astype(BF16)
    r1 = x - hi.astype(F32)
    mid = r1.astype(BF16)
    lo = (r1 - mid.astype(F32)).astype(BF16)
    return hi, mid, lo


def _seg_sum(x, ones_bd):
    hi, lo = _split2(x)
    return _dot(hi, ones_bd) + _dot(lo, ones_bd)


def _rms(x, g):
    ms = jnp.mean(x * x, axis=-1, keepdims=True)
    return x * lax.rsqrt(ms + NORM_EPS) * g


def _sigmoid(x):
    return jax.nn.sigmoid(x)


def _norm_matmul_kernel(x_ref, g_ref, w_ref, o_ref, xn_ref):
    @pl.when(pl.program_id(1) == 0)
    def _():
        xn_ref[...] = _rms(x_ref[...], g_ref[...]).astype(BF16)

    o_ref[...] = _dot(xn_ref[...], w_ref[...].astype(BF16))


def _norm_matmul(h, g, w, layer, m_cols, name):
    n, d = h.shape
    tm = _tile(n, 1024)
    tn = _tile(m_cols, 1024, V7X_LANES)
    vmem = (2 * _nbytes((tm, d), F32) + _nbytes((tm, d), BF16) + 2 * _nbytes((d, tn), w.dtype)
            + _nbytes((d, tn), BF16) + 3 * _nbytes((tm, tn), F32) + _nbytes((tm, d), F32))
    return pl.pallas_call(
        _norm_matmul_kernel,
        grid=(n // tm, m_cols // tn),
        in_specs=[pl.BlockSpec((tm, d), lambda i, j: (i, 0)),
                  pl.BlockSpec((1, d), lambda i, j: (0, 0)),
                  pl.BlockSpec((None, d, tn), lambda i, j: (layer, 0, j))],
        out_specs=pl.BlockSpec((tm, tn), lambda i, j: (i, j)),
        out_shape=jax.ShapeDtypeStruct((n, m_cols), F32),
        scratch_shapes=[pltpu.VMEM((tm, d), BF16)],
        compiler_params=_params(("parallel", "arbitrary"), vmem),
        name=name,
    )(h, g.reshape(1, d), w)


def _matmul_residual_kernel(n_terms, *refs):
    x_refs, w_refs = refs[:n_terms], refs[n_terms:2 * n_terms]
    r_ref, o_ref = refs[2 * n_terms:]
    acc = r_ref[...]
    for x_ref, w_ref in zip(x_refs, w_refs):
        acc = acc + _dot(x_ref[...], w_ref[...])
    o_ref[...] = acc


def _matmul_residual(xs_bf16, w_bf16, layer, res, name):
    n, m = res.shape
    k = xs_bf16[0].shape[1]
    assert all(x.shape[1] == k for x in xs_bf16) and w_bf16.shape[1] == k * len(xs_bf16)
    tm = _tile(n, 512)
    ksum = w_bf16.shape[1]
    vmem = (2 * _nbytes((tm, ksum), BF16) + _nbytes((ksum, m), BF16)
            + (5 + len(xs_bf16)) * _nbytes((tm, m), F32))
    x_specs = [pl.BlockSpec((tm, k), lambda i: (i, 0)) for _ in xs_bf16]
    w_specs = [pl.BlockSpec((None, k, m), functools.partial(lambda i, r: (layer, r, 0), r=r),
                            pipeline_mode=pl.Buffered(1))
               for r in range(len(xs_bf16))]
    return pl.pallas_call(
        functools.partial(_matmul_residual_kernel, len(xs_bf16)),
        grid=(n // tm,),
        in_specs=x_specs + w_specs + [pl.BlockSpec((tm, m), lambda i: (i, 0))],
        out_specs=pl.BlockSpec((tm, m), lambda i: (i, 0)),
        out_shape=jax.ShapeDtypeStruct((n, m), F32),
        compiler_params=_params(("parallel",), vmem),
        name=name,
    )(*xs_bf16, *([w_bf16] * len(xs_bf16)), res)


def _mlp_kernel(has_final, *refs):
    if has_final:
        x_ref, g_ref, w1_ref, w2_ref, gf_ref, o_ref, xn_ref = refs
    else:
        x_ref, g_ref, w1_ref, w2_ref, o_ref, xn_ref = refs
    j = pl.program_id(1)

    @pl.when(j == 0)
    def _():
        x = x_ref[...]
        xn_ref[...] = _rms(x, g_ref[...]).astype(BF16)
        o_ref[...] = x

    a = jnp.maximum(_dot(xn_ref[...], w1_ref[...]), 0.0)
    o_ref[...] += _dot((a * a).astype(BF16), w2_ref[...])

    if has_final:
        @pl.when(j == pl.num_programs(1) - 1)
        def _():
            o_ref[...] = _rms(o_ref[...], gf_ref[...])


def _mlp(h, g, w1, w2, layer, name, final_g=None):
    n, d = h.shape
    f = w1.shape[2]
    tm = _tile(n, 1024)
    tf = _tile(f, 512, V7X_LANES)
    has_final = final_g is not None
    vec = pl.BlockSpec((1, d), lambda i, j: (0, 0))
    in_specs = [pl.BlockSpec((tm, d), lambda i, j: (i, 0)), vec,
                pl.BlockSpec((None, d, tf), lambda i, j: (layer, 0, j)),
                pl.BlockSpec((None, tf, d), lambda i, j: (layer, j, 0))]
    args = [h, g.reshape(1, d), w1, w2]
    if has_final:
        in_specs.append(vec)
        args.append(final_g.reshape(1, d))
    vmem = (4 * _nbytes((tm, d), F32) + _nbytes((tm, d), BF16) + 4 * _nbytes((d, tf), w1.dtype)
            + 3 * _nbytes((tm, tf), F32) + _nbytes((tm, d), F32))
    return pl.pallas_call(
        functools.partial(_mlp_kernel, has_final),
        grid=(n // tm, f // tf),
        in_specs=in_specs,
        out_specs=pl.BlockSpec((tm, d), lambda i, j: (i, 0)),
        out_shape=jax.ShapeDtypeStruct((n, d), F32),
        scratch_shapes=[pltpu.VMEM((tm, d), BF16)],
        compiler_params=_params(("parallel", "arbitrary"), vmem),
        name=name,
    )(*args)


def _retention_tables(blk, dh):
    log_g = jnp.log1p(-jnp.exp2(-5.0 - jnp.arange(RET_HEADS, dtype=F32)))
    pos = jnp.arange(blk, dtype=F32)
    n, m = pos[:, None], pos[None, :]
    cn, cm = jnp.floor(n / STREAM_CHUNK), jnp.floor(m / STREAM_CHUNK)
    dist = jnp.where(cn == cm, jnp.abs(n - m), n - m)
    lg = log_g[:, None, None]
    dmask = jnp.where((cm <= cn)[None], jnp.exp(lg * dist[None]), 0.0)
    qdec = jnp.broadcast_to(jnp.exp(lg * (pos + 1.0)[None, :, None]), (RET_HEADS, blk, dh))
    kdec = jnp.broadcast_to(jnp.exp(lg * (blk - 1.0 - pos)[None, :, None]), (RET_HEADS, blk, dh))
    cdec = jnp.broadcast_to(jnp.exp(lg * blk), (RET_HEADS, 1, dh))
    return dmask, qdec, kdec, cdec


def _rope_tables(t, dh):
    inv = 1.0 / (ROPE_BASE ** (jnp.arange(0, dh, 2, dtype=F32) / dh))
    ang = jnp.arange(t, dtype=F32)[:, None] * inv[None, :]
    return jnp.cos(ang), jnp.sin(ang)


def _retention_kernel(q_ref, k_ref, v_ref, g_ref, cos_ref, sin_ref, dm_ref, qd_ref, kd_ref,
                      cd_ref, o_ref, s_ref):
    @pl.when(pl.program_id(2) == 0)
    def _():
        s_ref[...] = jnp.zeros_like(s_ref)

    dh = q_ref.shape[1]
    half = dh // 2
    cos, sin = cos_ref[...], sin_ref[...]

    def rot(t):
        t1, t2 = t[:, :half], t[:, half:]
        return jnp.concatenate([t1 * cos - t2 * sin, t2 * cos + t1 * sin], axis=-1)

    q = rot(q_ref[...])
    k = rot(k_ref[...]) * (dh ** -0.5)
    vb = v_ref[...].astype(BF16)
    scores = _dot_nt(q.astype(BF16), k.astype(BF16)) * dm_ref[0]
    state = s_ref[...]
    out = _dot(scores.astype(BF16), vb) + _dot((q * qd_ref[0]).astype(BF16), state.astype(BF16))
    s_ref[...] = state * cd_ref[0] + _dot_tn((k * kd_ref[0]).astype(BF16), vb)

    mu = jnp.mean(out, axis=-1, keepdims=True)
    cen = out - mu
    var = jnp.mean(cen * cen, axis=-1, keepdims=True)
    y = cen * lax.rsqrt(var + RET_GN_EPS)
    g = g_ref[...]
    o_ref[...] = (g * _sigmoid(g) * y).astype(BF16)


def _retention(z, b, t, ret_w, cos, sin):
    n = z.shape[0]
    dh = ret_w // RET_HEADS
    blk = _tile(t, 512, STREAM_CHUNK)
    nblk = t // blk
    dmask, qdec, kdec, cdec = _retention_tables(blk, dh)
    row = lambda bi, hi, ci: bi * nblk + ci
    col_spec = lambda off: pl.BlockSpec((blk, dh), lambda bi, hi, ci: (row(bi, hi, ci), off + hi))
    head_spec = lambda shape: pl.BlockSpec((1,) + shape, lambda bi, hi, ci: (hi, 0, 0))
    vmem = (2 * 4 * _nbytes((blk, dh), F32) + 4 * _nbytes((blk, dh // 2), F32)
            + 2 * _nbytes((blk, blk), F32) + 4 * _nbytes((blk, dh), F32) + _nbytes((dh, dh), F32)
            + 12 * _nbytes((blk, max(blk, dh)), F32))
    return pl.pallas_call(
        _retention_kernel,
        grid=(b, RET_HEADS, nblk),
        in_specs=[col_spec(0), col_spec(RET_HEADS), col_spec(2 * RET_HEADS), col_spec(3 * RET_HEADS),
                  pl.BlockSpec((blk, dh // 2), lambda bi, hi, ci: (ci, 0)),
                  pl.BlockSpec((blk, dh // 2), lambda bi, hi, ci: (ci, 0)),
                  head_spec((blk, blk)), head_spec((blk, dh)), head_spec((blk, dh)),
                  head_spec((1, dh))],
        out_specs=pl.BlockSpec((blk, dh), lambda bi, hi, ci: (row(bi, hi, ci), hi)),
        out_shape=jax.ShapeDtypeStruct((n, ret_w), BF16),
        scratch_shapes=[pltpu.VMEM((dh, dh), F32)],
        compiler_params=_params(("parallel", "parallel", "arbitrary"), vmem),
        name="retention",
    )(z, z, z, z, cos, sin, dmask, qdec, kdec, cdec)


def _rwkv_in_kernel(has_vres, tiles_per_seq, rw, ranks, windows, *refs):
    if has_vres:
        (x_ref, g_ref, w_ref, mu_ref, wlw_ref, wla_ref, wlg_ref, w0_ref, a0_ref, kk_ref, ka_ref,
         tri_ref, vf_ref, v0_ref, v1_ref, v2_ref,
         r_out, lw_out, c_out, k_out, v_out, p_out, a_out, g_out, xn_ref, z_ref, carry_ref) = refs
    else:
        (x_ref, g_ref, w_ref, mu_ref, wlw_ref, wla_ref, wlg_ref, w0_ref, a0_ref, kk_ref, ka_ref,
         tri_ref,
         r_out, lw_out, c_out, k_out, v_out, p_out, a_out, g_out, xn_ref, z_ref, carry_ref) = refs
    s = pl.program_id(0)
    n_tiles = pl.num_programs(0) - 1
    tm = z_ref.shape[0]
    at_start = jnp.maximum(s - 1, 0) % tiles_per_seq == 0
    row0 = lax.broadcasted_iota(jnp.int32, (tm, 1), 0) == 0

    @pl.when(s == 0)
    def _():
        z_ref[...] = jnp.zeros_like(z_ref)
        carry_ref[...] = jnp.zeros_like(carry_ref)

    @pl.when(s < n_tiles)
    def _():
        xn_ref[...] = _rms(x_ref[...], g_ref[...]).astype(BF16)

    def shift_mix_then_project(lo, hi):
        x = z_ref[:, lo:hi]
        last = jnp.where(at_start, 0.0, carry_ref[:, lo:hi])
        prev = jnp.where(row0, last, pltpu.roll(x, 1, 0))
        carry_ref[:, lo:hi] = x[tm - 1:tm, :]
        mixed = x + mu_ref[:, lo:hi] * (prev - x)
        z_ref[:, lo:hi] = _dot(xn_ref[...], w_ref[:, lo:hi])
        return mixed

    rank_w, rank_a = ranks
    lora = shift_mix_then_project(3 * rw, z_ref.shape[1])
    lane = lax.broadcasted_iota(jnp.int32, (1, lora.shape[1]), 1)
    feat = jnp.where(lane < rank_w, jnp.tanh(lora),
                     jnp.where(lane < rank_w + rank_a, lora, _sigmoid(lora))).astype(BF16)
    (w_lo, w_hi), (a_lo, a_hi), (g_lo, g_hi) = windows
    proj_w = _dot(feat[:, w_lo:w_hi], wlw_ref[...])
    proj_a = _dot(feat[:, a_lo:a_hi], wla_ref[...])
    gate = _dot(feat[:, g_lo:g_hi], wlg_ref[...])

    r = shift_mix_then_project(0, rw)
    log_w = -math.exp(-0.5) * _sigmoid(w0_ref[...] + proj_w)
    iclr = _sigmoid(a0_ref[...] + proj_a)

    vr = shift_mix_then_project(2 * rw, 3 * rw)
    if has_vres:
        low = _dot(vr.astype(BF16), v1_ref[...])
        mix = _sigmoid(v0_ref[...] + _dot(low.astype(BF16), v2_ref[...]))
        vr = vr + (vf_ref[...] - vr) * mix

    kr = shift_mix_then_project(rw, 2 * rw)
    kk = kr * kk_ref[...]
    pair = lax.broadcasted_iota(jnp.int32, (RWKV_PAIR, RWKV_PAIR), 0) // RWKV_HEAD_DIM
    ones_bd = (pair == pair.T).astype(BF16)
    sq = kk * kk
    norm = jnp.sqrt(jnp.concatenate(
        [_seg_sum(sq[:, i:i + RWKV_PAIR], ones_bd) for i in range(0, rw, RWKV_PAIR)], axis=1))
    kk = kk / jnp.maximum(norm, 1e-12)
    k2 = kr * (1.0 + (iclr - 1.0) * ka_ref[...])

    l_hi, l_mid, l_lo = _split3(log_w)
    tri = tri_ref[...]
    c_out[...] = _dot(tri, l_hi) + _dot(tri, l_mid) + _dot(tri, l_lo)

    r_out[...] = r
    lw_out[...] = log_w
    k_out[...] = k2
    v_out[...] = vr
    p_out[...] = kk
    a_out[...] = iclr
    g_out[...] = gate


def _lane_window(lo, hi):
    return (lo // V7X_LANES) * V7X_LANES, -(-hi // V7X_LANES) * V7X_LANES


def _rwkv_in(h, norm_g, w_in, t, ret_in, rw, mu, w2, a2, g2, w0, a0, k_k, k_a, v_first, v_res):
    n, d = h.shape
    has_vres = v_res is not None
    rank_w, rank_a, rank_g = w2.shape[0], a2.shape[0], g2.shape[0]
    lora = rank_w + rank_a + rank_g
    assert lora <= LORA_PAD and rw % RWKV_PAIR == 0 and w_in.shape[1] == ret_in + 3 * rw + lora
    width = 3 * rw + LORA_PAD
    tm = _tile(t, 256, RWKV_CHUNK)
    n_tiles = n // tm
    pos = jnp.arange(tm)
    tri = ((pos[:, None] // RWKV_CHUNK == pos[None, :] // RWKV_CHUNK)
           & (pos[:, None] >= pos[None, :])).astype(BF16)
    row2 = lambda v: v.reshape(1, -1)
    w_rw = jnp.pad(w_in[:, ret_in:], ((0, 0), (0, LORA_PAD - lora))).astype(BF16)
    mu_p = jnp.pad(mu, (0, LORA_PAD - lora)).reshape(1, width)
    bounds = [(0, rank_w), (rank_w, rank_w + rank_a), (rank_w + rank_a, lora)]
    windows = tuple(_lane_window(lo, hi) for lo, hi in bounds)

    def embed(w_low, bound, window):
        rows = jnp.zeros((window[1] - window[0], rw), F32)
        return rows.at[bound[0] - window[0]:bound[1] - window[0]].set(w_low).astype(BF16)

    wl = [embed(wg, bd, win) for wg, bd, win in zip((w2, a2, g2), bounds, windows)]
    once = pl.Buffered(1)
    vec = pl.BlockSpec((1, rw), lambda s: (0, 0), pipeline_mode=once)
    full = lambda a: pl.BlockSpec(a.shape, lambda s: (0,) * a.ndim, pipeline_mode=once)
    prev_tile = pl.BlockSpec((tm, rw), lambda s: (jnp.maximum(s - 1, 0), 0))
    args = [h, row2(norm_g), w_rw, mu_p, *wl, row2(w0), row2(a0), row2(k_k), row2(k_a), tri]
    specs = [pl.BlockSpec((tm, d), lambda s: (jnp.minimum(s, n_tiles - 1), 0)),
             full(row2(norm_g)), full(w_rw), full(mu_p), *[full(w) for w in wl],
             vec, vec, vec, vec, full(tri)]
    resident = _nbytes(w_rw.shape, BF16) + sum(_nbytes(w.shape, BF16) for w in wl) + _nbytes(tri.shape, BF16)
    if has_vres:
        v0, v1, v2 = v_res
        rank_v = v1.shape[1]
        rank_pad = _lane_window(0, rank_v)[1]
        v1p = jnp.zeros((rw, rank_pad), F32).at[:, :rank_v].set(v1).astype(BF16)
        v2p = jnp.zeros((rank_pad, rw), F32).at[:rank_v, :].set(v2).astype(BF16)
        args += [v_first, row2(v0), v1p, v2p]
        specs += [prev_tile, vec, full(v1p), full(v2p)]
        resident += 2 * _nbytes(v1p.shape, BF16)
    vmem = (resident + 2 * _nbytes((tm, d), F32) + 2 * (8 + has_vres) * _nbytes((tm, rw), F32)
            + _nbytes((tm, d), BF16) + _nbytes((tm, width), F32) + 10 * _nbytes((tm, rw), F32))
    return pl.pallas_call(
        functools.partial(_rwkv_in_kernel, has_vres, t // tm, rw, (rank_w, rank_a), windows),
        grid=(n_tiles + 1,),
        in_specs=specs,
        out_specs=[prev_tile] * 8,
        out_shape=[jax.ShapeDtypeStruct((n, rw), F32)] * 8,
        scratch_shapes=[pltpu.VMEM((tm, d), BF16), pltpu.VMEM((tm, width), F32),
                        pltpu.VMEM((1, width), F32)],
        compiler_params=_params(("arbitrary",), vmem),
        name="rwkv_in",
    )(*args)


def _rwkv_core_kernel(r_ref, lw_ref, c_ref, k_ref, v_ref, p_ref, a_ref, g_ref, rk_ref, lnw_ref,
                      lnb_ref, o_ref, h_ref, y_ref):
    @pl.when(pl.program_id(2) == 0)
    def _():
        h_ref[...] = jnp.zeros_like(h_ref)

    L = RWKV_CHUNK
    W = RWKV_PAIR
    hd = RWKV_HEAD_DIM
    n_chunks = r_ref.shape[0] // L
    n_pairs = r_ref.shape[1] // W

    ri = lax.broadcasted_iota(jnp.int32, (W, W), 0)
    ci = lax.broadcasted_iota(jnp.int32, (W, W), 1)
    same_head = (ri // hd) == (ci // hd)
    strict = same_head & ((ri % hd) > (ci % hd))
    incl = same_head & ((ri % hd) >= (ci % hd))
    eye = (ri == ci).astype(F32)
    merge_masks = []
    s = 1
    while s < hd:
        merge_masks.append(((ri // (2 * s)) == (ci // (2 * s))) & ((ri // s) != (ci // s)))
        s *= 2
    ones_bd = same_head.astype(BF16)
    bd_mask = (lax.broadcasted_iota(jnp.int32, (2 * L, W), 0) // L) == (
        lax.broadcasted_iota(jnp.int32, (2 * L, W), 1) // hd)

    def bd(x):
        return jnp.where(bd_mask, jnp.concatenate([x, x], axis=0), 0.0)

    def fold(x):
        return x[:L, :] + x[L:, :]

    def pair_chunk(sl, g):
        ln = slice(g * W, (g + 1) * W)
        r, lw, k, v, p = r_ref[sl, ln], lw_ref[sl, ln], k_ref[sl, ln], v_ref[sl, ln], p_ref[sl, ln]
        q = p * a_ref[sl, ln]
        c = c_ref[sl, ln]
        c_last = c[L - 1:L, :]
        e_in = jnp.exp(c)
        e_out = jnp.exp(-c)
        e_end = jnp.exp(c_last - c)
        rt, kt, qt = r * e_in, k * e_out, q * e_out
        pt = p * jnp.exp(c - lw)
        kh, qh = k * e_end, q * e_end

        pt_bd, rt_bd = bd(pt), bd(rt)
        lhs = jnp.concatenate([pt_bd, rt_bd], axis=0).astype(BF16)
        rhs = jnp.concatenate([bd(qt), bd(kt)], axis=0).astype(BF16)
        m = _dot_nt(lhs, rhs)
        yield
        a_pq = jnp.where(strict, m[:W, :W], 0.0)
        a_pk = jnp.where(strict, m[:W, W:], 0.0).astype(BF16)
        a_rq = jnp.where(incl, m[W:, :W], 0.0).astype(BF16)
        a_rk = jnp.where(incl, m[W:, W:], 0.0).astype(BF16)
        v_bd = bd(v).astype(BF16)
        pk_v = _dot(a_pk, v_bd)
        rk_v = _dot(a_rk, v_bd)
        yield

        t_inv = eye - jnp.where(merge_masks[0], a_pq, 0.0)
        for mask in merge_masks[1:]:
            tb = t_inv.astype(BF16)
            inner = _dot(jnp.where(mask, a_pq, 0.0).astype(BF16), tb)
            yield
            t_inv = t_inv - _dot(tb, inner.astype(BF16))
            yield
        t_inv = t_inv.astype(BF16)

        sol = _dot(t_inv, jnp.concatenate([pt_bd, pk_v], axis=1).astype(BF16)).astype(BF16)
        yield
        rq_sol = _dot(a_rq, sol)
        r_hat = fold(rt_bd - rq_sol[:, :W])
        y0 = fold(rk_v - rq_sol[:, W:])
        qh_sol = _dot_tn(bd(qh).astype(BF16), sol)
        g_mat = eye * jnp.exp(c_last) - qh_sol[:, :W]
        h_add = _dot_tn(bd(kh).astype(BF16), v_bd) - qh_sol[:, W:]
        yield

        h_hi, h_lo = _split2(h_ref[g])
        y_ref[sl, ln] = _dot(r_hat.astype(BF16), h_hi) + y0
        g_hi, g_lo = _split2(g_mat)
        gh = _dot(g_hi, jnp.concatenate([h_hi, h_lo], axis=1))
        h_ref[g] = gh[:, :W] + gh[:, W:] + _dot(g_lo, h_hi) + h_add

    chunks_per_step = 2 if n_chunks % 2 == 0 else 1

    def chunk_group(i, carry):
        live = []
        for j in range(chunks_per_step):
            sl = pl.ds(pl.multiple_of((i * chunks_per_step + j) * L, L), L)
            live += [pair_chunk(sl, g) for g in range(n_pairs)]
        while live:
            live = [gen for gen in live if next(gen, True) is None]
        return carry

    lax.fori_loop(0, n_chunks // chunks_per_step, chunk_group, 0)

    inv_hd = 1.0 / hd
    lanes = [slice(g * W, (g + 1) * W) for g in range(n_pairs)]
    ys = [y_ref[:, ln] for ln in lanes]
    mus = [_seg_sum(y, ones_bd) * inv_hd for y in ys]
    cens = [y - mu for y, mu in zip(ys, mus)]
    bonus_sums = [_seg_sum(r_ref[:, ln] * k_ref[:, ln] * rk_ref[:, ln], ones_bd) for ln in lanes]
    variances = [_seg_sum(cen * cen, ones_bd) * inv_hd for cen in cens]
    for ln, cen, var, bsum in zip(lanes, cens, variances, bonus_sums):
        yn = cen * lax.rsqrt(var + RWKV_LN_EPS) * lnw_ref[:, ln] + lnb_ref[:, ln]
        o_ref[:, ln] = ((yn + bsum * v_ref[:, ln]) * g_ref[:, ln]).astype(BF16)


def _rwkv_core(r, lw, c, k, v, p, a, gate, r_k, lnx_w, lnx_b, b, t):
    n, rw = r.shape
    assert rw % RWKV_PAIR == 0 and t % RWKV_CHUNK == 0
    group = _tile(rw, RWKV_GROUP_LANES, RWKV_PAIR)
    blk = _tile(t, 256, RWKV_CHUNK)
    nblk = t // blk
    tok = pl.BlockSpec((blk, group), lambda bi, pi, ci: (bi * nblk + ci, pi))
    vec = pl.BlockSpec((1, group), lambda bi, pi, ci: (0, pi))
    row2 = lambda x: x.reshape(1, rw)
    vmem = (2 * 8 * _nbytes((blk, group), F32) + 2 * _nbytes((blk, group), BF16)
            + _nbytes((blk, group), F32)
            + 64 * (group // RWKV_PAIR) * _nbytes((2 * RWKV_PAIR, 2 * RWKV_PAIR), F32)
            + 8 * _nbytes((blk, group), F32))
    return pl.pallas_call(
        _rwkv_core_kernel,
        grid=(b, rw // group, nblk),
        in_specs=[tok] * 8 + [vec] * 3,
        out_specs=tok,
        out_shape=jax.ShapeDtypeStruct((n, rw), BF16),
        scratch_shapes=[pltpu.VMEM((group // RWKV_PAIR, RWKV_PAIR, RWKV_PAIR), F32),
                        pltpu.VMEM((blk, group), F32)],
        compiler_params=_params(("parallel", "parallel", "arbitrary"), vmem),
        name="rwkv_core",
    )(r, lw, c, k, v, p, a, gate, row2(r_k), row2(lnx_w), row2(lnx_b))


def _lru_conv_gates(x, tail_prev, cw, cb, gxw, gxb, gaw, gab):
    lt, tail = x.shape[0], tail_prev.shape[0]
    ext = jnp.concatenate([tail_prev, x], axis=0)
    xc = cb + cw[CONV_WIDTH - 1:CONV_WIDTH, :] * x
    for j in range(1, CONV_WIDTH):
        xc = xc + cw[CONV_WIDTH - 1 - j:CONV_WIDTH - j, :] * pltpu.roll(ext, j, 0)[tail:, :]
    xcb = xc.astype(BF16)
    return xc, _dot(xcb, gxw) + gxb, _dot(xcb, gaw) + gab, x[lt - tail:, :]


def _lru_scan(xc, pre_x, pre_a, lam, h_prev):
    lt = xc.shape[0]
    gate_x = _sigmoid(pre_x)
    gate_a = _sigmoid(pre_a)
    neg_lam = -lam
    softplus = jnp.maximum(neg_lam, 0.0) + jnp.log(1.0 + jnp.exp(-jnp.abs(neg_lam)))
    log_a = -LRU_C * gate_a * softplus
    a = jnp.exp(log_a)
    bb = xc * gate_x * jnp.sqrt(-jnp.tanh(log_a) * (a * a + 1.0))

    sub = V7X_SUBLANES
    n_groups = lt // sub
    a = a.reshape(n_groups, sub, a.shape[1])
    bb = bb.reshape(n_groups, sub, bb.shape[1])
    row_in_group = lax.broadcasted_iota(jnp.int32, (1, sub, 1), 1)
    s = 1
    while s < sub:
        valid = row_in_group >= s
        b_prev = jnp.where(valid, pltpu.roll(bb, s, 1), 0.0)
        a_prev = jnp.where(valid, pltpu.roll(a, s, 1), 1.0)
        bb = bb + a * b_prev
        a = a * a_prev
        s *= 2
    groups = []
    for gi in range(n_groups):
        hg = a[gi] * h_prev + bb[gi]
        groups.append(hg)
        h_prev = hg[sub - 1:sub, :]
    return jnp.concatenate(groups, axis=0), h_prev


def _odd_mixer_kernel(tiles_per_seq, x_ref, g_ref, w_ref, cw_ref, cb_ref, gxw_ref, gxb_ref,
                      gaw_ref, gab_ref, lam_ref, o_ref, xn_ref, zy_ref, zx_ref, xtail_ref, h_ref):
    s = pl.program_id(0)
    n_tiles = pl.num_programs(0) - 1
    d = zy_ref.shape[1]
    seq_start = jnp.maximum(s - 1, 0) % tiles_per_seq == 0

    @pl.when(s == 0)
    def _():
        zy_ref[...] = jnp.zeros_like(zy_ref)
        zx_ref[...] = jnp.zeros_like(zx_ref)
        xtail_ref[...] = jnp.zeros_like(xtail_ref)
        h_ref[...] = jnp.zeros_like(h_ref)

    @pl.when(s < n_tiles)
    def _():
        xn_ref[...] = _rms(x_ref[...], g_ref[...]).astype(BF16)

    for blk in range(d // LRU_BLOCK):
        cols = slice(blk * LRU_BLOCK, (blk + 1) * LRU_BLOCK)
        xcols = slice(d + blk * LRU_BLOCK, d + (blk + 1) * LRU_BLOCK)
        tail_prev = jnp.where(seq_start, 0.0, xtail_ref[:, cols])
        h_prev = jnp.where(seq_start, 0.0, h_ref[:, cols])
        y = zy_ref[:, cols]
        xc, pre_x, pre_a, tail_new = _lru_conv_gates(
            zx_ref[:, cols], tail_prev, cw_ref[:, cols], cb_ref[:, cols], gxw_ref[blk],
            gxb_ref[:, cols], gaw_ref[blk], gab_ref[:, cols])
        gelu = 0.5 * y * (1.0 + jnp.tanh(math.sqrt(2.0 / math.pi) * (y + 0.044715 * (y * y * y))))
        xn = xn_ref[...]
        zy_ref[:, cols] = _dot(xn, w_ref[:, cols])
        zx_ref[:, cols] = _dot(xn, w_ref[:, xcols])
        hs, h_new = _lru_scan(xc, pre_x, pre_a, lam_ref[:, cols], h_prev)
        o_ref[:, cols] = (gelu * hs).astype(BF16)
        xtail_ref[:, cols] = tail_new
        h_ref[:, cols] = h_new


def _odd_mixer(h, norm_g, w_in_bf16, layer, t, conv_w, conv_b, gx_w, gx_b, ga_w, ga_b, lam):
    n, d = h.shape
    assert d % LRU_BLOCK == 0 and w_in_bf16.shape[2] == 2 * d
    nb = d // LRU_BLOCK
    tm = _tile(t, 256)
    n_tiles = n // tm
    tail = V7X_SUBLANES
    assert CONV_WIDTH - 1 <= tail <= tm
    once = pl.Buffered(1)
    vec = lambda rows: pl.BlockSpec((rows, d), lambda s: (0, 0), pipeline_mode=once)
    wsp = pl.BlockSpec((nb, LRU_BLOCK, LRU_BLOCK), lambda s: (0, 0, 0), pipeline_mode=once)
    row2 = lambda x: x.reshape(1, d)
    vmem = (_nbytes((d, 2 * d), BF16) + 2 * _nbytes((nb, LRU_BLOCK, LRU_BLOCK), BF16)
            + 2 * _nbytes((tm, d), F32) + _nbytes((tm, d), BF16) + 2 * _nbytes((tm, d), BF16)
            + 2 * _nbytes((tm, d), F32) + 2 * _nbytes((tm, d), F32)
            + 28 * _nbytes((tm, LRU_BLOCK), F32))
    return pl.pallas_call(
        functools.partial(_odd_mixer_kernel, t // tm),
        grid=(n_tiles + 1,),
        in_specs=[pl.BlockSpec((tm, d), lambda s: (jnp.minimum(s, n_tiles - 1), 0)),
                  vec(1),
                  pl.BlockSpec((None, d, 2 * d), lambda s: (layer, 0, 0), pipeline_mode=once),
                  vec(CONV_WIDTH), vec(1), wsp, vec(1), wsp, vec(1), vec(1)],
        out_specs=pl.BlockSpec((tm, d), lambda s: (jnp.maximum(s - 1, 0), 0)),
        out_shape=jax.ShapeDtypeStruct((n, d), BF16),
        scratch_shapes=[pltpu.VMEM((tm, d), BF16), pltpu.VMEM((tm, d), F32), pltpu.VMEM((tm, d), F32),
                        pltpu.VMEM((tail, d), F32), pltpu.VMEM((1, d), F32)],
        compiler_params=_params(("arbitrary",), vmem),
        name="odd_mixer",
    )(h, norm_g.reshape(1, d), w_in_bf16, conv_w, row2(conv_b), gx_w.astype(BF16), row2(gx_b),
      ga_w.astype(BF16), row2(ga_b), row2(lam))


def _even_layer(h, b, t, e, norm_g, w_in_all, w_out_all, mu, w0, w2, a0, a2, g2, k_k, k_a, r_k, lnx_w,
                lnx_b, v_first, v_res, rope):
    d = h.shape[1]
    ret_w = d // 2
    rw = d - ret_w
    ret_in = 4 * ret_w
    z_ret = _norm_matmul(h, norm_g, w_in_all, e, ret_in, "even_in_proj")
    out_ret = _retention(z_ret, b, t, ret_w, *rope)
    r, lw, c, k, v, p, a, gate = _rwkv_in(h, norm_g, w_in_all[e], t, ret_in, rw, mu, w2, a2, g2, w0, a0,
                                          k_k, k_a, v_first, v_res)
    out_rw = _rwkv_core(r, lw, c, k, v, p, a, gate, r_k.reshape(-1), lnx_w, lnx_b, b, t)
    h = _matmul_residual([out_ret, out_rw], w_out_all, e, h, "even_out_proj")
    return h, (v if v_res is None else v_first)


def _odd_layer(h, t, o, norm_g, w_in_b, conv_w, conv_b, gx_w, gx_b, ga_w, ga_b, lam, w_out_b):
    gated = _odd_mixer(h, norm_g, w_in_b, o, t, conv_w, conv_b, gx_w, gx_b, ga_w, ga_b, lam)
    return _matmul_residual([gated], w_out_b, o, h, "odd_out_proj")


def kernel(x, ev_norm, ev_w_in, ev_w_out, rw_mu, rw_w0, rw_w2, rw_a0, rw_a2, rw_g2, rw_k_k, rw_k_a, rw_r_k, rw_lnx_w, rw_lnx_b, rw_v0, rw_v1, rw_v2, od_norm, od_w_in, od_conv_w, od_conv_b, od_gx_w, od_gx_b, od_ga_w, od_ga_b, od_lam, od_w_out, ff_norm, ff_w1, ff_w2, final_norm):
    b, t, d = x.shape
    depth = ff_norm.shape[0]
    h = x.reshape(b * t, d)
    rope = _rope_tables(t, (d // 2) // RET_HEADS)
    ff_w1_b, ff_w2_b = ff_w1.astype(BF16), ff_w2.astype(BF16)
    od_w_in_b, od_w_out_b, ev_w_out_b = od_w_in.astype(BF16), od_w_out.astype(BF16), ev_w_out.astype(BF16)
    v_first = None
    for layer in range(depth):
        if layer % 2 == 0:
            e = layer // 2
            v_res = None if e == 0 else (rw_v0[e - 1], rw_v1[e - 1], rw_v2[e - 1])
            h, v_first = _even_layer(
                h, b, t, e, ev_norm[e], ev_w_in, ev_w_out_b, rw_mu[e], rw_w0[e], rw_w2[e],
                rw_a0[e], rw_a2[e], rw_g2[e], rw_k_k[e], rw_k_a[e], rw_r_k[e], rw_lnx_w[e],
                rw_lnx_b[e], v_first, v_res, rope)
        else:
            o = layer // 2
            h = _odd_layer(h, t, o, od_norm[o], od_w_in_b, od_conv_w[o], od_conv_b[o], od_gx_w[o],
                           od_gx_b[o], od_ga_w[o], od_ga_b[o], od_lam[o], od_w_out_b)
        h = _mlp(h, ff_norm[layer], ff_w1_b, ff_w2_b, layer, f"mlp_{layer}",
                 final_g=final_norm if layer == depth - 1 else None)
    return h.reshape(b, t, d)
```

```python
import functools
import math

import jax
import jax.numpy as jnp
from jax import lax
from jax.experimental import pallas as pl
from jax.experimental.pallas import tpu as pltpu

F32 = jnp.float32
BF16 = jnp.bfloat16

NORM_EPS = 1e-6
RET_HEADS = 4
RET_GN_EPS = 1e-5
ROPE_BASE = 10000.0
STREAM_CHUNK = 64
RWKV_HEAD_DIM = 64
RWKV_LN_EPS = 64e-5
LRU_BLOCK = 256
CONV_WIDTH = 4
LRU_C = 8.0

V7X_LANES = 128
V7X_SUBLANES = 8
V7X_VMEM_BYTES = 64 * 1024 * 1024
VMEM_CAP_BYTES = V7X_VMEM_BYTES - 8 * 1024 * 1024
VMEM_FLOOR_BYTES = 16 * 1024 * 1024

RWKV_CHUNK = 64
RWKV_PAIR = 2 * RWKV_HEAD_DIM
RWKV_GROUP_LANES = 8 * RWKV_PAIR
LORA_PAD = 512


def _tile(n, pref, mult=V7X_SUBLANES):
    if n <= pref:
        return n
    t = (pref // mult) * mult
    while t >= mult:
        if n % t == 0:
            return t
        t -= mult
    raise ValueError(f"no tile for {n} <= {pref}")


def _params(semantics, vmem_bytes):
    limit = int(min(max(vmem_bytes, VMEM_FLOOR_BYTES), VMEM_CAP_BYTES))
    return pltpu.CompilerParams(dimension_semantics=semantics, vmem_limit_bytes=limit)


def _nbytes(shape, dtype):
    return math.prod(shape) * jnp.dtype(dtype).itemsize


def _dot(a, b):
    return jnp.dot(a, b, preferred_element_type=F32)


def _dot_nt(a, b):
    return lax.dot_general(a, b, (((1,), (1,)), ((), ())), preferred_element_type=F32)


def _dot_tn(a, b):
    return lax.dot_general(a, b, (((0,), (0,)), ((), ())), preferred_element_type=F32)


def _split2(x):
    hi = x.astype(BF16)
    lo = (x - hi.astype(F32)).astype(BF16)
    return hi, lo


def _split3(x):
    hi = x.astype(BF16)
    r1 = x - hi.astype(F32)
    mid = r1.astype(BF16)
    lo = (r1 - mid.astype(F32)).astype(BF16)
    return hi, mid, lo


def _seg_sum(x, ones_bd):
    hi, lo = _split2(x)
    return _dot(hi, ones_bd) + _dot(lo, ones_bd)


def _rms(x, g):
    ms = jnp.mean(x * x, axis=-1, keepdims=True)
    return x * lax.rsqrt(ms + NORM_EPS) * g


def _sigmoid(x):
    return jax.nn.sigmoid(x)


def _matmul_residual_kernel(n_terms, *refs):
    x_refs, w_refs = refs[:n_terms], refs[n_terms:2 * n_terms]
    r_ref, o_ref = refs[2 * n_terms:]
    acc = r_ref[...]
    for x_ref, w_ref in zip(x_refs, w_refs):
        acc = acc + _dot(x_ref[...], w_ref[...])
    o_ref[...] = acc


def _matmul_residual(xs_bf16, w_bf16, layer, res, name):
    n, m = res.shape
    k = xs_bf16[0].shape[1]
    assert all(x.shape[1] == k for x in xs_bf16) and w_bf16.shape[1] == k * len(xs_bf16)
    tm = _tile(n, 512)
    ksum = w_bf16.shape[1]
    vmem = (2 * _nbytes((tm, ksum), BF16) + _nbytes((ksum, m), BF16)
            + (5 + len(xs_bf16)) * _nbytes((tm, m), F32))
    x_specs = [pl.BlockSpec((tm, k), lambda i: (i, 0)) for _ in xs_bf16]
    w_specs = [pl.BlockSpec((None, k, m), functools.partial(lambda i, r: (layer, r, 0), r=r),
                            pipeline_mode=pl.Buffered(1))
               for r in range(len(xs_bf16))]
    return pl.pallas_call(
        functools.partial(_matmul_residual_kernel, len(xs_bf16)),
        grid=(n // tm,),
        in_specs=x_specs + w_specs + [pl.BlockSpec((tm, m), lambda i: (i, 0))],
        out_specs=pl.BlockSpec((tm, m), lambda i: (i, 0)),
        out_shape=jax.ShapeDtypeStruct((n, m), F32),
        compiler_params=_params(("parallel",), vmem),
        name=name,
    )(*xs_bf16, *([w_bf16] * len(xs_bf16)), res)


def _mlp_kernel(has_final, *refs):
    if has_final:
        x_ref, g_ref, w1_ref, w2_ref, gf_ref, o_ref, xn_ref = refs
    else:
        x_ref, g_ref, w1_ref, w2_ref, o_ref, xn_ref = refs
    j = pl.program_id(1)

    @pl.when(j == 0)
    def _():
        x = x_ref[...]
        xn_ref[...] = _rms(x, g_ref[...]).astype(BF16)
        o_ref[...] = x

    a = jnp.maximum(_dot(xn_ref[...], w1_ref[...]), 0.0)
    o_ref[...] += _dot((a * a).astype(BF16), w2_ref[...])

    if has_final:
        @pl.when(j == pl.num_programs(1) - 1)
        def _():
            o_ref[...] = _rms(o_ref[...], gf_ref[...])


def _mlp(h, g, w1, w2, layer, name, final_g=None):
    n, d = h.shape
    f = w1.shape[2]
    tm = _tile(n, 1024)
    tf = _tile(f, 512, V7X_LANES)
    has_final = final_g is not None
    vec = pl.BlockSpec((1, d), lambda i, j: (0, 0))
    in_specs = [pl.BlockSpec((tm, d), lambda i, j: (i, 0)), vec,
                pl.BlockSpec((None, d, tf), lambda i, j: (layer, 0, j)),
                pl.BlockSpec((None, tf, d), lambda i, j: (layer, j, 0))]
    args = [h, g.reshape(1, d), w1, w2]
    if has_final:
        in_specs.append(vec)
        args.append(final_g.reshape(1, d))
    vmem = (4 * _nbytes((tm, d), F32) + _nbytes((tm, d), BF16) + 4 * _nbytes((d, tf), w1.dtype)
            + 3 * _nbytes((tm, tf), F32) + _nbytes((tm, d), F32))
    return pl.pallas_call(
        functools.partial(_mlp_kernel, has_final),
        grid=(n // tm, f // tf),
        in_specs=in_specs,
        out_specs=pl.BlockSpec((tm, d), lambda i, j: (i, 0)),
        out_shape=jax.ShapeDtypeStruct((n, d), F32),
        scratch_shapes=[pltpu.VMEM((tm, d), BF16)],
        compiler_params=_params(("parallel", "arbitrary"), vmem),
        name=name,
    )(*args)


def _retention_tables(blk, dh):
    log_g = jnp.log1p(-jnp.exp2(-5.0 - jnp.arange(RET_HEADS, dtype=F32)))
    pos = jnp.arange(blk, dtype=F32)
    n, m = pos[:, None], pos[None, :]
    cn, cm = jnp.floor(n / STREAM_CHUNK), jnp.floor(m / STREAM_CHUNK)
    dist = jnp.where(cn == cm, jnp.abs(n - m), n - m)
    lg = log_g[:, None, None]
    dmask = jnp.where((cm <= cn)[None], jnp.exp(lg * dist[None]), 0.0)
    qdec = jnp.broadcast_to(jnp.exp(lg * (pos + 1.0)[None, :, None]), (RET_HEADS, blk, dh))
    kdec = jnp.broadcast_to(jnp.exp(lg * (blk - 1.0 - pos)[None, :, None]), (RET_HEADS, blk, dh))
    cdec = jnp.broadcast_to(jnp.exp(lg * blk), (RET_HEADS, 1, dh))
    return dmask, qdec, kdec, cdec


def _rope_tables(t, dh):
    inv = 1.0 / (ROPE_BASE ** (jnp.arange(0, dh, 2, dtype=F32) / dh))
    ang = jnp.arange(t, dtype=F32)[:, None] * inv[None, :]
    return jnp.cos(ang), jnp.sin(ang)


def _retention_kernel(tiles_per_seq, x_ref, g_ref, w_ref, cos_ref, sin_ref, dm_ref, qd_ref, kd_ref,
                      cd_ref, o_ref, xn_ref, z_ref, s_ref):
    s = pl.program_id(0)
    n_tiles = pl.num_programs(0) - 1
    ret_w = o_ref.shape[1]
    dh = ret_w // RET_HEADS
    half = dh // 2
    seq_start = jnp.maximum(s - 1, 0) % tiles_per_seq == 0

    @pl.when(s == 0)
    def _():
        z_ref[...] = jnp.zeros_like(z_ref)
        s_ref[...] = jnp.zeros_like(s_ref)

    @pl.when(s < n_tiles)
    def _():
        xn_ref[...] = _rms(x_ref[...], g_ref[...]).astype(BF16)

    cos, sin = cos_ref[...], sin_ref[...]

    def rot(t):
        t1, t2 = t[:, :half], t[:, half:]
        return jnp.concatenate([t1 * cos - t2 * sin, t2 * cos + t1 * sin], axis=-1)

    for head in range(RET_HEADS):
        q_cols, k_cols, v_cols, g_cols = [
            slice(part * ret_w + head * dh, part * ret_w + (head + 1) * dh) for part in range(4)]
        q = rot(z_ref[:, q_cols])
        k = rot(z_ref[:, k_cols]) * (dh ** -0.5)
        vb = z_ref[:, v_cols].astype(BF16)
        gate = z_ref[:, g_cols]
        scores = _dot_nt(q.astype(BF16), k.astype(BF16))
        xn = xn_ref[...]
        for cols in (q_cols, k_cols, v_cols, g_cols):
            z_ref[:, cols] = _dot(xn, w_ref[:, cols])
        scores = scores * dm_ref[head]
        state = jnp.where(seq_start, 0.0, s_ref[head])
        out = _dot(scores.astype(BF16), vb) + _dot((q * qd_ref[head]).astype(BF16), state.astype(BF16))
        s_ref[head] = state * cd_ref[head] + _dot_tn((k * kd_ref[head]).astype(BF16), vb)

        mu = jnp.mean(out, axis=-1, keepdims=True)
        cen = out - mu
        var = jnp.mean(cen * cen, axis=-1, keepdims=True)
        y = cen * lax.rsqrt(var + RET_GN_EPS)
        o_ref[:, head * dh:(head + 1) * dh] = (gate * _sigmoid(gate) * y).astype(BF16)


def _retention(h, norm_g, w_ret_bf16, t, ret_w, cos, sin):
    n, d = h.shape
    dh = ret_w // RET_HEADS
    assert w_ret_bf16.shape == (d, 4 * ret_w)
    tm = _tile(t, 256, STREAM_CHUNK)
    n_tiles = n // tm
    tiles_per_seq = t // tm
    tables = _retention_tables(tm, dh)
    once = pl.Buffered(1)
    full = lambda a: pl.BlockSpec(a.shape, lambda s: (0,) * a.ndim, pipeline_mode=once)
    rope_spec = pl.BlockSpec((tm, dh // 2), lambda s: (jnp.maximum(s - 1, 0) % tiles_per_seq, 0))
    g2 = norm_g.reshape(1, d)
    vmem = (_nbytes(w_ret_bf16.shape, BF16) + sum(_nbytes(a.shape, F32) for a in tables)
            + 2 * _nbytes((tm, d), F32) + _nbytes((tm, d), BF16) + 2 * _nbytes((tm, ret_w), BF16)
            + 4 * _nbytes((tm, dh // 2), F32) + _nbytes((tm, 4 * ret_w), F32)
            + _nbytes((RET_HEADS, dh, dh), F32) + 2 * _nbytes((tm, d), F32)
            + 16 * _nbytes((tm, max(tm, dh)), F32))
    return pl.pallas_call(
        functools.partial(_retention_kernel, tiles_per_seq),
        grid=(n_tiles + 1,),
        in_specs=[pl.BlockSpec((tm, d), lambda s: (jnp.minimum(s, n_tiles - 1), 0)),
                  full(g2), full(w_ret_bf16), rope_spec, rope_spec] + [full(a) for a in tables],
        out_specs=pl.BlockSpec((tm, ret_w), lambda s: (jnp.maximum(s - 1, 0), 0)),
        out_shape=jax.ShapeDtypeStruct((n, ret_w), BF16),
        scratch_shapes=[pltpu.VMEM((tm, d), BF16), pltpu.VMEM((tm, 4 * ret_w), F32),
                        pltpu.VMEM((RET_HEADS, dh, dh), F32)],
        compiler_params=_params(("arbitrary",), vmem),
        name="retention",
    )(h, g2, w_ret_bf16, cos, sin, *tables)


def _rwkv_in_kernel(has_vres, tiles_per_seq, rw, ranks, windows, *refs):
    if has_vres:
        (x_ref, g_ref, w_ref, mu_ref, wlw_ref, wla_ref, wlg_ref, w0_ref, a0_ref, kk_ref, ka_ref,
         tri_ref, vf_ref, v0_ref, v1_ref, v2_ref,
         r_out, lw_out, c_out, k_out, v_out, p_out, a_out, g_out, xn_ref, z_ref, carry_ref) = refs
    else:
        (x_ref, g_ref, w_ref, mu_ref, wlw_ref, wla_ref, wlg_ref, w0_ref, a0_ref, kk_ref, ka_ref,
         tri_ref,
         r_out, lw_out, c_out, k_out, v_out, p_out, a_out, g_out, xn_ref, z_ref, carry_ref) = refs
    s = pl.program_id(0)
    n_tiles = pl.num_programs(0) - 1
    tm = z_ref.shape[0]
    at_start = jnp.maximum(s - 1, 0) % tiles_per_seq == 0
    row0 = lax.broadcasted_iota(jnp.int32, (tm, 1), 0) == 0

    @pl.when(s == 0)
    def _():
        z_ref[...] = jnp.zeros_like(z_ref)
        carry_ref[...] = jnp.zeros_like(carry_ref)

    @pl.when(s < n_tiles)
    def _():
        xn_ref[...] = _rms(x_ref[...], g_ref[...]).astype(BF16)

    def shift_mix_then_project(lo, hi):
        x = z_ref[:, lo:hi]
        last = jnp.where(at_start, 0.0, carry_ref[:, lo:hi])
        prev = jnp.where(row0, last, pltpu.roll(x, 1, 0))
        carry_ref[:, lo:hi] = x[tm - 1:tm, :]
        mixed = x + mu_ref[:, lo:hi] * (prev - x)
        z_ref[:, lo:hi] = _dot(xn_ref[...], w_ref[:, lo:hi])
        return mixed

    rank_w, rank_a = ranks
    lora = shift_mix_then_project(3 * rw, z_ref.shape[1])
    lane = lax.broadcasted_iota(jnp.int32, (1, lora.shape[1]), 1)
    feat = jnp.where(lane < rank_w, jnp.tanh(lora),
                     jnp.where(lane < rank_w + rank_a, lora, _sigmoid(lora))).astype(BF16)
    (w_lo, w_hi), (a_lo, a_hi), (g_lo, g_hi) = windows
    proj_w = _dot(feat[:, w_lo:w_hi], wlw_ref[...])
    proj_a = _dot(feat[:, a_lo:a_hi], wla_ref[...])
    gate = _dot(feat[:, g_lo:g_hi], wlg_ref[...])

    r = shift_mix_then_project(0, rw)
    log_w = -math.exp(-0.5) * _sigmoid(w0_ref[...] + proj_w)
    iclr = _sigmoid(a0_ref[...] + proj_a)

    vr = shift_mix_then_project(2 * rw, 3 * rw)
    if has_vres:
        low = _dot(vr.astype(BF16), v1_ref[...])
        mix = _sigmoid(v0_ref[...] + _dot(low.astype(BF16), v2_ref[...]))
        vr = vr + (vf_ref[...] - vr) * mix

    kr = shift_mix_then_project(rw, 2 * rw)
    kk = kr * kk_ref[...]
    pair = lax.broadcasted_iota(jnp.int32, (RWKV_PAIR, RWKV_PAIR), 0) // RWKV_HEAD_DIM
    ones_bd = (pair == pair.T).astype(BF16)
    sq = kk * kk
    norm = jnp.sqrt(jnp.concatenate(
        [_seg_sum(sq[:, i:i + RWKV_PAIR], ones_bd) for i in range(0, rw, RWKV_PAIR)], axis=1))
    kk = kk / jnp.maximum(norm, 1e-12)
    k2 = kr * (1.0 + (iclr - 1.0) * ka_ref[...])

    l_hi, l_mid, l_lo = _split3(log_w)
    tri = tri_ref[...]
    c_out[...] = _dot(tri, l_hi) + _dot(tri, l_mid) + _dot(tri, l_lo)

    r_out[...] = r
    lw_out[...] = log_w
    k_out[...] = k2
    v_out[...] = vr
    p_out[...] = kk
    a_out[...] = iclr
    g_out[...] = gate


def _lane_window(lo, hi):
    return (lo // V7X_LANES) * V7X_LANES, -(-hi // V7X_LANES) * V7X_LANES


def _rwkv_in(h, norm_g, w_in, t, ret_in, rw, mu, w2, a2, g2, w0, a0, k_k, k_a, v_first, v_res):
    n, d = h.shape
    has_vres = v_res is not None
    rank_w, rank_a, rank_g = w2.shape[0], a2.shape[0], g2.shape[0]
    lora = rank_w + rank_a + rank_g
    assert lora <= LORA_PAD and rw % RWKV_PAIR == 0 and w_in.shape[1] == ret_in + 3 * rw + lora
    width = 3 * rw + LORA_PAD
    tm = _tile(t, 256, RWKV_CHUNK)
    n_tiles = n // tm
    pos = jnp.arange(tm)
    tri = ((pos[:, None] // RWKV_CHUNK == pos[None, :] // RWKV_CHUNK)
           & (pos[:, None] >= pos[None, :])).astype(BF16)
    row2 = lambda v: v.reshape(1, -1)
    w_rw = jnp.pad(w_in[:, ret_in:], ((0, 0), (0, LORA_PAD - lora))).astype(BF16)
    mu_p = jnp.pad(mu, (0, LORA_PAD - lora)).reshape(1, width)
    bounds = [(0, rank_w), (rank_w, rank_w + rank_a), (rank_w + rank_a, lora)]
    windows = tuple(_lane_window(lo, hi) for lo, hi in bounds)

    def embed(w_low, bound, window):
        rows = jnp.zeros((window[1] - window[0], rw), F32)
        return rows.at[bound[0] - window[0]:bound[1] - window[0]].set(w_low).astype(BF16)

    wl = [embed(wg, bd, win) for wg, bd, win in zip((w2, a2, g2), bounds, windows)]
    once = pl.Buffered(1)
    vec = pl.BlockSpec((1, rw), lambda s: (0, 0), pipeline_mode=once)
    full = lambda a: pl.BlockSpec(a.shape, lambda s: (0,) * a.ndim, pipeline_mode=once)
    prev_tile = pl.BlockSpec((tm, rw), lambda s: (jnp.maximum(s - 1, 0), 0))
    args = [h, row2(norm_g), w_rw, mu_p, *wl, row2(w0), row2(a0), row2(k_k), row2(k_a), tri]
    specs = [pl.BlockSpec((tm, d), lambda s: (jnp.minimum(s, n_tiles - 1), 0)),
             full(row2(norm_g)), full(w_rw), full(mu_p), *[full(w) for w in wl],
             vec, vec, vec, vec, full(tri)]
    resident = _nbytes(w_rw.shape, BF16) + sum(_nbytes(w.shape, BF16) for w in wl) + _nbytes(tri.shape, BF16)
    if has_vres:
        v0, v1, v2 = v_res
        rank_v = v1.shape[1]
        rank_pad = _lane_window(0, rank_v)[1]
        v1p = jnp.zeros((rw, rank_pad), F32).at[:, :rank_v].set(v1).astype(BF16)
        v2p = jnp.zeros((rank_pad, rw), F32).at[:rank_v, :].set(v2).astype(BF16)
        args += [v_first, row2(v0), v1p, v2p]
        specs += [prev_tile, vec, full(v1p), full(v2p)]
        resident += 2 * _nbytes(v1p.shape, BF16)
    vmem = (resident + 2 * _nbytes((tm, d), F32) + 2 * (8 + has_vres) * _nbytes((tm, rw), F32)
            + _nbytes((tm, d), BF16) + _nbytes((tm, width), F32) + 10 * _nbytes((tm, rw), F32))
    return pl.pallas_call(
        functools.partial(_rwkv_in_kernel, has_vres, t // tm, rw, (rank_w, rank_a), windows),
        grid=(n_tiles + 1,),
        in_specs=specs,
        out_specs=[prev_tile] * 8,
        out_shape=[jax.ShapeDtypeStruct((n, rw), F32)] * 8,
        scratch_shapes=[pltpu.VMEM((tm, d), BF16), pltpu.VMEM((tm, width), F32),
                        pltpu.VMEM((1, width), F32)],
        compiler_params=_params(("arbitrary",), vmem),
        name="rwkv_in",
    )(*args)


def _rwkv_core_kernel(r_ref, lw_ref, c_ref, k_ref, v_ref, p_ref, a_ref, g_ref, rk_ref, lnw_ref,
                      lnb_ref, o_ref, h_ref, y_ref):
    @pl.when(pl.program_id(2) == 0)
    def _():
        h_ref[...] = jnp.zeros_like(h_ref)

    L = RWKV_CHUNK
    W = RWKV_PAIR
    hd = RWKV_HEAD_DIM
    n_chunks = r_ref.shape[0] // L
    n_pairs = r_ref.shape[1] // W

    ri = lax.broadcasted_iota(jnp.int32, (W, W), 0)
    ci = lax.broadcasted_iota(jnp.int32, (W, W), 1)
    same_head = (ri // hd) == (ci // hd)
    strict = same_head & ((ri % hd) > (ci % hd))
    incl = same_head & ((ri % hd) >= (ci % hd))
    eye = (ri == ci).astype(F32)
    merge_masks = []
    s = 1
    while s < hd:
        merge_masks.append(((ri // (2 * s)) == (ci // (2 * s))) & ((ri // s) != (ci // s)))
        s *= 2
    ones_bd = same_head.astype(BF16)
    bd_mask = (lax.broadcasted_iota(jnp.int32, (2 * L, W), 0) // L) == (
        lax.broadcasted_iota(jnp.int32, (2 * L, W), 1) // hd)

    def bd(x):
        return jnp.where(bd_mask, jnp.concatenate([x, x], axis=0), 0.0)

    def fold(x):
        return x[:L, :] + x[L:, :]

    def pair_chunk(sl, g):
        ln = slice(g * W, (g + 1) * W)
        r, lw, k, v, p = r_ref[sl, ln], lw_ref[sl, ln], k_ref[sl, ln], v_ref[sl, ln], p_ref[sl, ln]
        q = p * a_ref[sl, ln]
        c = c_ref[sl, ln]
        c_last = c[L - 1:L, :]
        e_in = jnp.exp(c)
        e_out = jnp.exp(-c)
        e_end = jnp.exp(c_last - c)
        rt, kt, qt = r * e_in, k * e_out, q * e_out
        pt = p * jnp.exp(c - lw)
        kh, qh = k * e_end, q * e_end

        pt_bd, rt_bd = bd(pt), bd(rt)
        lhs = jnp.concatenate([pt_bd, rt_bd], axis=0).astype(BF16)
        rhs = jnp.concatenate([bd(qt), bd(kt)], axis=0).astype(BF16)
        m = _dot_nt(lhs, rhs)
        yield
        a_pq = jnp.where(strict, m[:W, :W], 0.0)
        a_pk = jnp.where(strict, m[:W, W:], 0.0).astype(BF16)
        a_rq = jnp.where(incl, m[W:, :W], 0.0).astype(BF16)
        a_rk = jnp.where(incl, m[W:, W:], 0.0).astype(BF16)
        v_bd = bd(v).astype(BF16)
        pk_v = _dot(a_pk, v_bd)
        rk_v = _dot(a_rk, v_bd)
        yield

        t_inv = eye - jnp.where(merge_masks[0], a_pq, 0.0)
        for mask in merge_masks[1:]:
            tb = t_inv.astype(BF16)
            inner = _dot(jnp.where(mask, a_pq, 0.0).astype(BF16), tb)
            yield
            t_inv = t_inv - _dot(tb, inner.astype(BF16))
            yield
        t_inv = t_inv.astype(BF16)

        sol = _dot(t_inv, jnp.concatenate([pt_bd, pk_v], axis=1).astype(BF16)).astype(BF16)
        yield
        rq_sol = _dot(a_rq, sol)
        r_hat = fold(rt_bd - rq_sol[:, :W])
        y0 = fold(rk_v - rq_sol[:, W:])
        qh_sol = _dot_tn(bd(qh).astype(BF16), sol)
        g_mat = eye * jnp.exp(c_last) - qh_sol[:, :W]
        h_add = _dot_tn(bd(kh).astype(BF16), v_bd) - qh_sol[:, W:]
        yield

        h_hi, h_lo = _split2(h_ref[g])
        y_ref[sl, ln] = _dot(r_hat.astype(BF16), h_hi) + y0
        g_hi, g_lo = _split2(g_mat)
        gh = _dot(g_hi, jnp.concatenate([h_hi, h_lo], axis=1))
        h_ref[g] = gh[:, :W] + gh[:, W:] + _dot(g_lo, h_hi) + h_add

    chunks_per_step = 2 if n_chunks % 2 == 0 else 1

    def chunk_group(i, carry):
        live = []
        for j in range(chunks_per_step):
            sl = pl.ds(pl.multiple_of((i * chunks_per_step + j) * L, L), L)
            live += [pair_chunk(sl, g) for g in range(n_pairs)]
        while live:
            live = [gen for gen in live if next(gen, True) is None]
        return carry

    lax.fori_loop(0, n_chunks // chunks_per_step, chunk_group, 0)

    inv_hd = 1.0 / hd
    lanes = [slice(g * W, (g + 1) * W) for g in range(n_pairs)]
    ys = [y_ref[:, ln] for ln in lanes]
    mus = [_seg_sum(y, ones_bd) * inv_hd for y in ys]
    cens = [y - mu for y, mu in zip(ys, mus)]
    bonus_sums = [_seg_sum(r_ref[:, ln] * k_ref[:, ln] * rk_ref[:, ln], ones_bd) for ln in lanes]
    variances = [_seg_sum(cen * cen, ones_bd) * inv_hd for cen in cens]
    for ln, cen, var, bsum in zip(lanes, cens, variances, bonus_sums):
        yn = cen * lax.rsqrt(var + RWKV_LN_EPS) * lnw_ref[:, ln] + lnb_ref[:, ln]
        o_ref[:, ln] = ((yn + bsum * v_ref[:, ln]) * g_ref[:, ln]).astype(BF16)


def _rwkv_core(r, lw, c, k, v, p, a, gate, r_k, lnx_w, lnx_b, b, t):
    n, rw = r.shape
    assert rw % RWKV_PAIR == 0 and t % RWKV_CHUNK == 0
    group = _tile(rw, RWKV_GROUP_LANES, RWKV_PAIR)
    blk = _tile(t, 256, RWKV_CHUNK)
    nblk = t // blk
    tok = pl.BlockSpec((blk, group), lambda bi, pi, ci: (bi * nblk + ci, pi))
    vec = pl.BlockSpec((1, group), lambda bi, pi, ci: (0, pi))
    row2 = lambda x: x.reshape(1, rw)
    vmem = (2 * 8 * _nbytes((blk, group), F32) + 2 * _nbytes((blk, group), BF16)
            + _nbytes((blk, group), F32)
            + 64 * (group // RWKV_PAIR) * _nbytes((2 * RWKV_PAIR, 2 * RWKV_PAIR), F32)
            + 8 * _nbytes((blk, group), F32))
    return pl.pallas_call(
        _rwkv_core_kernel,
        grid=(b, rw // group, nblk),
        in_specs=[tok] * 8 + [vec] * 3,
        out_specs=tok,
        out_shape=jax.ShapeDtypeStruct((n, rw), BF16),
        scratch_shapes=[pltpu.VMEM((group // RWKV_PAIR, RWKV_PAIR, RWKV_PAIR), F32),
                        pltpu.VMEM((blk, group), F32)],
        compiler_params=_params(("parallel", "parallel", "arbitrary"), vmem),
        name="rwkv_core",
    )(r, lw, c, k, v, p, a, gate, row2(r_k), row2(lnx_w), row2(lnx_b))


def _lru_conv_gates(x, tail_prev, cw, cb, gxw, gxb, gaw, gab):
    lt, tail = x.shape[0], tail_prev.shape[0]
    ext = jnp.concatenate([tail_prev, x], axis=0)
    xc = cb + cw[CONV_WIDTH - 1:CONV_WIDTH, :] * x
    for j in range(1, CONV_WIDTH):
        xc = xc + cw[CONV_WIDTH - 1 - j:CONV_WIDTH - j, :] * pltpu.roll(ext, j, 0)[tail:, :]
    xcb = xc.astype(BF16)
    return xc, _dot(xcb, gxw) + gxb, _dot(xcb, gaw) + gab, x[lt - tail:, :]


def _lru_scan(xc, pre_x, pre_a, lam, h_prev):
    lt = xc.shape[0]
    gate_x = _sigmoid(pre_x)
    gate_a = _sigmoid(pre_a)
    neg_lam = -lam
    softplus = jnp.maximum(neg_lam, 0.0) + jnp.log(1.0 + jnp.exp(-jnp.abs(neg_lam)))
    log_a = -LRU_C * gate_a * softplus
    a = jnp.exp(log_a)
    bb = xc * gate_x * jnp.sqrt(-jnp.tanh(log_a) * (a * a + 1.0))

    sub = V7X_SUBLANES
    n_groups = lt // sub
    a = a.reshape(n_groups, sub, a.shape[1])
    bb = bb.reshape(n_groups, sub, bb.shape[1])
    row_in_group = lax.broadcasted_iota(jnp.int32, (1, sub, 1), 1)
    s = 1
    while s < sub:
        valid = row_in_group >= s
        b_prev = jnp.where(valid, pltpu.roll(bb, s, 1), 0.0)
        a_prev = jnp.where(valid, pltpu.roll(a, s, 1), 1.0)
        bb = bb + a * b_prev
        a = a * a_prev
        s *= 2
    groups = []
    for gi in range(n_groups):
        hg = a[gi] * h_prev + bb[gi]
        groups.append(hg)
        h_prev = hg[sub - 1:sub, :]
    return jnp.concatenate(groups, axis=0), h_prev


def _odd_mixer_kernel(tiles_per_seq, x_ref, g_ref, w_ref, cw_ref, cb_ref, gxw_ref, gxb_ref,
                      gaw_ref, gab_ref, lam_ref, o_ref, xn_ref, zy_ref, zx_ref, xtail_ref, h_ref):
    s = pl.program_id(0)
    n_tiles = pl.num_programs(0) - 1
    d = zy_ref.shape[1]
    seq_start = jnp.maximum(s - 1, 0) % tiles_per_seq == 0

    @pl.when(s == 0)
    def _():
        zy_ref[...] = jnp.zeros_like(zy_ref)
        zx_ref[...] = jnp.zeros_like(zx_ref)
        xtail_ref[...] = jnp.zeros_like(xtail_ref)
        h_ref[...] = jnp.zeros_like(h_ref)

    @pl.when(s < n_tiles)
    def _():
        xn_ref[...] = _rms(x_ref[...], g_ref[...]).astype(BF16)

    for blk in range(d // LRU_BLOCK):
        cols = slice(blk * LRU_BLOCK, (blk + 1) * LRU_BLOCK)
        xcols = slice(d + blk * LRU_BLOCK, d + (blk + 1) * LRU_BLOCK)
        tail_prev = jnp.where(seq_start, 0.0, xtail_ref[:, cols])
        h_prev = jnp.where(seq_start, 0.0, h_ref[:, cols])
        y = zy_ref[:, cols]
        xc, pre_x, pre_a, tail_new = _lru_conv_gates(
            zx_ref[:, cols], tail_prev, cw_ref[:, cols], cb_ref[:, cols], gxw_ref[blk],
            gxb_ref[:, cols], gaw_ref[blk], gab_ref[:, cols])
        gelu = 0.5 * y * (1.0 + jnp.tanh(math.sqrt(2.0 / math.pi) * (y + 0.044715 * (y * y * y))))
        xn = xn_ref[...]
        zy_ref[:, cols] = _dot(xn, w_ref[:, cols])
        zx_ref[:, cols] = _dot(xn, w_ref[:, xcols])
        hs, h_new = _lru_scan(xc, pre_x, pre_a, lam_ref[:, cols], h_prev)
        o_ref[:, cols] = (gelu * hs).astype(BF16)
        xtail_ref[:, cols] = tail_new
        h_ref[:, cols] = h_new


def _odd_mixer(h, norm_g, w_in_bf16, layer, t, conv_w, conv_b, gx_w, gx_b, ga_w, ga_b, lam):
    n, d = h.shape
    assert d % LRU_BLOCK == 0 and w_in_bf16.shape[2] == 2 * d
    nb = d // LRU_BLOCK
    tm = _tile(t, 256)
    n_tiles = n // tm
    tail = V7X_SUBLANES
    assert CONV_WIDTH - 1 <= tail <= tm
    once = pl.Buffered(1)
    vec = lambda rows: pl.BlockSpec((rows, d), lambda s: (0, 0), pipeline_mode=once)
    wsp = pl.BlockSpec((nb, LRU_BLOCK, LRU_BLOCK), lambda s: (0, 0, 0), pipeline_mode=once)
    row2 = lambda x: x.reshape(1, d)
    vmem = (_nbytes((d, 2 * d), BF16) + 2 * _nbytes((nb, LRU_BLOCK, LRU_BLOCK), BF16)
            + 2 * _nbytes((tm, d), F32) + _nbytes((tm, d), BF16) + 2 * _nbytes((tm, d), BF16)
            + 2 * _nbytes((tm, d), F32) + 2 * _nbytes((tm, d), F32)
            + 28 * _nbytes((tm, LRU_BLOCK), F32))
    return pl.pallas_call(
        functools.partial(_odd_mixer_kernel, t // tm),
        grid=(n_tiles + 1,),
        in_specs=[pl.BlockSpec((tm, d), lambda s: (jnp.minimum(s, n_tiles - 1), 0)),
                  vec(1),
                  pl.BlockSpec((None, d, 2 * d), lambda s: (layer, 0, 0), pipeline_mode=once),
                  vec(CONV_WIDTH), vec(1), wsp, vec(1), wsp, vec(1), vec(1)],
        out_specs=pl.BlockSpec((tm, d), lambda s: (jnp.maximum(s - 1, 0), 0)),
        out_shape=jax.ShapeDtypeStruct((n, d), BF16),
        scratch_shapes=[pltpu.VMEM((tm, d), BF16), pltpu.VMEM((tm, d), F32), pltpu.VMEM((tm, d), F32),
                        pltpu.VMEM((tail, d), F32), pltpu.VMEM((1, d), F32)],
        compiler_params=_params(("arbitrary",), vmem),
        name="odd_mixer",
    )(h, norm_g.reshape(1, d), w_in_bf16, conv_w, row2(conv_b), gx_w.astype(BF16), row2(gx_b),
      ga_w.astype(BF16), row2(ga_b), row2(lam))


def _even_layer(h, b, t, e, norm_g, w_in_all, w_out_all, mu, w0, w2, a0, a2, g2, k_k, k_a, r_k, lnx_w,
                lnx_b, v_first, v_res, rope):
    d = h.shape[1]
    ret_w = d // 2
    rw = d - ret_w
    ret_in = 4 * ret_w
    w_in = w_in_all[e]
    out_ret = _retention(h, norm_g, w_in[:, :ret_in].astype(BF16), t, ret_w, *rope)
    r, lw, c, k, v, p, a, gate = _rwkv_in(h, norm_g, w_in, t, ret_in, rw, mu, w2, a2, g2, w0, a0,
                                          k_k, k_a, v_first, v_res)
    out_rw = _rwkv_core(r, lw, c, k, v, p, a, gate, r_k.reshape(-1), lnx_w, lnx_b, b, t)
    h = _matmul_residual([out_ret, out_rw], w_out_all, e, h, "even_out_proj")
    return h, (v if v_res is None else v_first)


def _odd_layer(h, t, o, norm_g, w_in_b, conv_w, conv_b, gx_w, gx_b, ga_w, ga_b, lam, w_out_b):
    gated = _odd_mixer(h, norm_g, w_in_b, o, t, conv_w, conv_b, gx_w, gx_b, ga_w, ga_b, lam)
    return _matmul_residual([gated], w_out_b, o, h, "odd_out_proj")


def kernel(x, ev_norm, ev_w_in, ev_w_out, rw_mu, rw_w0, rw_w2, rw_a0, rw_a2, rw_g2, rw_k_k, rw_k_a, rw_r_k, rw_lnx_w, rw_lnx_b, rw_v0, rw_v1, rw_v2, od_norm, od_w_in, od_conv_w, od_conv_b, od_gx_w, od_gx_b, od_ga_w, od_ga_b, od_lam, od_w_out, ff_norm, ff_w1, ff_w2, final_norm):
    b, t, d = x.shape
    depth = ff_norm.shape[0]
    h = x.reshape(b * t, d)
    rope = _rope_tables(t, (d // 2) // RET_HEADS)
    ff_w1_b, ff_w2_b = ff_w1.astype(BF16), ff_w2.astype(BF16)
    od_w_in_b, od_w_out_b, ev_w_out_b = od_w_in.astype(BF16), od_w_out.astype(BF16), ev_w_out.astype(BF16)
    v_first = None
    for layer in range(depth):
        if layer % 2 == 0:
            e = layer // 2
            v_res = None if e == 0 else (rw_v0[e - 1], rw_v1[e - 1], rw_v2[e - 1])
            h, v_first = _even_layer(
                h, b, t, e, ev_norm[e], ev_w_in, ev_w_out_b, rw_mu[e], rw_w0[e], rw_w2[e],
                rw_a0[e], rw_a2[e], rw_g2[e], rw_k_k[e], rw_k_a[e], rw_r_k[e], rw_lnx_w[e],
                rw_lnx_b[e], v_first, v_res, rope)
        else:
            o = layer // 2
            h = _odd_layer(h, t, o, od_norm[o], od_w_in_b, od_conv_w[o], od_conv_b[o], od_gx_w[o],
                           od_gx_b[o], od_ga_w[o], od_ga_b[o], od_lam[o], od_w_out_b)
        h = _mlp(h, ff_norm[layer], ff_w1_b, ff_w2_b, layer, f"mlp_{layer}",
                 final_g=final_norm if layer == depth - 1 else None)
    return h.reshape(b, t, d)
```

```python
import functools
import math

import jax
import jax.numpy as jnp
from jax import lax
from jax.experimental import pallas as pl
from jax.experimental.pallas import tpu as pltpu

F32 = jnp.float32
BF16 = jnp.bfloat16

NORM_EPS = 1e-6
RET_HEADS = 4
RET_GN_EPS = 1e-5
ROPE_BASE = 10000.0
STREAM_CHUNK = 64
RWKV_HEAD_DIM = 64
RWKV_LN_EPS = 64e-5
LRU_BLOCK = 256
CONV_WIDTH = 4
LRU_C = 8.0

V7X_LANES = 128
V7X_SUBLANES = 8
V7X_VMEM_BYTES = 64 * 1024 * 1024
VMEM_CAP_BYTES = V7X_VMEM_BYTES - 8 * 1024 * 1024
VMEM_FLOOR_BYTES = 16 * 1024 * 1024

RWKV_CHUNK = 64
RWKV_PAIR = 2 * RWKV_HEAD_DIM
RWKV_GROUP_LANES = 8 * RWKV_PAIR
LORA_PAD = 512


def _tile(n, pref, mult=V7X_SUBLANES):
    if n <= pref:
        return n
    t = (pref // mult) * mult
    while t >= mult:
        if n % t == 0:
            return t
        t -= mult
    raise ValueError(f"no tile for {n} <= {pref}")


def _params(semantics, vmem_bytes):
    limit = int(min(max(vmem_bytes, VMEM_FLOOR_BYTES), VMEM_CAP_BYTES))
    return pltpu.CompilerParams(dimension_semantics=semantics, vmem_limit_bytes=limit)


def _nbytes(shape, dtype):
    return math.prod(shape) * jnp.dtype(dtype).itemsize


def _dot(a, b):
    return jnp.dot(a, b, preferred_element_type=F32)


def _dot_nt(a, b):
    return lax.dot_general(a, b, (((1,), (1,)), ((), ())), preferred_element_type=F32)


def _dot_tn(a, b):
    return lax.dot_general(a, b, (((0,), (0,)), ((), ())), preferred_element_type=F32)


def _split2(x):
    hi = x.astype(BF16)
    lo = (x - hi.astype(F32)).astype(BF16)
    return hi, lo


def _split3(x):
    hi = x.astype(BF16)
    r1 = x - hi.astype(F32)
    mid = r1.astype(BF16)
    lo = (r1 - mid.astype(F32)).astype(BF16)
    return hi, mid, lo


def _seg_sum(x, ones_bd):
    hi, lo = _split2(x)
    return _dot(hi, ones_bd) + _dot(lo, ones_bd)


def _rms(x, g):
    ms = jnp.mean(x * x, axis=-1, keepdims=True)
    return x * lax.rsqrt(ms + NORM_EPS) * g


def _sigmoid(x):
    return jax.nn.sigmoid(x)


def _matmul_residual_kernel(n_terms, *refs):
    x_refs, w_refs = refs[:n_terms], refs[n_terms:2 * n_terms]
    r_ref, o_ref = refs[2 * n_terms:]
    acc = r_ref[...]
    for x_ref, w_ref in zip(x_refs, w_refs):
        acc = acc + _dot(x_ref[...], w_ref[...])
    o_ref[...] = acc


def _matmul_residual(xs_bf16, w_bf16, layer, res, name):
    n, m = res.shape
    k = xs_bf16[0].shape[1]
    assert all(x.shape[1] == k for x in xs_bf16) and w_bf16.shape[1] == k * len(xs_bf16)
    tm = _tile(n, 512)
    ksum = w_bf16.shape[1]
    vmem = (2 * _nbytes((tm, ksum), BF16) + _nbytes((ksum, m), BF16)
            + (5 + len(xs_bf16)) * _nbytes((tm, m), F32))
    x_specs = [pl.BlockSpec((tm, k), lambda i: (i, 0)) for _ in xs_bf16]
    w_specs = [pl.BlockSpec((None, k, m), functools.partial(lambda i, r: (layer, r, 0), r=r),
                            pipeline_mode=pl.Buffered(1))
               for r in range(len(xs_bf16))]
    return pl.pallas_call(
        functools.partial(_matmul_residual_kernel, len(xs_bf16)),
        grid=(n // tm,),
        in_specs=x_specs + w_specs + [pl.BlockSpec((tm, m), lambda i: (i, 0))],
        out_specs=pl.BlockSpec((tm, m), lambda i: (i, 0)),
        out_shape=jax.ShapeDtypeStruct((n, m), F32),
        compiler_params=_params(("parallel",), vmem),
        name=name,
    )(*xs_bf16, *([w_bf16] * len(xs_bf16)), res)


def _mlp_kernel(has_final, *refs):
    if has_final:
        x_ref, g_ref, w1_ref, w2_ref, gf_ref, o_ref, xn_ref = refs
    else:
        x_ref, g_ref, w1_ref, w2_ref, o_ref, xn_ref = refs
    j = pl.program_id(1)

    @pl.when(j == 0)
    def _():
        x = x_ref[...]
        xn_ref[...] = _rms(x, g_ref[...]).astype(BF16)
        o_ref[...] = x

    a = jnp.maximum(_dot(xn_ref[...], w1_ref[...]), 0.0)
    o_ref[...] += _dot((a * a).astype(BF16), w2_ref[...])

    if has_final:
        @pl.when(j == pl.num_programs(1) - 1)
        def _():
            o_ref[...] = _rms(o_ref[...], gf_ref[...])


def _mlp(h, g, w1, w2, layer, name, final_g=None):
    n, d = h.shape
    f = w1.shape[2]
    tm = _tile(n, 1024)
    tf = _tile(f, 512, V7X_LANES)
    has_final = final_g is not None
    vec = pl.BlockSpec((1, d), lambda i, j: (0, 0))
    in_specs = [pl.BlockSpec((tm, d), lambda i, j: (i, 0)), vec,
                pl.BlockSpec((None, d, tf), lambda i, j: (layer, 0, j)),
                pl.BlockSpec((None, tf, d), lambda i, j: (layer, j, 0))]
    args = [h, g.reshape(1, d), w1, w2]
    if has_final:
        in_specs.append(vec)
        args.append(final_g.reshape(1, d))
    vmem = (4 * _nbytes((tm, d), F32) + _nbytes((tm, d), BF16) + 4 * _nbytes((d, tf), w1.dtype)
            + 3 * _nbytes((tm, tf), F32) + _nbytes((tm, d), F32))
    return pl.pallas_call(
        functools.partial(_mlp_kernel, has_final),
        grid=(n // tm, f // tf),
        in_specs=in_specs,
        out_specs=pl.BlockSpec((tm, d), lambda i, j: (i, 0)),
        out_shape=jax.ShapeDtypeStruct((n, d), F32),
        scratch_shapes=[pltpu.VMEM((tm, d), BF16)],
        compiler_params=_params(("parallel", "arbitrary"), vmem),
        name=name,
    )(*args)


def _retention_tables(blk, dh):
    log_g = jnp.log1p(-jnp.exp2(-5.0 - jnp.arange(RET_HEADS, dtype=F32)))
    pos = jnp.arange(blk, dtype=F32)
    n, m = pos[:, None], pos[None, :]
    cn, cm = jnp.floor(n / STREAM_CHUNK), jnp.floor(m / STREAM_CHUNK)
    dist = jnp.where(cn == cm, jnp.abs(n - m), n - m)
    lg = log_g[:, None, None]
    dmask = jnp.where((cm <= cn)[None], jnp.exp(lg * dist[None]), 0.0)
    qdec = jnp.broadcast_to(jnp.exp(lg * (pos + 1.0)[None, :, None]), (RET_HEADS, blk, dh))
    kdec = jnp.broadcast_to(jnp.exp(lg * (blk - 1.0 - pos)[None, :, None]), (RET_HEADS, blk, dh))
    cdec = jnp.broadcast_to(jnp.exp(lg * blk), (RET_HEADS, 1, dh))
    return dmask, qdec, kdec, cdec


def _rope_tables(t, dh):
    inv = 1.0 / (ROPE_BASE ** (jnp.arange(0, dh, 2, dtype=F32) / dh))
    ang = jnp.arange(t, dtype=F32)[:, None] * inv[None, :]
    return jnp.cos(ang), jnp.sin(ang)


def _retention_kernel(tiles_per_seq, x_ref, g_ref, w_ref, cos_ref, sin_ref, dm_ref, qd_ref, kd_ref,
                      cd_ref, o_ref, xn_ref, z_ref, s_ref):
    s = pl.program_id(0)
    ret_w = o_ref.shape[1]
    dh = ret_w // RET_HEADS
    half = dh // 2
    seq_start = jnp.maximum(s - 1, 0) % tiles_per_seq == 0

    @pl.when(s == 0)
    def _():
        z_ref[...] = jnp.zeros_like(z_ref)
        s_ref[...] = jnp.zeros_like(s_ref)

    xn_ref[...] = _rms(x_ref[...], g_ref[...]).astype(BF16)

    cos, sin = cos_ref[...], sin_ref[...]

    def rot(t):
        t1, t2 = t[:, :half], t[:, half:]
        return jnp.concatenate([t1 * cos - t2 * sin, t2 * cos + t1 * sin], axis=-1)

    for head in range(RET_HEADS):
        q_cols, k_cols, v_cols, g_cols = [
            slice(part * ret_w + head * dh, part * ret_w + (head + 1) * dh) for part in range(4)]
        q = rot(z_ref[:, q_cols])
        k = rot(z_ref[:, k_cols]) * (dh ** -0.5)
        vb = z_ref[:, v_cols].astype(BF16)
        gate = z_ref[:, g_cols]
        scores = _dot_nt(q.astype(BF16), k.astype(BF16))
        xn = xn_ref[...]
        for cols in (q_cols, k_cols, v_cols, g_cols):
            z_ref[:, cols] = _dot(xn, w_ref[:, cols])
        scores = scores * dm_ref[head]
        state = jnp.where(seq_start, 0.0, s_ref[head])
        out = _dot(scores.astype(BF16), vb) + _dot((q * qd_ref[head]).astype(BF16), state.astype(BF16))
        s_ref[head] = state * cd_ref[head] + _dot_tn((k * kd_ref[head]).astype(BF16), vb)

        mu = jnp.mean(out, axis=-1, keepdims=True)
        cen = out - mu
        var = jnp.mean(cen * cen, axis=-1, keepdims=True)
        y = cen * lax.rsqrt(var + RET_GN_EPS)
        o_ref[:, head * dh:(head + 1) * dh] = (gate * _sigmoid(gate) * y).astype(BF16)


def _retention(h, norm_g, w_ret_bf16, t, ret_w, cos, sin):
    n, d = h.shape
    dh = ret_w // RET_HEADS
    assert w_ret_bf16.shape == (d, 4 * ret_w)
    tm = _tile(t, 256, STREAM_CHUNK)
    n_tiles = n // tm
    tiles_per_seq = t // tm
    tables = _retention_tables(tm, dh)
    once = pl.Buffered(1)
    full = lambda a: pl.BlockSpec(a.shape, lambda s: (0,) * a.ndim, pipeline_mode=once)
    rope_spec = pl.BlockSpec((tm, dh // 2), lambda s: (jnp.maximum(s - 1, 0) % tiles_per_seq, 0))
    g2 = norm_g.reshape(1, d)
    vmem = (_nbytes(w_ret_bf16.shape, BF16) + sum(_nbytes(a.shape, F32) for a in tables)
            + 2 * _nbytes((tm, d), F32) + _nbytes((tm, d), BF16) + 2 * _nbytes((tm, ret_w), BF16)
            + 4 * _nbytes((tm, dh // 2), F32) + _nbytes((tm, 4 * ret_w), F32)
            + _nbytes((RET_HEADS, dh, dh), F32) + 2 * _nbytes((tm, d), F32)
            + 16 * _nbytes((tm, max(tm, dh)), F32))
    return pl.pallas_call(
        functools.partial(_retention_kernel, tiles_per_seq),
        grid=(n_tiles + 1,),
        in_specs=[pl.BlockSpec((tm, d), lambda s: (jnp.minimum(s, n_tiles - 1), 0)),
                  full(g2), full(w_ret_bf16), rope_spec, rope_spec] + [full(a) for a in tables],
        out_specs=pl.BlockSpec((tm, ret_w), lambda s: (jnp.maximum(s - 1, 0), 0)),
        out_shape=jax.ShapeDtypeStruct((n, ret_w), BF16),
        scratch_shapes=[pltpu.VMEM((tm, d), BF16), pltpu.VMEM((tm, 4 * ret_w), F32),
                        pltpu.VMEM((RET_HEADS, dh, dh), F32)],
        compiler_params=_params(("arbitrary",), vmem),
        name="retention",
    )(h, g2, w_ret_bf16, cos, sin, *tables)


def _rwkv_in_kernel(has_vres, tiles_per_seq, rw, ranks, windows, *refs):
    if has_vres:
        (x_ref, g_ref, w_ref, mu_ref, wlw_ref, wla_ref, wlg_ref, w0_ref, a0_ref, kk_ref, ka_ref,
         tri_ref, vf_ref, v0_ref, v1_ref, v2_ref,
         r_out, lw_out, c_out, k_out, v_out, p_out, a_out, g_out, xn_ref, z_ref, carry_ref) = refs
    else:
        (x_ref, g_ref, w_ref, mu_ref, wlw_ref, wla_ref, wlg_ref, w0_ref, a0_ref, kk_ref, ka_ref,
         tri_ref,
         r_out, lw_out, c_out, k_out, v_out, p_out, a_out, g_out, xn_ref, z_ref, carry_ref) = refs
    s = pl.program_id(0)
    tm = z_ref.shape[0]
    at_start = jnp.maximum(s - 1, 0) % tiles_per_seq == 0
    row0 = lax.broadcasted_iota(jnp.int32, (tm, 1), 0) == 0

    @pl.when(s == 0)
    def _():
        z_ref[...] = jnp.zeros_like(z_ref)
        carry_ref[...] = jnp.zeros_like(carry_ref)

    xn_ref[...] = _rms(x_ref[...], g_ref[...]).astype(BF16)

    def shift_mix_then_project(lo, hi):
        x = z_ref[:, lo:hi]
        last = jnp.where(at_start, 0.0, carry_ref[:, lo:hi])
        prev = jnp.where(row0, last, pltpu.roll(x, 1, 0))
        carry_ref[:, lo:hi] = x[tm - 1:tm, :]
        mixed = x + mu_ref[:, lo:hi] * (prev - x)
        z_ref[:, lo:hi] = _dot(xn_ref[...], w_ref[:, lo:hi])
        return mixed

    rank_w, rank_a = ranks
    lora = shift_mix_then_project(3 * rw, z_ref.shape[1])
    lane = lax.broadcasted_iota(jnp.int32, (1, lora.shape[1]), 1)
    feat = jnp.where(lane < rank_w, jnp.tanh(lora),
                     jnp.where(lane < rank_w + rank_a, lora, _sigmoid(lora))).astype(BF16)
    (w_lo, w_hi), (a_lo, a_hi), (g_lo, g_hi) = windows
    proj_w = _dot(feat[:, w_lo:w_hi], wlw_ref[...])
    proj_a = _dot(feat[:, a_lo:a_hi], wla_ref[...])
    gate = _dot(feat[:, g_lo:g_hi], wlg_ref[...])

    r = shift_mix_then_project(0, rw)
    log_w = -math.exp(-0.5) * _sigmoid(w0_ref[...] + proj_w)
    iclr = _sigmoid(a0_ref[...] + proj_a)

    vr = shift_mix_then_project(2 * rw, 3 * rw)
    if has_vres:
        low = _dot(vr.astype(BF16), v1_ref[...])
        mix = _sigmoid(v0_ref[...] + _dot(low.astype(BF16), v2_ref[...]))
        vr = vr + (vf_ref[...] - vr) * mix

    kr = shift_mix_then_project(rw, 2 * rw)
    kk = kr * kk_ref[...]
    pair = lax.broadcasted_iota(jnp.int32, (RWKV_PAIR, RWKV_PAIR), 0) // RWKV_HEAD_DIM
    ones_bd = (pair == pair.T).astype(BF16)
    sq = kk * kk
    norm = jnp.sqrt(jnp.concatenate(
        [_seg_sum(sq[:, i:i + RWKV_PAIR], ones_bd) for i in range(0, rw, RWKV_PAIR)], axis=1))
    kk = kk / jnp.maximum(norm, 1e-12)
    k2 = kr * (1.0 + (iclr - 1.0) * ka_ref[...])

    l_hi, l_mid, l_lo = _split3(log_w)
    tri = tri_ref[...]
    c_out[...] = _dot(tri, l_hi) + _dot(tri, l_mid) + _dot(tri, l_lo)

    r_out[...] = r
    lw_out[...] = log_w
    k_out[...] = k2
    v_out[...] = vr
    p_out[...] = kk
    a_out[...] = iclr
    g_out[...] = gate


def _lane_window(lo, hi):
    return (lo // V7X_LANES) * V7X_LANES, -(-hi // V7X_LANES) * V7X_LANES


def _rwkv_in(h, norm_g, w_in, t, ret_in, rw, mu, w2, a2, g2, w0, a0, k_k, k_a, v_first, v_res):
    n, d = h.shape
    has_vres = v_res is not None
    rank_w, rank_a, rank_g = w2.shape[0], a2.shape[0], g2.shape[0]
    lora = rank_w + rank_a + rank_g
    assert lora <= LORA_PAD and rw % RWKV_PAIR == 0 and w_in.shape[1] == ret_in + 3 * rw + lora
    width = 3 * rw + LORA_PAD
    tm = _tile(t, 256, RWKV_CHUNK)
    n_tiles = n // tm
    pos = jnp.arange(tm)
    tri = ((pos[:, None] // RWKV_CHUNK == pos[None, :] // RWKV_CHUNK)
           & (pos[:, None] >= pos[None, :])).astype(BF16)
    row2 = lambda v: v.reshape(1, -1)
    w_rw = jnp.pad(w_in[:, ret_in:], ((0, 0), (0, LORA_PAD - lora))).astype(BF16)
    mu_p = jnp.pad(mu, (0, LORA_PAD - lora)).reshape(1, width)
    bounds = [(0, rank_w), (rank_w, rank_w + rank_a), (rank_w + rank_a, lora)]
    windows = tuple(_lane_window(lo, hi) for lo, hi in bounds)

    def embed(w_low, bound, window):
        rows = jnp.zeros((window[1] - window[0], rw), F32)
        return rows.at[bound[0] - window[0]:bound[1] - window[0]].set(w_low).astype(BF16)

    wl = [embed(wg, bd, win) for wg, bd, win in zip((w2, a2, g2), bounds, windows)]
    once = pl.Buffered(1)
    vec = pl.BlockSpec((1, rw), lambda s: (0, 0), pipeline_mode=once)
    full = lambda a: pl.BlockSpec(a.shape, lambda s: (0,) * a.ndim, pipeline_mode=once)
    prev_tile = pl.BlockSpec((tm, rw), lambda s: (jnp.maximum(s - 1, 0), 0))
    args = [h, row2(norm_g), w_rw, mu_p, *wl, row2(w0), row2(a0), row2(k_k), row2(k_a), tri]
    specs = [pl.BlockSpec((tm, d), lambda s: (jnp.minimum(s, n_tiles - 1), 0)),
             full(row2(norm_g)), full(w_rw), full(mu_p), *[full(w) for w in wl],
             vec, vec, vec, vec, full(tri)]
    resident = _nbytes(w_rw.shape, BF16) + sum(_nbytes(w.shape, BF16) for w in wl) + _nbytes(tri.shape, BF16)
    if has_vres:
        v0, v1, v2 = v_res
        rank_v = v1.shape[1]
        rank_pad = _lane_window(0, rank_v)[1]
        v1p = jnp.zeros((rw, rank_pad), F32).at[:, :rank_v].set(v1).astype(BF16)
        v2p = jnp.zeros((rank_pad, rw), F32).at[:rank_v, :].set(v2).astype(BF16)
        args += [v_first, row2(v0), v1p, v2p]
        specs += [prev_tile, vec, full(v1p), full(v2p)]
        resident += 2 * _nbytes(v1p.shape, BF16)
    vmem = (resident + 2 * _nbytes((tm, d), F32) + 2 * (8 + has_vres) * _nbytes((tm, rw), F32)
            + _nbytes((tm, d), BF16) + _nbytes((tm, width), F32) + 10 * _nbytes((tm, rw), F32))
    return pl.pallas_call(
        functools.partial(_rwkv_in_kernel, has_vres, t // tm, rw, (rank_w, rank_a), windows),
        grid=(n_tiles + 1,),
        in_specs=specs,
        out_specs=[prev_tile] * 8,
        out_shape=[jax.ShapeDtypeStruct((n, rw), F32)] * 8,
        scratch_shapes=[pltpu.VMEM((tm, d), BF16), pltpu.VMEM((tm, width), F32),
                        pltpu.VMEM((1, width), F32)],
        compiler_params=_params(("arbitrary",), vmem),
        name="rwkv_in",
    )(*args)


def _rwkv_core_kernel(r_ref, lw_ref, c_ref, k_ref, v_ref, p_ref, a_ref, g_ref, rk_ref, lnw_ref,
                      lnb_ref, o_ref, h_ref, y_ref):
    @pl.when(pl.program_id(2) == 0)
    def _():
        h_ref[...] = jnp.zeros_like(h_ref)

    L = RWKV_CHUNK
    W = RWKV_PAIR
    hd = RWKV_HEAD_DIM
    n_chunks = r_ref.shape[0] // L
    n_pairs = r_ref.shape[1] // W

    ri = lax.broadcasted_iota(jnp.int32, (W, W), 0)
    ci = lax.broadcasted_iota(jnp.int32, (W, W), 1)
    same_head = (ri // hd) == (ci // hd)
    strict = same_head & ((ri % hd) > (ci % hd))
    incl = same_head & ((ri % hd) >= (ci % hd))
    eye = (ri == ci).astype(F32)
    merge_masks = []
    s = 1
    while s < hd:
        merge_masks.append(((ri // (2 * s)) == (ci // (2 * s))) & ((ri // s) != (ci // s)))
        s *= 2
    ones_bd = same_head.astype(BF16)
    bd_mask = (lax.broadcasted_iota(jnp.int32, (2 * L, W), 0) // L) == (
        lax.broadcasted_iota(jnp.int32, (2 * L, W), 1) // hd)

    def bd(x):
        return jnp.where(bd_mask, jnp.concatenate([x, x], axis=0), 0.0)

    def fold(x):
        return x[:L, :] + x[L:, :]

    def pair_chunk(sl, g):
        ln = slice(g * W, (g + 1) * W)
        r, lw, k, v, p = r_ref[sl, ln], lw_ref[sl, ln], k_ref[sl, ln], v_ref[sl, ln], p_ref[sl, ln]
        q = p * a_ref[sl, ln]
        c = c_ref[sl, ln]
        c_last = c[L - 1:L, :]
        e_in = jnp.exp(c)
        e_out = jnp.exp(-c)
        e_end = jnp.exp(c_last - c)
        rt, kt, qt = r * e_in, k * e_out, q * e_out
        pt = p * jnp.exp(c - lw)
        kh, qh = k * e_end, q * e_end

        pt_bd, rt_bd = bd(pt), bd(rt)
        lhs = jnp.concatenate([pt_bd, rt_bd], axis=0).astype(BF16)
        rhs = jnp.concatenate([bd(qt), bd(kt)], axis=0).astype(BF16)
        m = _dot_nt(lhs, rhs)
        yield
        a_pq = jnp.where(strict, m[:W, :W], 0.0)
        a_pk = jnp.where(strict, m[:W, W:], 0.0).astype(BF16)
        a_rq = jnp.where(incl, m[W:, :W], 0.0).astype(BF16)
        a_rk = jnp.where(incl, m[W:, W:], 0.0).astype(BF16)
        v_bd = bd(v).astype(BF16)
        pk_v = _dot(a_pk, v_bd)
        rk_v = _dot(a_rk, v_bd)
        yield

        t_inv = eye - jnp.where(merge_masks[0], a_pq, 0.0)
        for mask in merge_masks[1:]:
            tb = t_inv.astype(BF16)
            inner = _dot(jnp.where(mask, a_pq, 0.0).astype(BF16), tb)
            yield
            t_inv = t_inv - _dot(tb, inner.astype(BF16))
            yield
        t_inv = t_inv.astype(BF16)

        sol = _dot(t_inv, jnp.concatenate([pt_bd, pk_v], axis=1).astype(BF16)).astype(BF16)
        yield
        rq_sol = _dot(a_rq, sol)
        r_hat = fold(rt_bd - rq_sol[:, :W])
        y0 = fold(rk_v - rq_sol[:, W:])
        qh_sol = _dot_tn(bd(qh).astype(BF16), sol)
        g_mat = eye * jnp.exp(c_last) - qh_sol[:, :W]
        h_add = _dot_tn(bd(kh).astype(BF16), v_bd) - qh_sol[:, W:]
        yield

        h_hi, h_lo = _split2(h_ref[g])
        y_ref[sl, ln] = _dot(r_hat.astype(BF16), h_hi) + y0
        g_hi, g_lo = _split2(g_mat)
        gh = _dot(g_hi, jnp.concatenate([h_hi, h_lo], axis=1))
        h_ref[g] = gh[:, :W] + gh[:, W:] + _dot(g_lo, h_hi) + h_add

    chunks_per_step = 2 if n_chunks % 2 == 0 else 1

    def chunk_group(i, carry):
        live = []
        for j in range(chunks_per_step):
            sl = pl.ds(pl.multiple_of((i * chunks_per_step + j) * L, L), L)
            live += [pair_chunk(sl, g) for g in range(n_pairs)]
        while live:
            live = [gen for gen in live if next(gen, True) is None]
        return carry

    lax.fori_loop(0, n_chunks // chunks_per_step, chunk_group, 0)

    inv_hd = 1.0 / hd
    lanes = [slice(g * W, (g + 1) * W) for g in range(n_pairs)]
    ys = [y_ref[:, ln] for ln in lanes]
    mus = [_seg_sum(y, ones_bd) * inv_hd for y in ys]
    cens = [y - mu for y, mu in zip(ys, mus)]
    bonus_sums = [_seg_sum(r_ref[:, ln] * k_ref[:, ln] * rk_ref[:, ln], ones_bd) for ln in lanes]
    variances = [_seg_sum(cen * cen, ones_bd) * inv_hd for cen in cens]
    for ln, cen, var, bsum in zip(lanes, cens, variances, bonus_sums):
        yn = cen * lax.rsqrt(var + RWKV_LN_EPS) * lnw_ref[:, ln] + lnb_ref[:, ln]
        o_ref[:, ln] = ((yn + bsum * v_ref[:, ln]) * g_ref[:, ln]).astype(BF16)


def _rwkv_core(r, lw, c, k, v, p, a, gate, r_k, lnx_w, lnx_b, b, t):
    n, rw = r.shape
    assert rw % RWKV_PAIR == 0 and t % RWKV_CHUNK == 0
    group = _tile(rw, RWKV_GROUP_LANES, RWKV_PAIR)
    blk = _tile(t, 256, RWKV_CHUNK)
    nblk = t // blk
    tok = pl.BlockSpec((blk, group), lambda bi, pi, ci: (bi * nblk + ci, pi))
    vec = pl.BlockSpec((1, group), lambda bi, pi, ci: (0, pi))
    row2 = lambda x: x.reshape(1, rw)
    vmem = (2 * 8 * _nbytes((blk, group), F32) + 2 * _nbytes((blk, group), BF16)
            + _nbytes((blk, group), F32)
            + 64 * (group // RWKV_PAIR) * _nbytes((2 * RWKV_PAIR, 2 * RWKV_PAIR), F32)
            + 8 * _nbytes((blk, group), F32))
    return pl.pallas_call(
        _rwkv_core_kernel,
        grid=(b, rw // group, nblk),
        in_specs=[tok] * 8 + [vec] * 3,
        out_specs=tok,
        out_shape=jax.ShapeDtypeStruct((n, rw), BF16),
        scratch_shapes=[pltpu.VMEM((group // RWKV_PAIR, RWKV_PAIR, RWKV_PAIR), F32),
                        pltpu.VMEM((blk, group), F32)],
        compiler_params=_params(("parallel", "parallel", "arbitrary"), vmem),
        name="rwkv_core",
    )(r, lw, c, k, v, p, a, gate, row2(r_k), row2(lnx_w), row2(lnx_b))


def _lru_conv_gates(x, tail_prev, cw, cb, gxw, gxb, gaw, gab):
    lt, tail = x.shape[0], tail_prev.shape[0]
    ext = jnp.concatenate([tail_prev, x], axis=0)
    xc = cb + cw[CONV_WIDTH - 1:CONV_WIDTH, :] * x
    for j in range(1, CONV_WIDTH):
        xc = xc + cw[CONV_WIDTH - 1 - j:CONV_WIDTH - j, :] * pltpu.roll(ext, j, 0)[tail:, :]
    xcb = xc.astype(BF16)
    return xc, _dot(xcb, gxw) + gxb, _dot(xcb, gaw) + gab, x[lt - tail:, :]


def _lru_scan(xc, pre_x, pre_a, lam, h_prev):
    lt = xc.shape[0]
    gate_x = _sigmoid(pre_x)
    gate_a = _sigmoid(pre_a)
    neg_lam = -lam
    softplus = jnp.maximum(neg_lam, 0.0) + jnp.log(1.0 + jnp.exp(-jnp.abs(neg_lam)))
    log_a = -LRU_C * gate_a * softplus
    a = jnp.exp(log_a)
    bb = xc * gate_x * jnp.sqrt(-jnp.tanh(log_a) * (a * a + 1.0))

    sub = V7X_SUBLANES
    n_groups = lt // sub
    a = a.reshape(n_groups, sub, a.shape[1])
    bb = bb.reshape(n_groups, sub, bb.shape[1])
    row_in_group = lax.broadcasted_iota(jnp.int32, (1, sub, 1), 1)
    s = 1
    while s < sub:
        valid = row_in_group >= s
        b_prev = jnp.where(valid, pltpu.roll(bb, s, 1), 0.0)
        a_prev = jnp.where(valid, pltpu.roll(a, s, 1), 1.0)
        bb = bb + a * b_prev
        a = a * a_prev
        s *= 2
    groups = []
    for gi in range(n_groups):
        hg = a[gi] * h_prev + bb[gi]
        groups.append(hg)
        h_prev = hg[sub - 1:sub, :]
    return jnp.concatenate(groups, axis=0), h_prev


def _odd_mixer_kernel(tiles_per_seq, x_ref, g_ref, w_ref, cw_ref, cb_ref, gxw_ref, gxb_ref,
                      gaw_ref, gab_ref, lam_ref, o_ref, xn_ref, zy_ref, zx_ref, xtail_ref, h_ref):
    s = pl.program_id(0)
    n_tiles = pl.num_programs(0) - 1
    d = zy_ref.shape[1]
    seq_start = jnp.maximum(s - 1, 0) % tiles_per_seq == 0

    @pl.when(s == 0)
    def _():
        zy_ref[...] = jnp.zeros_like(zy_ref)
        zx_ref[...] = jnp.zeros_like(zx_ref)
        xtail_ref[...] = jnp.zeros_like(xtail_ref)
        h_ref[...] = jnp.zeros_like(h_ref)

    @pl.when(s < n_tiles)
    def _():
        xn_ref[...] = _rms(x_ref[...], g_ref[...]).astype(BF16)

    for blk in range(d // LRU_BLOCK):
        cols = slice(blk * LRU_BLOCK, (blk + 1) * LRU_BLOCK)
        xcols = slice(d + blk * LRU_BLOCK, d + (blk + 1) * LRU_BLOCK)
        tail_prev = jnp.where(seq_start, 0.0, xtail_ref[:, cols])
        h_prev = jnp.where(seq_start, 0.0, h_ref[:, cols])
        y = zy_ref[:, cols]
        xc, pre_x, pre_a, tail_new = _lru_conv_gates(
            zx_ref[:, cols], tail_prev, cw_ref[:, cols], cb_ref[:, cols], gxw_ref[blk],
            gxb_ref[:, cols], gaw_ref[blk], gab_ref[:, cols])
        gelu = 0.5 * y * (1.0 + jnp.tanh(math.sqrt(2.0 / math.pi) * (y + 0.044715 * (y * y * y))))
        xn = xn_ref[...]
        zy_ref[:, cols] = _dot(xn, w_ref[:, cols])
        zx_ref[:, cols] = _dot(xn, w_ref[:, xcols])
        hs, h_new = _lru_scan(xc, pre_x, pre_a, lam_ref[:, cols], h_prev)
        o_ref[:, cols] = (gelu * hs).astype(BF16)
        xtail_ref[:, cols] = tail_new
        h_ref[:, cols] = h_new


def _odd_mixer(h, norm_g, w_in_bf16, layer, t, conv_w, conv_b, gx_w, gx_b, ga_w, ga_b, lam):
    n, d = h.shape
    assert d % LRU_BLOCK == 0 and w_in_bf16.shape[2] == 2 * d
    nb = d // LRU_BLOCK
    tm = _tile(t, 256)
    n_tiles = n // tm
    tail = V7X_SUBLANES
    assert CONV_WIDTH - 1 <= tail <= tm
    once = pl.Buffered(1)
    vec = lambda rows: pl.BlockSpec((rows, d), lambda s: (0, 0), pipeline_mode=once)
    wsp = pl.BlockSpec((nb, LRU_BLOCK, LRU_BLOCK), lambda s: (0, 0, 0), pipeline_mode=once)
    row2 = lambda x: x.reshape(1, d)
    vmem = (_nbytes((d, 2 * d), BF16) + 2 * _nbytes((nb, LRU_BLOCK, LRU_BLOCK), BF16)
            + 2 * _nbytes((tm, d), F32) + _nbytes((tm, d), BF16) + 2 * _nbytes((tm, d), BF16)
            + 2 * _nbytes((tm, d), F32) + 2 * _nbytes((tm, d), F32)
            + 28 * _nbytes((tm, LRU_BLOCK), F32))
    return pl.pallas_call(
        functools.partial(_odd_mixer_kernel, t // tm),
        grid=(n_tiles + 1,),
        in_specs=[pl.BlockSpec((tm, d), lambda s: (jnp.minimum(s, n_tiles - 1), 0)),
                  vec(1),
                  pl.BlockSpec((None, d, 2 * d), lambda s: (layer, 0, 0), pipeline_mode=once),
                  vec(CONV_WIDTH), vec(1), wsp, vec(1), wsp, vec(1), vec(1)],
        out_specs=pl.BlockSpec((tm, d), lambda s: (jnp.maximum(s - 1, 0), 0)),
        out_shape=jax.ShapeDtypeStruct((n, d), BF16),
        scratch_shapes=[pltpu.VMEM((tm, d), BF16), pltpu.VMEM((tm, d), F32), pltpu.VMEM((tm, d), F32),
                        pltpu.VMEM((tail, d), F32), pltpu.VMEM((1, d), F32)],
        compiler_params=_params(("arbitrary",), vmem),
        name="odd_mixer",
    )(h, norm_g.reshape(1, d), w_in_bf16, conv_w, row2(conv_b), gx_w.astype(BF16), row2(gx_b),
      ga_w.astype(BF16), row2(ga_b), row2(lam))


def _even_layer(h, b, t, e, norm_g, w_in_all, w_out_all, mu, w0, w2, a0, a2, g2, k_k, k_a, r_k, lnx_w,
                lnx_b, v_first, v_res, rope):
    d = h.shape[1]
    ret_w = d // 2
    rw = d - ret_w
    ret_in = 4 * ret_w
    w_in = w_in_all[e]
    out_ret = _retention(h, norm_g, w_in[:, :ret_in].astype(BF16), t, ret_w, *rope)
    r, lw, c, k, v, p, a, gate = _rwkv_in(h, norm_g, w_in, t, ret_in, rw, mu, w2, a2, g2, w0, a0,
                                          k_k, k_a, v_first, v_res)
    out_rw = _rwkv_core(r, lw, c, k, v, p, a, gate, r_k.reshape(-1), lnx_w, lnx_b, b, t)
    h = _matmul_residual([out_ret, out_rw], w_out_all, e, h, "even_out_proj")
    return h, (v if v_res is None else v_first)


def _odd_layer(h, t, o, norm_g, w_in_b, conv_w, conv_b, gx_w, gx_b, ga_w, ga_b, lam, w_out_b):
    gated = _odd_mixer(h, norm_g, w_in_b, o, t, conv_w, conv_b, gx_w, gx_b, ga_w, ga_b, lam)
    return _matmul_residual([gated], w_out_b, o, h, "odd_out_proj")


def kernel(x, ev_norm, ev_w_in, ev_w_out, rw_mu, rw_w0, rw_w2, rw_a0, rw_a2, rw_g2, rw_k_k, rw_k_a, rw_r_k, rw_lnx_w, rw_lnx_b, rw_v0, rw_v1, rw_v2, od_norm, od_w_in, od_conv_w, od_conv_b, od_gx_w, od_gx_b, od_ga_w, od_ga_b, od_lam, od_w_out, ff_norm, ff_w1, ff_w2, final_norm):
    b, t, d = x.shape
    depth = ff_norm.shape[0]
    h = x.reshape(b * t, d)
    rope = _rope_tables(t, (d // 2) // RET_HEADS)
    ff_w1_b, ff_w2_b = ff_w1.astype(BF16), ff_w2.astype(BF16)
    od_w_in_b, od_w_out_b, ev_w_out_b = od_w_in.astype(BF16), od_w_out.astype(BF16), ev_w_out.astype(BF16)
    v_first = None
    for layer in range(depth):
        if layer % 2 == 0:
            e = layer // 2
            v_res = None if e == 0 else (rw_v0[e - 1], rw_v1[e - 1], rw_v2[e - 1])
            h, v_first = _even_layer(
                h, b, t, e, ev_norm[e], ev_w_in, ev_w_out_b, rw_mu[e], rw_w0[e], rw_w2[e],
                rw_a0[e], rw_a2[e], rw_g2[e], rw_k_k[e], rw_k_a[e], rw_r_k[e], rw_lnx_w[e],
                rw_lnx_b[e], v_first, v_res, rope)
        else:
            o = layer // 2
            h = _odd_layer(h, t, o, od_norm[o], od_w_in_b, od_conv_w[o], od_conv_b[o], od_gx_w[o],
                           od_gx_b[o], od_ga_w[o], od_ga_b[o], od_lam[o], od_w_out_b)
        h = _mlp(h, ff_norm[layer], ff_w1_b, ff_w2_b, layer, f"mlp_{layer}",
                 final_g=final_norm if layer == depth - 1 else None)
    return h.reshape(b, t, d)
```

```python
import functools
import math

import jax
import jax.numpy as jnp
from jax import lax
from jax.experimental import pallas as pl
from jax.experimental.pallas import tpu as pltpu

F32 = jnp.float32
BF16 = jnp.bfloat16

NORM_EPS = 1e-6
RET_HEADS = 4
RET_GN_EPS = 1e-5
ROPE_BASE = 10000.0
STREAM_CHUNK = 64
RWKV_HEAD_DIM = 64
RWKV_LN_EPS = 64e-5
LRU_BLOCK = 256
CONV_WIDTH = 4
LRU_C = 8.0

V7X_LANES = 128
V7X_SUBLANES = 8
V7X_VMEM_BYTES = 64 * 1024 * 1024
VMEM_CAP_BYTES = V7X_VMEM_BYTES - 8 * 1024 * 1024
VMEM_FLOOR_BYTES = 16 * 1024 * 1024

RWKV_CHUNK = 64
RWKV_PAIR = 2 * RWKV_HEAD_DIM
RWKV_GROUP_LANES = 8 * RWKV_PAIR
LORA_PAD = 512


def _tile(n, pref, mult=V7X_SUBLANES):
    if n <= pref:
        return n
    t = (pref // mult) * mult
    while t >= mult:
        if n % t == 0:
            return t
        t -= mult
    raise ValueError(f"no tile for {n} <= {pref}")


def _params(semantics, vmem_bytes):
    limit = int(min(max(vmem_bytes, VMEM_FLOOR_BYTES), VMEM_CAP_BYTES))
    return pltpu.CompilerParams(dimension_semantics=semantics, vmem_limit_bytes=limit)


def _nbytes(shape, dtype):
    return math.prod(shape) * jnp.dtype(dtype).itemsize


def _dot(a, b):
    return jnp.dot(a, b, preferred_element_type=F32)


def _dot_nt(a, b):
    return lax.dot_general(a, b, (((1,), (1,)), ((), ())), preferred_element_type=F32)


def _dot_tn(a, b):
    return lax.dot_general(a, b, (((0,), (0,)), ((), ())), preferred_element_type=F32)


def _split2(x):
    hi = x.astype(BF16)
    lo = (x - hi.astype(F32)).astype(BF16)
    return hi, lo


def _split3(x):
    hi = x.astype(BF16)
    r1 = x - hi.astype(F32)
    mid = r1.astype(BF16)
    lo = (r1 - mid.astype(F32)).astype(BF16)
    return hi, mid, lo


def _seg_sum(x, ones_bd):
    hi, lo = _split2(x)
    return _dot(hi, ones_bd) + _dot(lo, ones_bd)


def _rms(x, g):
    ms = jnp.mean(x * x, axis=-1, keepdims=True)
    return x * lax.rsqrt(ms + NORM_EPS) * g


def _sigmoid(x):
    return jax.nn.sigmoid(x)


def _matmul_residual_kernel(n_terms, *refs):
    x_refs, w_refs = refs[:n_terms], refs[n_terms:2 * n_terms]
    r_ref, o_ref = refs[2 * n_terms:]
    acc = r_ref[...]
    for x_ref, w_ref in zip(x_refs, w_refs):
        acc = acc + _dot(x_ref[...], w_ref[...])
    o_ref[...] = acc


def _matmul_residual(xs_bf16, w_bf16, layer, res, name):
    n, m = res.shape
    k = xs_bf16[0].shape[1]
    assert all(x.shape[1] == k for x in xs_bf16) and w_bf16.shape[1] == k * len(xs_bf16)
    tm = _tile(n, 512)
    ksum = w_bf16.shape[1]
    vmem = (2 * _nbytes((tm, ksum), BF16) + _nbytes((ksum, m), BF16)
            + (5 + len(xs_bf16)) * _nbytes((tm, m), F32))
    x_specs = [pl.BlockSpec((tm, k), lambda i: (i, 0)) for _ in xs_bf16]
    w_specs = [pl.BlockSpec((None, k, m), functools.partial(lambda i, r: (layer, r, 0), r=r),
                            pipeline_mode=pl.Buffered(1))
               for r in range(len(xs_bf16))]
    return pl.pallas_call(
        functools.partial(_matmul_residual_kernel, len(xs_bf16)),
        grid=(n // tm,),
        in_specs=x_specs + w_specs + [pl.BlockSpec((tm, m), lambda i: (i, 0))],
        out_specs=pl.BlockSpec((tm, m), lambda i: (i, 0)),
        out_shape=jax.ShapeDtypeStruct((n, m), F32),
        compiler_params=_params(("parallel",), vmem),
        name=name,
    )(*xs_bf16, *([w_bf16] * len(xs_bf16)), res)


def _mlp_kernel(has_final, *refs):
    if has_final:
        x_ref, g_ref, w1_ref, w2_ref, gf_ref, o_ref, xn_ref = refs
    else:
        x_ref, g_ref, w1_ref, w2_ref, o_ref, xn_ref = refs
    j = pl.program_id(1)

    @pl.when(j == 0)
    def _():
        x = x_ref[...]
        xn_ref[...] = _rms(x, g_ref[...]).astype(BF16)
        o_ref[...] = x

    a = jnp.maximum(_dot(xn_ref[...], w1_ref[...].astype(BF16)), 0.0)
    o_ref[...] += _dot((a * a).astype(BF16), w2_ref[...])

    if has_final:
        @pl.when(j == pl.num_programs(1) - 1)
        def _():
            o_ref[...] = _rms(o_ref[...], gf_ref[...])


def _mlp(h, g, w1, w2, layer, name, final_g=None):
    n, d = h.shape
    f = w1.shape[2]
    tm = _tile(n, 1024)
    tf = _tile(f, 512, V7X_LANES)
    has_final = final_g is not None
    vec = pl.BlockSpec((1, d), lambda i, j: (0, 0))
    in_specs = [pl.BlockSpec((tm, d), lambda i, j: (i, 0)), vec,
                pl.BlockSpec((None, d, tf), lambda i, j: (layer, 0, j)),
                pl.BlockSpec((None, tf, d), lambda i, j: (layer, j, 0))]
    args = [h, g.reshape(1, d), w1, w2]
    if has_final:
        in_specs.append(vec)
        args.append(final_g.reshape(1, d))
    vmem = (4 * _nbytes((tm, d), F32) + _nbytes((tm, d), BF16) + 2 * _nbytes((d, tf), w1.dtype)
            + _nbytes((d, tf), BF16) + 2 * _nbytes((tf, d), w2.dtype)
            + 3 * _nbytes((tm, tf), F32) + _nbytes((tm, d), F32))
    return pl.pallas_call(
        functools.partial(_mlp_kernel, has_final),
        grid=(n // tm, f // tf),
        in_specs=in_specs,
        out_specs=pl.BlockSpec((tm, d), lambda i, j: (i, 0)),
        out_shape=jax.ShapeDtypeStruct((n, d), F32),
        scratch_shapes=[pltpu.VMEM((tm, d), BF16)],
        compiler_params=_params(("parallel", "arbitrary"), vmem),
        name=name,
    )(*args)


def _retention_tables(blk, dh):
    log_g = jnp.log1p(-jnp.exp2(-5.0 - jnp.arange(RET_HEADS, dtype=F32)))
    pos = jnp.arange(blk, dtype=F32)
    n, m = pos[:, None], pos[None, :]
    cn, cm = jnp.floor(n / STREAM_CHUNK), jnp.floor(m / STREAM_CHUNK)
    dist = jnp.where(cn == cm, jnp.abs(n - m), n - m)
    lg = log_g[:, None, None]
    dmask = jnp.where((cm <= cn)[None], jnp.exp(lg * dist[None]), 0.0)
    qdec = jnp.broadcast_to(jnp.exp(lg * (pos + 1.0)[None, :, None]), (RET_HEADS, blk, dh))
    kdec = jnp.broadcast_to(jnp.exp(lg * (blk - 1.0 - pos)[None, :, None]), (RET_HEADS, blk, dh))
    cdec = jnp.broadcast_to(jnp.exp(lg * blk), (RET_HEADS, 1, dh))
    return dmask, qdec, kdec, cdec


def _rope_tables(t, dh):
    inv = 1.0 / (ROPE_BASE ** (jnp.arange(0, dh, 2, dtype=F32) / dh))
    ang = jnp.arange(t, dtype=F32)[:, None] * inv[None, :]
    return jnp.cos(ang), jnp.sin(ang)


def _retention_kernel(tiles_per_seq, x_ref, g_ref, w_ref, cos_ref, sin_ref, dm_ref, qd_ref, kd_ref,
                      cd_ref, o_ref, xn_ref, z_ref, s_ref):
    s = pl.program_id(0)
    ret_w = o_ref.shape[1]
    dh = ret_w // RET_HEADS
    half = dh // 2
    seq_start = jnp.maximum(s - 1, 0) % tiles_per_seq == 0

    @pl.when(s == 0)
    def _():
        z_ref[...] = jnp.zeros_like(z_ref)
        s_ref[...] = jnp.zeros_like(s_ref)

    xn_ref[...] = _rms(x_ref[...], g_ref[...]).astype(BF16)

    cos, sin = cos_ref[...], sin_ref[...]

    def rot(t):
        t1, t2 = t[:, :half], t[:, half:]
        return jnp.concatenate([t1 * cos - t2 * sin, t2 * cos + t1 * sin], axis=-1)

    for head in range(RET_HEADS):
        q_cols, k_cols, v_cols, g_cols = [
            slice(part * ret_w + head * dh, part * ret_w + (head + 1) * dh) for part in range(4)]
        q = rot(z_ref[:, q_cols])
        k = rot(z_ref[:, k_cols]) * (dh ** -0.5)
        vb = z_ref[:, v_cols].astype(BF16)
        gate = z_ref[:, g_cols]
        scores = _dot_nt(q.astype(BF16), k.astype(BF16))
        xn = xn_ref[...]
        for cols in (q_cols, k_cols, v_cols, g_cols):
            z_ref[:, cols] = _dot(xn, w_ref[:, cols])
        scores = scores * dm_ref[head]
        state = jnp.where(seq_start, 0.0, s_ref[head])
        out = _dot(scores.astype(BF16), vb) + _dot((q * qd_ref[head]).astype(BF16), state.astype(BF16))
        s_ref[head] = state * cd_ref[head] + _dot_tn((k * kd_ref[head]).astype(BF16), vb)

        mu = jnp.mean(out, axis=-1, keepdims=True)
        cen = out - mu
        var = jnp.mean(cen * cen, axis=-1, keepdims=True)
        y = cen * lax.rsqrt(var + RET_GN_EPS)
        o_ref[:, head * dh:(head + 1) * dh] = (gate * _sigmoid(gate) * y).astype(BF16)


def _retention(h, norm_g, w_ret_bf16, t, ret_w, cos, sin):
    n, d = h.shape
    dh = ret_w // RET_HEADS
    assert w_ret_bf16.shape == (d, 4 * ret_w)
    tm = _tile(t, 256, STREAM_CHUNK)
    n_tiles = n // tm
    tiles_per_seq = t // tm
    tables = _retention_tables(tm, dh)
    once = pl.Buffered(1)
    full = lambda a: pl.BlockSpec(a.shape, lambda s: (0,) * a.ndim, pipeline_mode=once)
    rope_spec = pl.BlockSpec((tm, dh // 2), lambda s: (jnp.maximum(s - 1, 0) % tiles_per_seq, 0))
    g2 = norm_g.reshape(1, d)
    vmem = (_nbytes(w_ret_bf16.shape, BF16) + sum(_nbytes(a.shape, F32) for a in tables)
            + 2 * _nbytes((tm, d), F32) + _nbytes((tm, d), BF16) + 2 * _nbytes((tm, ret_w), BF16)
            + 4 * _nbytes((tm, dh // 2), F32) + _nbytes((tm, 4 * ret_w), F32)
            + _nbytes((RET_HEADS, dh, dh), F32) + 2 * _nbytes((tm, d), F32)
            + 16 * _nbytes((tm, max(tm, dh)), F32))
    return pl.pallas_call(
        functools.partial(_retention_kernel, tiles_per_seq),
        grid=(n_tiles + 1,),
        in_specs=[pl.BlockSpec((tm, d), lambda s: (jnp.minimum(s, n_tiles - 1), 0)),
                  full(g2), full(w_ret_bf16), rope_spec, rope_spec] + [full(a) for a in tables],
        out_specs=pl.BlockSpec((tm, ret_w), lambda s: (jnp.maximum(s - 1, 0), 0)),
        out_shape=jax.ShapeDtypeStruct((n, ret_w), BF16),
        scratch_shapes=[pltpu.VMEM((tm, d), BF16), pltpu.VMEM((tm, 4 * ret_w), F32),
                        pltpu.VMEM((RET_HEADS, dh, dh), F32)],
        compiler_params=_params(("arbitrary",), vmem),
        name="retention",
    )(h, g2, w_ret_bf16, cos, sin, *tables)


def _rwkv_in_kernel(has_vres, tiles_per_seq, rw, ranks, windows, *refs):
    if has_vres:
        (x_ref, g_ref, w_ref, mu_ref, wlw_ref, wla_ref, wlg_ref, w0_ref, a0_ref, kk_ref, ka_ref,
         tri_ref, vf_ref, v0_ref, v1_ref, v2_ref,
         r_out, lw_out, c_out, k_out, v_out, p_out, a_out, g_out, xn_ref, z_ref, carry_ref) = refs
    else:
        (x_ref, g_ref, w_ref, mu_ref, wlw_ref, wla_ref, wlg_ref, w0_ref, a0_ref, kk_ref, ka_ref,
         tri_ref,
         r_out, lw_out, c_out, k_out, v_out, p_out, a_out, g_out, xn_ref, z_ref, carry_ref) = refs
    s = pl.program_id(0)
    tm = z_ref.shape[0]
    at_start = jnp.maximum(s - 1, 0) % tiles_per_seq == 0
    row0 = lax.broadcasted_iota(jnp.int32, (tm, 1), 0) == 0

    @pl.when(s == 0)
    def _():
        z_ref[...] = jnp.zeros_like(z_ref)
        carry_ref[...] = jnp.zeros_like(carry_ref)

    xn_ref[...] = _rms(x_ref[...], g_ref[...]).astype(BF16)

    def shift_mix_then_project(lo, hi):
        x = z_ref[:, lo:hi]
        last = jnp.where(at_start, 0.0, carry_ref[:, lo:hi])
        prev = jnp.where(row0, last, pltpu.roll(x, 1, 0))
        carry_ref[:, lo:hi] = x[tm - 1:tm, :]
        mixed = x + mu_ref[:, lo:hi] * (prev - x)
        z_ref[:, lo:hi] = _dot(xn_ref[...], w_ref[:, lo:hi])
        return mixed

    rank_w, rank_a = ranks
    lora = shift_mix_then_project(3 * rw, z_ref.shape[1])
    lane = lax.broadcasted_iota(jnp.int32, (1, lora.shape[1]), 1)
    feat = jnp.where(lane < rank_w, jnp.tanh(lora),
                     jnp.where(lane < rank_w + rank_a, lora, _sigmoid(lora))).astype(BF16)
    (w_lo, w_hi), (a_lo, a_hi), (g_lo, g_hi) = windows
    proj_w = _dot(feat[:, w_lo:w_hi], wlw_ref[...])
    proj_a = _dot(feat[:, a_lo:a_hi], wla_ref[...])
    gate = _dot(feat[:, g_lo:g_hi], wlg_ref[...])

    r = shift_mix_then_project(0, rw)
    log_w = -math.exp(-0.5) * _sigmoid(w0_ref[...] + proj_w)
    iclr = _sigmoid(a0_ref[...] + proj_a)

    vr = shift_mix_then_project(2 * rw, 3 * rw)
    if has_vres:
        low = _dot(vr.astype(BF16), v1_ref[...])
        mix = _sigmoid(v0_ref[...] + _dot(low.astype(BF16), v2_ref[...]))
        vr = vr + (vf_ref[...] - vr) * mix

    kr = shift_mix_then_project(rw, 2 * rw)
    kk = kr * kk_ref[...]
    pair = lax.broadcasted_iota(jnp.int32, (RWKV_PAIR, RWKV_PAIR), 0) // RWKV_HEAD_DIM
    ones_bd = (pair == pair.T).astype(BF16)
    sq = kk * kk
    norm = jnp.sqrt(jnp.concatenate(
        [_seg_sum(sq[:, i:i + RWKV_PAIR], ones_bd) for i in range(0, rw, RWKV_PAIR)], axis=1))
    kk = kk / jnp.maximum(norm, 1e-12)
    k2 = kr * (1.0 + (iclr - 1.0) * ka_ref[...])

    l_hi, l_mid, l_lo = _split3(log_w)
    tri = tri_ref[...]
    c_out[...] = _dot(tri, l_hi) + _dot(tri, l_mid) + _dot(tri, l_lo)

    r_out[...] = r
    lw_out[...] = log_w
    k_out[...] = k2
    v_out[...] = vr
    p_out[...] = kk
    a_out[...] = iclr
    g_out[...] = gate


def _lane_window(lo, hi):
    return (lo // V7X_LANES) * V7X_LANES, -(-hi // V7X_LANES) * V7X_LANES


def _rwkv_in(h, norm_g, w_in, t, ret_in, rw, mu, w2, a2, g2, w0, a0, k_k, k_a, v_first, v_res):
    n, d = h.shape
    has_vres = v_res is not None
    rank_w, rank_a, rank_g = w2.shape[0], a2.shape[0], g2.shape[0]
    lora = rank_w + rank_a + rank_g
    assert lora <= LORA_PAD and rw % RWKV_PAIR == 0 and w_in.shape[1] == ret_in + 3 * rw + lora
    width = 3 * rw + LORA_PAD
    tm = _tile(t, 256, RWKV_CHUNK)
    n_tiles = n // tm
    pos = jnp.arange(tm)
    tri = ((pos[:, None] // RWKV_CHUNK == pos[None, :] // RWKV_CHUNK)
           & (pos[:, None] >= pos[None, :])).astype(BF16)
    row2 = lambda v: v.reshape(1, -1)
    w_rw = jnp.pad(w_in[:, ret_in:], ((0, 0), (0, LORA_PAD - lora))).astype(BF16)
    mu_p = jnp.pad(mu, (0, LORA_PAD - lora)).reshape(1, width)
    bounds = [(0, rank_w), (rank_w, rank_w + rank_a), (rank_w + rank_a, lora)]
    windows = tuple(_lane_window(lo, hi) for lo, hi in bounds)

    def embed(w_low, bound, window):
        rows = jnp.zeros((window[1] - window[0], rw), F32)
        return rows.at[bound[0] - window[0]:bound[1] - window[0]].set(w_low).astype(BF16)

    wl = [embed(wg, bd, win) for wg, bd, win in zip((w2, a2, g2), bounds, windows)]
    once = pl.Buffered(1)
    vec = pl.BlockSpec((1, rw), lambda s: (0, 0), pipeline_mode=once)
    full = lambda a: pl.BlockSpec(a.shape, lambda s: (0,) * a.ndim, pipeline_mode=once)
    prev_tile = pl.BlockSpec((tm, rw), lambda s: (jnp.maximum(s - 1, 0), 0))
    args = [h, row2(norm_g), w_rw, mu_p, *wl, row2(w0), row2(a0), row2(k_k), row2(k_a), tri]
    specs = [pl.BlockSpec((tm, d), lambda s: (jnp.minimum(s, n_tiles - 1), 0)),
             full(row2(norm_g)), full(w_rw), full(mu_p), *[full(w) for w in wl],
             vec, vec, vec, vec, full(tri)]
    resident = _nbytes(w_rw.shape, BF16) + sum(_nbytes(w.shape, BF16) for w in wl) + _nbytes(tri.shape, BF16)
    if has_vres:
        v0, v1, v2 = v_res
        rank_v = v1.shape[1]
        rank_pad = _lane_window(0, rank_v)[1]
        v1p = jnp.zeros((rw, rank_pad), F32).at[:, :rank_v].set(v1).astype(BF16)
        v2p = jnp.zeros((rank_pad, rw), F32).at[:rank_v, :].set(v2).astype(BF16)
        args += [v_first, row2(v0), v1p, v2p]
        specs += [prev_tile, vec, full(v1p), full(v2p)]
        resident += 2 * _nbytes(v1p.shape, BF16)
    vmem = (resident + 2 * _nbytes((tm, d), F32) + 2 * (8 + has_vres) * _nbytes((tm, rw), F32)
            + _nbytes((tm, d), BF16) + _nbytes((tm, width), F32) + 10 * _nbytes((tm, rw), F32))
    return pl.pallas_call(
        functools.partial(_rwkv_in_kernel, has_vres, t // tm, rw, (rank_w, rank_a), windows),
        grid=(n_tiles + 1,),
        in_specs=specs,
        out_specs=[prev_tile] * 8,
        out_shape=[jax.ShapeDtypeStruct((n, rw), F32)] * 8,
        scratch_shapes=[pltpu.VMEM((tm, d), BF16), pltpu.VMEM((tm, width), F32),
                        pltpu.VMEM((1, width), F32)],
        compiler_params=_params(("arbitrary",), vmem),
        name="rwkv_in",
    )(*args)


def _rwkv_core_kernel(r_ref, lw_ref, c_ref, k_ref, v_ref, p_ref, a_ref, g_ref, rk_ref, lnw_ref,
                      lnb_ref, o_ref, h_ref, y_ref):
    @pl.when(pl.program_id(2) == 0)
    def _():
        h_ref[...] = jnp.zeros_like(h_ref)

    L = RWKV_CHUNK
    W = RWKV_PAIR
    hd = RWKV_HEAD_DIM
    n_chunks = r_ref.shape[0] // L
    n_pairs = r_ref.shape[1] // W

    ri = lax.broadcasted_iota(jnp.int32, (W, W), 0)
    ci = lax.broadcasted_iota(jnp.int32, (W, W), 1)
    same_head = (ri // hd) == (ci // hd)
    strict = same_head & ((ri % hd) > (ci % hd))
    incl = same_head & ((ri % hd) >= (ci % hd))
    eye = (ri == ci).astype(F32)
    merge_masks = []
    s = 1
    while s < hd:
        merge_masks.append(((ri // (2 * s)) == (ci // (2 * s))) & ((ri // s) != (ci // s)))
        s *= 2
    ones_bd = same_head.astype(BF16)
    bd_mask = (lax.broadcasted_iota(jnp.int32, (2 * L, W), 0) // L) == (
        lax.broadcasted_iota(jnp.int32, (2 * L, W), 1) // hd)

    def bd(x):
        return jnp.where(bd_mask, jnp.concatenate([x, x], axis=0), 0.0)

    def fold(x):
        return x[:L, :] + x[L:, :]

    def pair_chunk(sl, g):
        ln = slice(g * W, (g + 1) * W)
        r, lw, k, v, p = r_ref[sl, ln], lw_ref[sl, ln], k_ref[sl, ln], v_ref[sl, ln], p_ref[sl, ln]
        q = p * a_ref[sl, ln]
        c = c_ref[sl, ln]
        c_last = c[L - 1:L, :]
        e_in = jnp.exp(c)
        e_out = jnp.exp(-c)
        e_end = jnp.exp(c_last - c)
        rt, kt, qt = r * e_in, k * e_out, q * e_out
        pt = p * jnp.exp(c - lw)
        kh, qh = k * e_end, q * e_end

        pt_bd, rt_bd = bd(pt), bd(rt)
        lhs = jnp.concatenate([pt_bd, rt_bd], axis=0).astype(BF16)
        rhs = jnp.concatenate([bd(qt), bd(kt)], axis=0).astype(BF16)
        m = _dot_nt(lhs, rhs)
        yield
        a_pq = jnp.where(strict, m[:W, :W], 0.0)
        a_pk = jnp.where(strict, m[:W, W:], 0.0).astype(BF16)
        a_rq = jnp.where(incl, m[W:, :W], 0.0).astype(BF16)
        a_rk = jnp.where(incl, m[W:, W:], 0.0).astype(BF16)
        v_bd = bd(v).astype(BF16)
        pk_v = _dot(a_pk, v_bd)
        rk_v = _dot(a_rk, v_bd)
        yield

        t_inv = eye - jnp.where(merge_masks[0], a_pq, 0.0)
        for mask in merge_masks[1:]:
            tb = t_inv.astype(BF16)
            inner = _dot(jnp.where(mask, a_pq, 0.0).astype(BF16), tb)
            yield
            t_inv = t_inv - _dot(tb, inner.astype(BF16))
            yield
        t_inv = t_inv.astype(BF16)

        sol = _dot(t_inv, jnp.concatenate([pt_bd, pk_v], axis=1).astype(BF16)).astype(BF16)
        yield
        rq_sol = _dot(a_rq, sol)
        r_hat = fold(rt_bd - rq_sol[:, :W])
        y0 = fold(rk_v - rq_sol[:, W:])
        qh_sol = _dot_tn(bd(qh).astype(BF16), sol)
        g_mat = eye * jnp.exp(c_last) - qh_sol[:, :W]
        h_add = _dot_tn(bd(kh).astype(BF16), v_bd) - qh_sol[:, W:]
        yield

        h_hi, h_lo = _split2(h_ref[g])
        y_ref[sl, ln] = _dot(r_hat.astype(BF16), h_hi) + y0
        g_hi, g_lo = _split2(g_mat)
        gh = _dot(g_hi, jnp.concatenate([h_hi, h_lo], axis=1))
        h_ref[g] = gh[:, :W] + gh[:, W:] + _dot(g_lo, h_hi) + h_add

    chunks_per_step = 2 if n_chunks % 2 == 0 else 1

    def chunk_group(i, carry):
        live = []
        for j in range(chunks_per_step):
            sl = pl.ds(pl.multiple_of((i * chunks_per_step + j) * L, L), L)
            live += [pair_chunk(sl, g) for g in range(n_pairs)]
        while live:
            live = [gen for gen in live if next(gen, True) is None]
        return carry

    lax.fori_loop(0, n_chunks // chunks_per_step, chunk_group, 0)

    inv_hd = 1.0 / hd
    lanes = [slice(g * W, (g + 1) * W) for g in range(n_pairs)]
    ys = [y_ref[:, ln] for ln in lanes]
    mus = [_seg_sum(y, ones_bd) * inv_hd for y in ys]
    cens = [y - mu for y, mu in zip(ys, mus)]
    bonus_sums = [_seg_sum(r_ref[:, ln] * k_ref[:, ln] * rk_ref[:, ln], ones_bd) for ln in lanes]
    variances = [_seg_sum(cen * cen, ones_bd) * inv_hd for cen in cens]
    for ln, cen, var, bsum in zip(lanes, cens, variances, bonus_sums):
        yn = cen * lax.rsqrt(var + RWKV_LN_EPS) * lnw_ref[:, ln] + lnb_ref[:, ln]
        o_ref[:, ln] = ((yn + bsum * v_ref[:, ln]) * g_ref[:, ln]).astype(BF16)


def _rwkv_core(r, lw, c, k, v, p, a, gate, r_k, lnx_w, lnx_b, b, t):
    n, rw = r.shape
    assert rw % RWKV_PAIR == 0 and t % RWKV_CHUNK == 0
    group = _tile(rw, RWKV_GROUP_LANES, RWKV_PAIR)
    blk = _tile(t, 256, RWKV_CHUNK)
    nblk = t // blk
    tok = pl.BlockSpec((blk, group), lambda bi, pi, ci: (bi * nblk + ci, pi))
    vec = pl.BlockSpec((1, group), lambda bi, pi, ci: (0, pi))
    row2 = lambda x: x.reshape(1, rw)
    vmem = (2 * 8 * _nbytes((blk, group), F32) + 2 * _nbytes((blk, group), BF16)
            + _nbytes((blk, group), F32)
            + 64 * (group // RWKV_PAIR) * _nbytes((2 * RWKV_PAIR, 2 * RWKV_PAIR), F32)
            + 8 * _nbytes((blk, group), F32))
    return pl.pallas_call(
        _rwkv_core_kernel,
        grid=(b, rw // group, nblk),
        in_specs=[tok] * 8 + [vec] * 3,
        out_specs=tok,
        out_shape=jax.ShapeDtypeStruct((n, rw), BF16),
        scratch_shapes=[pltpu.VMEM((group // RWKV_PAIR, RWKV_PAIR, RWKV_PAIR), F32),
                        pltpu.VMEM((blk, group), F32)],
        compiler_params=_params(("parallel", "parallel", "arbitrary"), vmem),
        name="rwkv_core",
    )(r, lw, c, k, v, p, a, gate, row2(r_k), row2(lnx_w), row2(lnx_b))


def _lru_conv_gates(x, tail_prev, cw, cb, gxw, gxb, gaw, gab):
    lt, tail = x.shape[0], tail_prev.shape[0]
    ext = jnp.concatenate([tail_prev, x], axis=0)
    xc = cb + cw[CONV_WIDTH - 1:CONV_WIDTH, :] * x
    for j in range(1, CONV_WIDTH):
        xc = xc + cw[CONV_WIDTH - 1 - j:CONV_WIDTH - j, :] * pltpu.roll(ext, j, 0)[tail:, :]
    xcb = xc.astype(BF16)
    return xc, _dot(xcb, gxw) + gxb, _dot(xcb, gaw) + gab, x[lt - tail:, :]


def _lru_scan(xc, pre_x, pre_a, lam, h_prev):
    lt = xc.shape[0]
    gate_x = _sigmoid(pre_x)
    gate_a = _sigmoid(pre_a)
    neg_lam = -lam
    softplus = jnp.maximum(neg_lam, 0.0) + jnp.log(1.0 + jnp.exp(-jnp.abs(neg_lam)))
    log_a = -LRU_C * gate_a * softplus
    a = jnp.exp(log_a)
    bb = xc * gate_x * jnp.sqrt(-jnp.tanh(log_a) * (a * a + 1.0))

    sub = V7X_SUBLANES
    n_groups = lt // sub
    a = a.reshape(n_groups, sub, a.shape[1])
    bb = bb.reshape(n_groups, sub, bb.shape[1])
    row_in_group = lax.broadcasted_iota(jnp.int32, (1, sub, 1), 1)
    s = 1
    while s < sub:
        valid = row_in_group >= s
        b_prev = jnp.where(valid, pltpu.roll(bb, s, 1), 0.0)
        a_prev = jnp.where(valid, pltpu.roll(a, s, 1), 1.0)
        bb = bb + a * b_prev
        a = a * a_prev
        s *= 2
    groups = []
    for gi in range(n_groups):
        hg = a[gi] * h_prev + bb[gi]
        groups.append(hg)
        h_prev = hg[sub - 1:sub, :]
    return jnp.concatenate(groups, axis=0), h_prev


def _odd_mixer_kernel(tiles_per_seq, x_ref, g_ref, w_ref, cw_ref, cb_ref, gxw_ref, gxb_ref,
                      gaw_ref, gab_ref, lam_ref, o_ref, xn_ref, zy_ref, zx_ref, xtail_ref, h_ref):
    s = pl.program_id(0)
    n_tiles = pl.num_programs(0) - 1
    d = zy_ref.shape[1]
    seq_start = jnp.maximum(s - 1, 0) % tiles_per_seq == 0

    @pl.when(s == 0)
    def _():
        zy_ref[...] = jnp.zeros_like(zy_ref)
        zx_ref[...] = jnp.zeros_like(zx_ref)
        xtail_ref[...] = jnp.zeros_like(xtail_ref)
        h_ref[...] = jnp.zeros_like(h_ref)

    @pl.when(s < n_tiles)
    def _():
        xn_ref[...] = _rms(x_ref[...], g_ref[...]).astype(BF16)

    for blk in range(d // LRU_BLOCK):
        cols = slice(blk * LRU_BLOCK, (blk + 1) * LRU_BLOCK)
        xcols = slice(d + blk * LRU_BLOCK, d + (blk + 1) * LRU_BLOCK)
        tail_prev = jnp.where(seq_start, 0.0, xtail_ref[:, cols])
        h_prev = jnp.where(seq_start, 0.0, h_ref[:, cols])
        y = zy_ref[:, cols]
        xc, pre_x, pre_a, tail_new = _lru_conv_gates(
            zx_ref[:, cols], tail_prev, cw_ref[:, cols], cb_ref[:, cols], gxw_ref[blk],
            gxb_ref[:, cols], gaw_ref[blk], gab_ref[:, cols])
        gelu = 0.5 * y * (1.0 + jnp.tanh(math.sqrt(2.0 / math.pi) * (y + 0.044715 * (y * y * y))))
        xn = xn_ref[...]
        zy_ref[:, cols] = _dot(xn, w_ref[:, cols])
        zx_ref[:, cols] = _dot(xn, w_ref[:, xcols])
        hs, h_new = _lru_scan(xc, pre_x, pre_a, lam_ref[:, cols], h_prev)
        o_ref[:, cols] = (gelu * hs).astype(BF16)
        xtail_ref[:, cols] = tail_new
        h_ref[:, cols] = h_new


def _odd_mixer(h, norm_g, w_in_bf16, layer, t, conv_w, conv_b, gx_w, gx_b, ga_w, ga_b, lam):
    n, d = h.shape
    assert d % LRU_BLOCK == 0 and w_in_bf16.shape[2] == 2 * d
    nb = d // LRU_BLOCK
    tm = _tile(t, 256)
    n_tiles = n // tm
    tail = V7X_SUBLANES
    assert CONV_WIDTH - 1 <= tail <= tm
    once = pl.Buffered(1)
    vec = lambda rows: pl.BlockSpec((rows, d), lambda s: (0, 0), pipeline_mode=once)
    wsp = pl.BlockSpec((nb, LRU_BLOCK, LRU_BLOCK), lambda s: (0, 0, 0), pipeline_mode=once)
    row2 = lambda x: x.reshape(1, d)
    vmem = (_nbytes((d, 2 * d), BF16) + 2 * _nbytes((nb, LRU_BLOCK, LRU_BLOCK), BF16)
            + 2 * _nbytes((tm, d), F32) + _nbytes((tm, d), BF16) + 2 * _nbytes((tm, d), BF16)
            + 2 * _nbytes((tm, d), F32) + 2 * _nbytes((tm, d), F32)
            + 28 * _nbytes((tm, LRU_BLOCK), F32))
    return pl.pallas_call(
        functools.partial(_odd_mixer_kernel, t // tm),
        grid=(n_tiles + 1,),
        in_specs=[pl.BlockSpec((tm, d), lambda s: (jnp.minimum(s, n_tiles - 1), 0)),
                  vec(1),
                  pl.BlockSpec((None, d, 2 * d), lambda s: (layer, 0, 0), pipeline_mode=once),
                  vec(CONV_WIDTH), vec(1), wsp, vec(1), wsp, vec(1), vec(1)],
        out_specs=pl.BlockSpec((tm, d), lambda s: (jnp.maximum(s - 1, 0), 0)),
        out_shape=jax.ShapeDtypeStruct((n, d), BF16),
        scratch_shapes=[pltpu.VMEM((tm, d), BF16), pltpu.VMEM((tm, d), F32), pltpu.VMEM((tm, d), F32),
                        pltpu.VMEM((tail, d), F32), pltpu.VMEM((1, d), F32)],
        compiler_params=_params(("arbitrary",), vmem),
        name="odd_mixer",
    )(h, norm_g.reshape(1, d), w_in_bf16, conv_w, row2(conv_b), gx_w.astype(BF16), row2(gx_b),
      ga_w.astype(BF16), row2(ga_b), row2(lam))


def _even_layer(h, b, t, e, norm_g, w_in_all, w_out_all, mu, w0, w2, a0, a2, g2, k_k, k_a, r_k, lnx_w,
                lnx_b, v_first, v_res, rope):
    d = h.shape[1]
    ret_w = d // 2
    rw = d - ret_w
    ret_in = 4 * ret_w
    w_in = w_in_all[e]
    out_ret = _retention(h, norm_g, w_in[:, :ret_in].astype(BF16), t, ret_w, *rope)
    r, lw, c, k, v, p, a, gate = _rwkv_in(h, norm_g, w_in, t, ret_in, rw, mu, w2, a2, g2, w0, a0,
                                          k_k, k_a, v_first, v_res)
    out_rw = _rwkv_core(r, lw, c, k, v, p, a, gate, r_k.reshape(-1), lnx_w, lnx_b, b, t)
    h = _matmul_residual([out_ret, out_rw], w_out_all, e, h, "even_out_proj")
    return h, (v if v_res is None else v_first)


def _odd_layer(h, t, o, norm_g, w_in_b, conv_w, conv_b, gx_w, gx_b, ga_w, ga_b, lam, w_out_b):
    gated = _odd_mixer(h, norm_g, w_in_b, o, t, conv_w, conv_b, gx_w, gx_b, ga_w, ga_b, lam)
    return _matmul_residual([gated], w_out_b, o, h, "odd_out_proj")


def kernel(x, ev_norm, ev_w_in, ev_w_out, rw_mu, rw_w0, rw_w2, rw_a0, rw_a2, rw_g2, rw_k_k, rw_k_a, rw_r_k, rw_lnx_w, rw_lnx_b, rw_v0, rw_v1, rw_v2, od_norm, od_w_in, od_conv_w, od_conv_b, od_gx_w, od_gx_b, od_ga_w, od_ga_b, od_lam, od_w_out, ff_norm, ff_w1, ff_w2, final_norm):
    b, t, d = x.shape
    depth = ff_norm.shape[0]
    h = x.reshape(b * t, d)
    rope = _rope_tables(t, (d // 2) // RET_HEADS)
    ff_w2_b = ff_w2.astype(BF16)
    od_w_in_b, od_w_out_b, ev_w_out_b = od_w_in.astype(BF16), od_w_out.astype(BF16), ev_w_out.astype(BF16)
    v_first = None
    for layer in range(depth):
        if layer % 2 == 0:
            e = layer // 2
            v_res = None if e == 0 else (rw_v0[e - 1], rw_v1[e - 1], rw_v2[e - 1])
            h, v_first = _even_layer(
                h, b, t, e, ev_norm[e], ev_w_in, ev_w_out_b, rw_mu[e], rw_w0[e], rw_w2[e],
                rw_a0[e], rw_a2[e], rw_g2[e], rw_k_k[e], rw_k_a[e], rw_r_k[e], rw_lnx_w[e],
                rw_lnx_b[e], v_first, v_res, rope)
        else:
            o = layer // 2
            h = _odd_layer(h, t, o, od_norm[o], od_w_in_b, od_conv_w[o], od_conv_b[o], od_gx_w[o],
                           od_gx_b[o], od_ga_w[o], od_ga_b[o], od_lam[o], od_w_out_b)
        h = _mlp(h, ff_norm[layer], ff_w1, ff_w2_b, layer, f"mlp_{layer}",
                 final_g=final_norm if layer == depth - 1 else None)
    return h.reshape(b, t, d)
```
